```python
import jax
import jax.numpy as jnp
from jax import lax
import numpy as np

D_MODEL = 1024
BATCH = 2
SEQ = 8192
DEPTH = 1
DEC_BATCH = 32
DEC_SEQ = 8
PAST_LEN = 8192
PAGE_SIZE = 128

N_HEADS_NSA = 8
HEAD_DIM_NSA = 64
N_KV_NSA = 2
GQA_REP = N_HEADS_NSA // N_KV_NSA
D_NSA = N_HEADS_NSA * HEAD_DIM_NSA
CMP_STRIDE = 16
CMP_BLOCK = 2 * CMP_STRIDE
SEL_BLOCK = 64
SEL_CHUNKS = SEL_BLOCK // CMP_STRIDE
N_SEL = 16
WINDOW = 512
Q_BLOCK = 128
FORCE_BONUS = 100.0
N_HEADS_RET = 4
DK_RET = 64
DV_RET = 128
D_RET = N_HEADS_RET * DV_RET
RET_CHUNK = 128
D_MIX = D_NSA + D_RET
D_FF = 2816
CONV_W = 3
ROPE_THETA = 10000.0
EPS = 1e-6
NEG_INF = -1e30
IN_SIZES = (D_NSA, 6 * N_KV_NSA * HEAD_DIM_NSA, 3 * N_HEADS_NSA, N_HEADS_RET * DK_RET, N_HEADS_RET * DK_RET, D_RET, D_RET)
D_IN = sum(IN_SIZES)

kernel_name = 'hymba_nsa_retention_convffn_step'


def split_cols(a):
    offs, acc = [], 0
    for s in IN_SIZES[:-1]:
        acc += s
        offs.append(acc)
    return jnp.split(a, offs, axis=-1)


def rmsnorm(x, g):
    x32 = x.astype(jnp.float32)
    y = x32 * lax.rsqrt(jnp.mean(x32 * x32, axis=-1, keepdims=True) + EPS)
    return (y * g.astype(jnp.float32)).astype(x.dtype)


def rope(x, pos):
    half = x.shape[-1] // 2
    inv = ROPE_THETA ** (-jnp.arange(half, dtype=jnp.float32) / half)
    ang = pos.astype(jnp.float32)[:, None] * inv[None, :]
    cos = jnp.cos(ang)[None, :, None, :]
    sin = jnp.sin(ang)[None, :, None, :]
    x32 = x.astype(jnp.float32)
    x1, x2 = x32[..., :half], x32[..., half:]
    return jnp.concatenate([x1 * cos - x2 * sin, x2 * cos + x1 * sin], axis=-1).astype(x.dtype)


def masked_softmax(s, mask):
    s = jnp.where(mask, s.astype(jnp.float32), NEG_INF)
    m = jnp.max(s, axis=-1, keepdims=True)
    e = jnp.where(mask, jnp.exp(s - m), 0.0)
    return e / jnp.maximum(jnp.sum(e, axis=-1, keepdims=True), 1e-30)


def compress(x, pos_emb, w1, w2):
    B, T, G, Dh = x.shape
    n = T // CMP_STRIDE
    xr = x[:, :n * CMP_STRIDE].reshape(B, n, CMP_STRIDE, G, Dh)
    blocks = jnp.concatenate([xr[:, :-1], xr[:, 1:]], axis=2) + pos_emb[None, None, :, None, :]
    flat = jnp.moveaxis(blocks, 3, 2).reshape(B, n - 1, G, CMP_BLOCK * Dh)
    return jax.nn.gelu(flat @ w1) @ w2


def sel_blocks(x):
    B, T, G, Dh = x.shape
    ns = -(-T // SEL_BLOCK)
    x = jnp.pad(x, ((0, 0), (0, ns * SEL_BLOCK - T), (0, 0), (0, 0)))
    return jnp.transpose(x.reshape(B, ns, SEL_BLOCK, G, Dh), (0, 3, 1, 2, 4))


def nsa_context(kv_full, lw):
    kc = compress(kv_full[:, :, 0], lw['cmp_pos_k'], lw['cmp_w1_k'], lw['cmp_w2_k'])
    vc = compress(kv_full[:, :, 1], lw['cmp_pos_v'], lw['cmp_w1_v'], lw['cmp_w2_v'])
    c_end = jnp.arange(kc.shape[1], dtype=jnp.int32) * CMP_STRIDE + (CMP_BLOCK - 1)
    return kc, vc, c_end, sel_blocks(kv_full[:, :, 2]), sel_blocks(kv_full[:, :, 3])


def nsa_block(q, q_pos, gates, kc, vc, c_end, ks_g, vs_g, kw, vw, kw_pos):
    B, Tq, G, R, Dh = q.shape
    scale = HEAD_DIM_NSA ** -0.5
    s = jnp.einsum('btgrd,bngd->bgrtn', q, kc) * scale
    p_cmp = masked_softmax(s, c_end[None, :] <= q_pos[:, None])
    o_cmp = jnp.einsum('bgrtn,bngd->btgrd', p_cmp.astype(vc.dtype), vc)
    ns = ks_g.shape[2]
    zpad = ((0, 0),) * 4
    chunk = jnp.pad(p_cmp, zpad + ((0, 1),)) + jnp.pad(p_cmp, zpad + ((1, 0),))
    chunk = jnp.pad(chunk, zpad + ((0, ns * SEL_CHUNKS - chunk.shape[-1]),))
    imp = chunk.reshape(B, G, R, Tq, ns, SEL_CHUNKS).sum(-1).sum(2)
    blk = jnp.arange(ns, dtype=jnp.int32)[None, :]
    q_blk = (q_pos // SEL_BLOCK)[:, None]
    valid = blk * SEL_BLOCK <= q_pos[:, None]
    forced = (blk == 0) | (blk == q_blk) | (blk == q_blk - 1)
    score = jnp.where(valid, imp + FORCE_BONUS * forced.astype(jnp.float32), NEG_INF)
    _, idx = lax.top_k(score, min(N_SEL, ns))
    n_sel = idx.shape[-1]
    sel_valid = idx * SEL_BLOCK <= q_pos[None, None, :, None]
    gather = jax.vmap(jax.vmap(lambda a, i: a[i]))
    kb = gather(ks_g, idx)
    vb = gather(vs_g, idx)
    tok_pos = idx[..., None] * SEL_BLOCK + jnp.arange(SEL_BLOCK, dtype=jnp.int32)
    m_tot = n_sel * SEL_BLOCK
    smask = (sel_valid[..., None] & (tok_pos <= q_pos[None, None, :, None, None])).reshape(B, G, 1, Tq, m_tot)
    s = jnp.einsum('btgrd,bgtnsd->bgrtns', q, kb).reshape(B, G, R, Tq, m_tot) * scale
    p = masked_softmax(s, smask)
    o_slc = jnp.einsum('bgrtm,bgtmd->btgrd', p.astype(vb.dtype), vb.reshape(B, G, Tq, m_tot, Dh))
    dpos = q_pos[:, None] - kw_pos[None, :]
    wmask = (dpos >= 0) & (dpos < WINDOW) & (kw_pos[None, :] >= 0)
    s = jnp.einsum('btgrd,bkgd->bgrtk', q, kw) * scale
    p = masked_softmax(s, wmask)
    o_win = jnp.einsum('bgrtk,bkgd->btgrd', p.astype(vw.dtype), vw)
    g = jax.nn.sigmoid(gates.astype(jnp.float32)).astype(q.dtype)[..., None]
    o = g[:, :, 0] * o_cmp + g[:, :, 1] * o_slc + g[:, :, 2] * o_win
    return o.reshape(B, Tq, G * R * Dh)


def nsa_prompt(q, gates, kv_rows, win_rows, lw):
    B, T = q.shape[:2]
    kc, vc, c_end, ks_g, vs_g = nsa_context(kv_rows, lw)
    pad = ((0, 0), (WINDOW, 0), (0, 0), (0, 0))
    kw_pad = jnp.pad(win_rows[:, :, 0], pad)
    vw_pad = jnp.pad(win_rows[:, :, 1], pad)
    nq = T // Q_BLOCK

    def body(args):
        q_b, g_b, start = args
        q_pos = start + jnp.arange(Q_BLOCK, dtype=jnp.int32)
        kw = lax.dynamic_slice_in_dim(kw_pad, start, WINDOW + Q_BLOCK, axis=1)
        vw = lax.dynamic_slice_in_dim(vw_pad, start, WINDOW + Q_BLOCK, axis=1)
        kw_pos = start - WINDOW + jnp.arange(WINDOW + Q_BLOCK, dtype=jnp.int32)
        return nsa_block(q_b, q_pos, g_b, kc, vc, c_end, ks_g, vs_g, kw, vw, kw_pos)

    qs = jnp.moveaxis(q.reshape(B, nq, Q_BLOCK, *q.shape[2:]), 1, 0)
    gs = jnp.moveaxis(gates.reshape(B, nq, Q_BLOCK, *gates.shape[2:]), 1, 0)
    starts = jnp.arange(nq, dtype=jnp.int32) * Q_BLOCK
    out = lax.map(body, (qs, gs, starts))
    return jnp.moveaxis(out, 0, 1).reshape(B, T, D_NSA)


def retention_chunkwise(q, k, v, s0, chunk):
    B, T, H, Dk = q.shape
    Dv = v.shape[-1]
    n = T // chunk
    log_g = jnp.log1p(-jnp.power(2.0, -5.0 - jnp.arange(H, dtype=jnp.float32)))
    i = jnp.arange(chunk, dtype=jnp.float32)
    diff = i[:, None] - i[None, :]
    dmat = jnp.where(diff >= 0, jnp.exp(jnp.maximum(diff, 0.0)[None] * log_g[:, None, None]), 0.0)
    xi = jnp.exp((i[None, :] + 1.0) * log_g[:, None]).T[None, :, :, None]
    zeta = jnp.exp((chunk - 1.0 - i)[None, :] * log_g[:, None])
    g_c = jnp.exp(chunk * log_g)[None, :, None, None]

    def to_chunks(a):
        return jnp.moveaxis(a.astype(jnp.float32).reshape(B, n, chunk, *a.shape[2:]), 1, 0)

    def step(s, inp):
        qc, kc, vc = inp
        att = jnp.einsum('bihd,bjhd->bhij', qc, kc) * dmat
        inner = jnp.einsum('bhij,bjhe->bihe', att, vc)
        cross = jnp.einsum('bihd,bhde->bihe', qc, s) * xi
        s = g_c * s + jnp.einsum('bjhd,bjhe,hj->bhde', kc, vc, zeta)
        return s, inner + cross

    s_fin, out = lax.scan(step, s0.astype(jnp.float32), (to_chunks(q), to_chunks(k), to_chunks(v)))
    return jnp.moveaxis(out, 0, 1).reshape(B, T, H, Dv), s_fin


def head_groupnorm(o, g):
    mu = jnp.mean(o, axis=-1, keepdims=True)
    var = jnp.mean(jnp.square(o - mu), axis=-1, keepdims=True)
    y = (o - mu) * lax.rsqrt(var + EPS)
    return y.reshape(*o.shape[:2], -1) * g.astype(jnp.float32)


def run_layer(x, c, lw, past_kv, win_buf, ret_s0, conv_buf):
    B, T, _ = x.shape
    is_prompt = past_kv is None
    pos0 = 0 if is_prompt else past_kv.shape[1]
    pos = pos0 + jnp.arange(T, dtype=jnp.int32)
    mod = jax.nn.silu(c) @ lw['w_ada'] + lw['b_ada']
    sh1, sc1, gt1, sh2, sc2, gt2 = jnp.split(mod[:, None, :], 6, axis=-1)
    h = rmsnorm(x, lw['g_norm1']) * (1.0 + sc1) + sh1
    q_n, kv_n, gate_n, q_r, k_r, v_r, g_r = split_cols(h @ lw['w_in'])
    q_n = rope(q_n.reshape(B, T, N_HEADS_NSA, HEAD_DIM_NSA), pos).reshape(B, T, N_KV_NSA, GQA_REP, HEAD_DIM_NSA)
    kvr = kv_n.reshape(B, T, 3, 2, N_KV_NSA, HEAD_DIM_NSA)
    keys = rope(kvr[:, :, :, 0].reshape(B, T, 3 * N_KV_NSA, HEAD_DIM_NSA), pos).reshape(B, T, 3, N_KV_NSA, HEAD_DIM_NSA)
    kv = jnp.stack([keys, kvr[:, :, :, 1]], axis=3).reshape(B, T, 6, N_KV_NSA, HEAD_DIM_NSA)
    nsa_rows, win_rows = kv[:, :, :4], kv[:, :, 4:]
    gates = gate_n.reshape(B, T, 3, N_KV_NSA, GQA_REP)
    if is_prompt:
        o_nsa = nsa_prompt(q_n, gates, nsa_rows, win_rows, lw)
        new_win = win_rows[:, T - min(WINDOW, T):]
        ret_chunk = min(RET_CHUNK, T)
    else:
        kv_full = jnp.concatenate([past_kv, nsa_rows.astype(past_kv.dtype)], axis=1)
        win_all = jnp.concatenate([win_buf, win_rows.astype(win_buf.dtype)], axis=1)
        wb = win_buf.shape[1]
        kw_pos = pos0 - wb + jnp.arange(wb + T, dtype=jnp.int32)
        o_nsa = nsa_block(q_n, pos, gates, *nsa_context(kv_full, lw), win_all[:, :, 0], win_all[:, :, 1], kw_pos)
        keep = min(WINDOW, pos0 + T)
        new_win = win_all[:, win_all.shape[1] - keep:]
        ret_chunk = T
    o_nsa = rmsnorm(o_nsa, lw['g_nsa_out'])
    qr = rope(q_r.reshape(B, T, N_HEADS_RET, DK_RET), pos) * (DK_RET ** -0.5)
    kr = rope(k_r.reshape(B, T, N_HEADS_RET, DK_RET), pos)
    vr = v_r.reshape(B, T, N_HEADS_RET, DV_RET)
    o_ret, s_new = retention_chunkwise(qr, kr, vr, ret_s0, ret_chunk)
    o_ret = (jax.nn.silu(g_r.astype(jnp.float32)) * head_groupnorm(o_ret, lw['g_ret_out'])).astype(o_nsa.dtype)
    mix = jnp.concatenate([o_nsa, o_ret], axis=-1) @ lw['w_out']
    x = x + gt1 * mix
    h2 = rmsnorm(x, lw['g_norm2']) * (1.0 + sc2) + sh2
    u = h2 @ lw['w_up']
    full = jnp.concatenate([conv_buf.astype(u.dtype), u], axis=1)
    y = lw['conv_b']
    for k in range(CONV_W):
        y = y + lw['conv_w'][k] * full[:, k:k + T]
    a, b = jnp.split(y, 2, axis=-1)
    x = x + gt2 * ((jax.nn.silu(a) * b) @ lw['w_down'])
    return x, nsa_rows, new_win, s_new, full[:, T:]


def setup_inputs(seed: int = 0) -> dict:
    key = jax.random.key(seed)
    ks = jax.random.split(key, 32)
    f32 = jnp.float32
    n_pages = PAST_LEN // PAGE_SIZE
    n_pool = (DEC_BATCH * n_pages * 5 + 3) // 4
    wb = min(WINDOW, PAST_LEN)

    def nrm(k, shape, s=1.0):
        return jax.random.normal(k, shape, f32) * s

    def gain(k, shape):
        return 1.0 + 0.01 * jax.random.normal(k, shape, f32)

    page_table = jax.random.permutation(ks[8], n_pool)[:DEC_BATCH * n_pages].reshape(DEC_BATCH, n_pages).astype(jnp.int32)
    return {
        'x_prompt': nrm(ks[0], (BATCH, SEQ, D_MODEL)),
        'x_sample': nrm(ks[1], (DEC_BATCH, DEC_SEQ, D_MODEL)),
        'c_prompt': nrm(ks[2], (BATCH, D_MODEL)),
        'c_sample': nrm(ks[3], (DEC_BATCH, D_MODEL)),
        'cache_nsa_kv': nrm(ks[4], (n_pool, DEPTH, PAGE_SIZE, 4, N_KV_NSA, HEAD_DIM_NSA)),
        'cache_win_kv': nrm(ks[5], (DEC_BATCH, DEPTH, wb, 2, N_KV_NSA, HEAD_DIM_NSA)),
        'state_ret': nrm(ks[6], (DEC_BATCH, DEPTH, N_HEADS_RET, DK_RET, DV_RET), 4.0),
        'state_conv': nrm(ks[7], (DEC_BATCH, DEPTH, CONV_W - 1, 2 * D_FF)),
        'page_table': page_table,
        'w_ada': nrm(ks[9], (DEPTH, D_MODEL, 6 * D_MODEL), D_MODEL ** -0.5),
        'b_ada': nrm(ks[10], (DEPTH, 6 * D_MODEL), 0.01),
        'g_norm1': gain(ks[11], (DEPTH, D_MODEL)),
        'w_in': nrm(ks[12], (DEPTH, D_MODEL, D_IN), D_MODEL ** -0.5),
        'cmp_pos_k': nrm(ks[13], (DEPTH, CMP_BLOCK, HEAD_DIM_NSA), 0.02),
        'cmp_w1_k': nrm(ks[14], (DEPTH, CMP_BLOCK * HEAD_DIM_NSA, HEAD_DIM_NSA), (CMP_BLOCK * HEAD_DIM_NSA) ** -0.5),
        'cmp_w2_k': nrm(ks[15], (DEPTH, HEAD_DIM_NSA, HEAD_DIM_NSA), HEAD_DIM_NSA ** -0.5),
        'cmp_pos_v': nrm(ks[16], (DEPTH, CMP_BLOCK, HEAD_DIM_NSA), 0.02),
        'cmp_w1_v': nrm(ks[17], (DEPTH, CMP_BLOCK * HEAD_DIM_NSA, HEAD_DIM_NSA), (CMP_BLOCK * HEAD_DIM_NSA) ** -0.5),
        'cmp_w2_v': nrm(ks[18], (DEPTH, HEAD_DIM_NSA, HEAD_DIM_NSA), HEAD_DIM_NSA ** -0.5),
        'g_nsa_out': gain(ks[19], (DEPTH, D_NSA)),
        'g_ret_out': gain(ks[20], (DEPTH, D_RET)),
        'w_out': nrm(ks[21], (DEPTH, D_MIX, D_MODEL), D_MIX ** -0.5),
        'g_norm2': gain(ks[22], (DEPTH, D_MODEL)),
        'w_up': nrm(ks[23], (DEPTH, D_MODEL, 2 * D_FF), D_MODEL ** -0.5),
        'conv_w': nrm(ks[24], (DEPTH, CONV_W, 2 * D_FF), CONV_W ** -0.5),
        'conv_b': nrm(ks[25], (DEPTH, 2 * D_FF), 0.01),
        'w_down': nrm(ks[26], (DEPTH, D_FF, D_MODEL), D_FF ** -0.5),
        'g_final': gain(ks[27], (D_MODEL,)),
    }


def reference(x_prompt, x_sample, c_prompt, c_sample, cache_nsa_kv, cache_win_kv, state_ret, state_conv, page_table,
              w_ada, b_ada, g_norm1, w_in, cmp_pos_k, cmp_w1_k, cmp_w2_k, cmp_pos_v, cmp_w1_v, cmp_w2_v,
              g_nsa_out, g_ret_out, w_out, g_norm2, w_up, conv_w, conv_b, w_down, g_final):
    wts = {'w_ada': w_ada, 'b_ada': b_ada, 'g_norm1': g_norm1, 'w_in': w_in,
           'cmp_pos_k': cmp_pos_k, 'cmp_w1_k': cmp_w1_k, 'cmp_w2_k': cmp_w2_k,
           'cmp_pos_v': cmp_pos_v, 'cmp_w1_v': cmp_w1_v, 'cmp_w2_v': cmp_w2_v,
           'g_nsa_out': g_nsa_out, 'g_ret_out': g_ret_out, 'w_out': w_out, 'g_norm2': g_norm2,
           'w_up': w_up, 'conv_w': conv_w, 'conv_b': conv_b, 'w_down': w_down}
    B = x_prompt.shape[0]
    xp, xs = x_prompt, x_sample
    kv_p, kv_s, win_p, win_s, ret_p, ret_s, conv_p, conv_s = [], [], [], [], [], [], [], []
    for l in range(DEPTH):
        lw = {name: w[l] for name, w in wts.items()}
        s0 = jnp.zeros((B, N_HEADS_RET, DK_RET, DV_RET), jnp.float32)
        cb0 = jnp.zeros((B, CONV_W - 1, 2 * D_FF), xp.dtype)
        xp, a, b, c_, d = run_layer(xp, c_prompt, lw, None, None, s0, cb0)
        kv_p.append(a); win_p.append(b); ret_p.append(c_); conv_p.append(d)
        past = cache_nsa_kv[page_table, l]
        past = past.reshape(past.shape[0], -1, *past.shape[3:])
        xs, a, b, c_, d = run_layer(xs, c_sample, lw, past, cache_win_kv[:, l], state_ret[:, l], state_conv[:, l])
        kv_s.append(a); win_s.append(b); ret_s.append(c_); conv_s.append(d)
    y_prompt = rmsnorm(xp, g_final)
    y_sample = rmsnorm(xs, g_final)
    return (y_prompt, y_sample,
            jnp.stack(kv_p, axis=1), jnp.stack(kv_s, axis=1),
            jnp.stack(win_p, axis=1), jnp.stack(win_s, axis=1),
            jnp.stack(ret_p, axis=1), jnp.stack(ret_s, axis=1),
            jnp.stack(conv_p, axis=1), jnp.stack(conv_s, axis=1))
```

```python
import functools

import numpy as np
import jax
import jax.numpy as jnp
from jax import lax
from jax.experimental import pallas as pl
from jax.experimental.pallas import tpu as pltpu

F32 = jnp.float32
BF16 = jnp.bfloat16

D_MODEL = 1024
PAGE = 128
HD = 64
N_KV = 2
REP = 4
D_NSA = 512
CMP_STRIDE = 16
SEL_BLOCK = 64
N_SEL = 16
WINDOW = 512
Q_TILE = 128
KV_TILE = 512
FORCE_BONUS = 100.0
N_RET = 4
DK_RET = 64
DV_RET = 128
D_RET = 512
RET_CHUNK = 128
D_FF = 2816
ROPE_THETA = 10000.0
EPS = 1e-6
NEG_INF = -1e30
LANES = 128
W_IN_COLS = 3072
VMEM_LIMIT = 56 * 1024 * 1024


def _cparams(*sem):
    return pltpu.CompilerParams(dimension_semantics=sem, vmem_limit_bytes=VMEM_LIMIT)


def _dot(a, b):
    return jnp.dot(a, b, preferred_element_type=F32)


def _dot_nt(a, b):
    return lax.dot_general(a, b, (((1,), (1,)), ((), ())), preferred_element_type=F32)


def _dot_tn(a, b):
    return lax.dot_general(a, b, (((0,), (0,)), ((), ())), preferred_element_type=F32)


def _dot_split3(p, m_bf16):
    hi = p.astype(BF16)
    r1 = p - hi.astype(F32)
    mid = r1.astype(BF16)
    lo = (r1 - mid.astype(F32)).astype(BF16)
    return _dot(hi, m_bf16) + _dot(mid, m_bf16) + _dot(lo, m_bf16)


def _rms(x, g):
    return x * lax.rsqrt(jnp.mean(x * x, axis=-1, keepdims=True) + EPS) * g


def _masked_softmax(s, mask):
    s = jnp.where(mask, s, NEG_INF)
    m = jnp.max(s, axis=-1, keepdims=True)
    e = jnp.where(mask, jnp.exp(s - m), 0.0)
    return e / jnp.maximum(jnp.sum(e, axis=-1, keepdims=True), 1e-30)


def _topk_mask_t(score_t, n_sel):
    nb = score_t.shape[0]
    blk = lax.broadcasted_iota(jnp.int32, score_t.shape, 0)

    def body(_, carry):
        work, sel = carry
        m = jnp.max(work, axis=0, keepdims=True)
        idx = jnp.min(jnp.where(work == m, blk, nb), axis=0, keepdims=True)
        pick = blk == idx
        return jnp.where(pick, -jnp.inf, work), jnp.where(pick, 1.0, sel)

    _, sel = lax.fori_loop(0, n_sel, body, (score_t, jnp.zeros(score_t.shape, F32)))
    return sel


def _selection(psum, mmat, qpos, ns, n_sel):
    imp = _dot_split3(psum, mmat)
    blk = lax.broadcasted_iota(jnp.int32, (1, imp.shape[1]), 1)
    qblk = qpos >> 6
    valid = (blk * SEL_BLOCK <= qpos) & (blk < ns)
    forced = (blk == 0) | (blk == qblk) | (blk == qblk - 1)
    score = jnp.where(valid, imp + jnp.where(forced, FORCE_BONUS, 0.0), NEG_INF)
    score = jnp.where(blk < ns, score, -jnp.inf)
    sel = _topk_mask_t(score.T, n_sel).T
    return jnp.where(valid, sel, 0.0)


def _mod_kernel(c_ref, w_ref, b_ref, o_ref):
    c = c_ref[...]
    o_ref[...] = _dot(c * jax.nn.sigmoid(c), w_ref[...]) + b_ref[...]


def _mod_call(c_all, w_ada, b_ada):
    n = c_all.shape[0]
    tn = 1536
    return pl.pallas_call(
        _mod_kernel,
        grid=(w_ada.shape[1] // tn,),
        in_specs=[pl.BlockSpec((n, D_MODEL), lambda j: (0, 0)),
                  pl.BlockSpec((D_MODEL, tn), lambda j: (0, j)),
                  pl.BlockSpec((1, tn), lambda j: (0, j))],
        out_specs=pl.BlockSpec((n, tn), lambda j: (0, j)),
        out_shape=jax.ShapeDtypeStruct((n, w_ada.shape[1]), F32),
        compiler_params=_cparams("arbitrary"),
        name="adaln_mod",
    )(c_all, w_ada, b_ada.reshape(1, -1))


def _inproj_kernel(x_ref, sc_ref, sh_ref, g1_ref, w_ref, cos_ref, sin_ref,
                   qh_ref, kvn_ref, kvw_ref, kvh_ref, ret_ref, gat_ref):
    h = (_rms(x_ref[...], g1_ref[...]) * (1.0 + sc_ref[...]) + sh_ref[...]).astype(BF16)
    cos = cos_ref[...]
    sin = sin_ref[...]
    lane = lax.broadcasted_iota(jnp.int32, (1, LANES), 1)
    first_half = (lane & (HD - 1)) < HD // 2

    def seg(c0, n):
        return _dot(h, w_ref[:, c0:c0 + n])

    def rope(a):
        sw = jnp.where(first_half, pltpu.roll(a, LANES - HD // 2, 1), pltpu.roll(a, HD // 2, 1))
        return a * cos + sw * sin

    for j in range(4):
        q = rope(seg(LANES * j, LANES)) * (HD ** -0.5)
        qh_ref[2 * j] = q[:, :HD].astype(qh_ref.dtype)
        qh_ref[2 * j + 1] = q[:, HD:].astype(qh_ref.dtype)
    for s in range(6):
        a = seg(512 + LANES * s, LANES)
        if s % 2 == 0:
            a = rope(a)
        if s < 4:
            kvn_ref[:, LANES * s:LANES * (s + 1)] = a
        else:
            kvw_ref[:, LANES * (s - 4):LANES * (s - 3)] = a
        if s >= 2:
            kvh_ref[2 * (s - 2)] = a[:, :HD].astype(kvh_ref.dtype)
            kvh_ref[2 * (s - 2) + 1] = a[:, HD:].astype(kvh_ref.dtype)
    for j in range(2):
        ret_ref[:, LANES * j:LANES * (j + 1)] = rope(seg(1280 + LANES * j, LANES)) * (DK_RET ** -0.5)
        ret_ref[:, 256 + LANES * j:256 + LANES * (j + 1)] = rope(seg(1536 + LANES * j, LANES))
    ret_ref[:, 512:1024] = seg(1792, 512)
    ret_ref[:, 1024:1536] = seg(2304, 512)
    gat_ref[...] = seg(2816, 256)


def _inproj_call(x2d, sc3, sh3, mod_idx, g1, w_in_re, cos, sin, tab_idx, tm, head_dtype):
    m = x2d.shape[0]
    mrows = sc3.shape[1]
    return pl.pallas_call(
        _inproj_kernel,
        grid=(m // tm,),
        in_specs=[pl.BlockSpec((tm, D_MODEL), lambda i: (i, 0)),
                  pl.BlockSpec((None, mrows, D_MODEL), lambda i: (mod_idx(i), 0, 0)),
                  pl.BlockSpec((None, mrows, D_MODEL), lambda i: (mod_idx(i), 0, 0)),
                  pl.BlockSpec((1, D_MODEL), lambda i: (0, 0)),
                  pl.BlockSpec((D_MODEL, W_IN_COLS), lambda i: (0, 0)),
                  pl.BlockSpec((tm, LANES), lambda i: (tab_idx(i), 0)),
                  pl.BlockSpec((tm, LANES), lambda i: (tab_idx(i), 0))],
        out_specs=[pl.BlockSpec((8, tm, HD), lambda i: (0, i, 0)),
                   pl.BlockSpec((tm, 512), lambda i: (i, 0)),
                   pl.BlockSpec((tm, 256), lambda i: (i, 0)),
                   pl.BlockSpec((8, tm, HD), lambda i: (0, i, 0)),
                   pl.BlockSpec((tm, 1536), lambda i: (i, 0)),
                   pl.BlockSpec((tm, 256), lambda i: (i, 0))],
        out_shape=[jax.ShapeDtypeStruct((8, m, HD), head_dtype),
                   jax.ShapeDtypeStruct((m, 512), F32),
                   jax.ShapeDtypeStruct((m, 256), F32),
                   jax.ShapeDtypeStruct((8, m, HD), head_dtype),
                   jax.ShapeDtypeStruct((m, 1536), F32),
                   jax.ShapeDtypeStruct((m, 256), F32)],
        compiler_params=_cparams("parallel"),
        name="in_proj",
    )(x2d, sc3, sh3, g1, w_in_re, cos, sin)


def _cmp_partial(slab, wk_ref, wv_ref, pos_ref, ab_ref):
    for kind, w_ref in ((0, wk_ref), (1, wv_ref)):
        xcat = jnp.concatenate([slab(t, kind) for t in range(CMP_STRIDE)], axis=1).astype(BF16)
        r = _dot(xcat, w_ref[...])
        bias = _dot(pos_ref[kind], w_ref[...])
        ab_ref[:, LANES * kind:LANES * (kind + 1)] = r[:, :LANES] + bias[0:1, :LANES]
        ab_ref[:, 256 + LANES * kind:256 + LANES * (kind + 1)] = r[:, LANES:] + bias[1:2, LANES:]


def _cmp_prompt_kernel(x_ref, wk_ref, wv_ref, pos_ref, ab_ref):
    _cmp_partial(lambda t, kind: x_ref[:, 512 * t + LANES * kind:512 * t + LANES * (kind + 1)],
                 wk_ref, wv_ref, pos_ref, ab_ref)


def _cmp_prompt_call(kvn, wk, wv, pos):
    rows = kvn.shape[0] // CMP_STRIDE
    xv = kvn.reshape(rows, CMP_STRIDE * 512)
    tr = min(128, rows)
    return pl.pallas_call(
        _cmp_prompt_kernel,
        grid=(rows // tr,),
        in_specs=[pl.BlockSpec((tr, CMP_STRIDE * 512), lambda i: (i, 0)),
                  pl.BlockSpec(wk.shape, lambda i: (0, 0)),
                  pl.BlockSpec(wv.shape, lambda i: (0, 0)),
                  pl.BlockSpec(pos.shape, lambda i: (0, 0, 0))],
        out_specs=pl.BlockSpec((tr, 512), lambda i: (i, 0)),
        out_shape=jax.ShapeDtypeStruct((rows, 512), F32),
        compiler_params=_cparams("parallel"),
        name="cmp_partial_prompt",
    )(xv, wk, wv, pos)


PAGES_PER_STEP = 8


def _cmp_sample_kernel(pt_ref, *refs):
    pages = refs[:PAGES_PER_STEP]
    wk_ref, wv_ref, pos_ref, ab_ref, ks_ref = refs[PAGES_PER_STEP:]

    def slab(t, kind):
        c0 = 512 * t + LANES * kind
        return jnp.concatenate([p[:, c0:c0 + LANES] for p in pages], axis=0)

    _cmp_partial(slab, wk_ref, wv_ref, pos_ref, ab_ref)
    rows = [p[:, 512 * t + 256:512 * t + 512] for p in pages for t in range(CMP_STRIDE)]
    ks_ref[...] = jnp.concatenate(rows, axis=0).astype(BF16)


def _cmp_sample_call(cache, page_table, wk, wv, pos):
    n_pool = cache.shape[0]
    nb, n_pages = page_table.shape
    chunks = PAGE // CMP_STRIDE
    cv = cache.reshape(n_pool, chunks, CMP_STRIDE * 512)
    steps = n_pages // PAGES_PER_STEP

    def page_spec(k):
        return pl.BlockSpec((None, chunks, CMP_STRIDE * 512),
                            lambda b, j, pt: (pt[b, j * PAGES_PER_STEP + k], 0, 0))

    grid_spec = pltpu.PrefetchScalarGridSpec(
        num_scalar_prefetch=1,
        grid=(nb, steps),
        in_specs=[page_spec(k) for k in range(PAGES_PER_STEP)] + [
            pl.BlockSpec(wk.shape, lambda b, j, pt: (0, 0)),
            pl.BlockSpec(wv.shape, lambda b, j, pt: (0, 0)),
            pl.BlockSpec(pos.shape, lambda b, j, pt: (0, 0, 0))],
        out_specs=[pl.BlockSpec((None, PAGES_PER_STEP * chunks, 512), lambda b, j, pt: (b, j, 0)),
                   pl.BlockSpec((None, PAGES_PER_STEP * PAGE, 256), lambda b, j, pt: (b, j, 0))],
    )
    return pl.pallas_call(
        _cmp_sample_kernel,
        grid_spec=grid_spec,
        out_shape=[jax.ShapeDtypeStruct((nb, n_pages * chunks, 512), F32),
                   jax.ShapeDtypeStruct((nb, n_pages * PAGE, 256), BF16)],
        compiler_params=_cparams("parallel", "arbitrary"),
        name="cmp_partial_sample",
    )(page_table, *([cv] * PAGES_PER_STEP), wk, wv, pos)


def _cmp_fin_kernel(ab_ref, w2_ref, o_ref):
    nc = ab_ref.shape[0]
    a = ab_ref[:, :256]
    b_next = pltpu.roll(ab_ref[:, 256:], nc - 1, 0)
    hid = jax.nn.gelu(a + b_next)
    out = _dot(hid.astype(BF16), w2_ref[...])
    row = lax.broadcasted_iota(jnp.int32, (nc, 1), 0)
    out = jnp.where(row < nc - 1, out, 0.0)
    for s in range(4):
        o_ref[s] = out[:, HD * s:HD * (s + 1)]


def _cmp_fin_call(ab, w2bd):
    nb, nc, _ = ab.shape
    return pl.pallas_call(
        _cmp_fin_kernel,
        grid=(nb,),
        in_specs=[pl.BlockSpec((None, nc, 512), lambda b: (b, 0, 0)),
                  pl.BlockSpec((256, 256), lambda b: (0, 0))],
        out_specs=pl.BlockSpec((None, 4, nc, HD), lambda b: (b, 0, 0, 0)),
        out_shape=jax.ShapeDtypeStruct((nb, 4, nc, HD), F32),
        compiler_params=_cparams("parallel"),
        name="cmp_finish",
    )(ab, w2bd)


def _nsa_prompt_kernel(q_ref, gat_ref, kc_ref, vc_ref, ks_ref, vs_ref, kw_ref, vw_ref, mmat_ref, emat_ref,
                       o_ref, sel_sc, m_sc, l_sc, acc_sc, *, ns, n_sel):
    i = pl.program_id(2)
    start = i * Q_TILE
    qpos = start + lax.broadcasted_iota(jnp.int32, (Q_TILE, 1), 0)
    nc = kc_ref.shape[0]
    kc = kc_ref[...].astype(BF16)
    vc = vc_ref[...].astype(BF16)

    c_end = lax.broadcasted_iota(jnp.int32, (1, nc), 1) * CMP_STRIDE + (2 * CMP_STRIDE - 1)
    cmask = c_end <= qpos
    psum = jnp.zeros((Q_TILE, nc), F32)
    o_cmp = []
    for r in range(REP):
        p = _masked_softmax(_dot_nt(q_ref[r], kc), cmask)
        psum = psum + p
        o_cmp.append(_dot(p.astype(BF16), vc))

    sel_sc[...] = _selection(psum, mmat_ref[...], qpos, ns, n_sel).astype(BF16)

    m_sc[...] = jnp.full(m_sc.shape, NEG_INF, F32)
    l_sc[...] = jnp.zeros(l_sc.shape, F32)
    acc_sc[...] = jnp.zeros(acc_sc.shape, F32)

    def tile(j, causal):
        k0 = pl.multiple_of(j * KV_TILE, KV_TILE)
        k_t = ks_ref[pl.ds(k0, KV_TILE), :]
        v_t = vs_ref[pl.ds(k0, KV_TILE), :]
        valid = _dot(sel_sc[...], emat_ref[j]) > 0.5
        if causal:
            kpos = k0 + lax.broadcasted_iota(jnp.int32, (1, KV_TILE), 1)
            valid = valid & (kpos <= qpos)
        for r in range(REP):
            rows = slice(r * Q_TILE, (r + 1) * Q_TILE)
            s = jnp.where(valid, _dot_nt(q_ref[r], k_t), NEG_INF)
            m_old = m_sc[rows]
            m_new = jnp.maximum(m_old, jnp.max(s, axis=-1, keepdims=True))
            alpha = jnp.exp(m_old - m_new)
            p = jnp.where(valid, jnp.exp(s - m_new), 0.0)
            l_sc[rows] = alpha * l_sc[rows] + jnp.sum(p, axis=-1, keepdims=True)
            acc_sc[rows] = alpha * acc_sc[rows] + _dot(p.astype(BF16), v_t)
            m_sc[rows] = m_new

    n_full = start // KV_TILE

    def body(j, carry):
        tile(j, False)
        return carry

    lax.fori_loop(0, n_full, body, 0)
    tile(n_full, True)

    base = pl.multiple_of(jnp.maximum(start - WINDOW, 0), Q_TILE)
    kw = kw_ref[pl.ds(base, WINDOW + Q_TILE), :]
    vw = vw_ref[pl.ds(base, WINDOW + Q_TILE), :]
    dpos = qpos - (base + lax.broadcasted_iota(jnp.int32, (1, WINDOW + Q_TILE), 1))
    wmask = (dpos >= 0) & (dpos < WINDOW)

    gate = jax.nn.sigmoid(gat_ref[...])
    for r in range(REP):
        rows = slice(r * Q_TILE, (r + 1) * Q_TILE)
        pw = _masked_softmax(_dot_nt(q_ref[r], kw), wmask)
        o_win = _dot(pw.astype(BF16), vw)
        o_slc = acc_sc[rows] / jnp.maximum(l_sc[rows], 1e-30)
        o_ref[:, HD * r:HD * (r + 1)] = (gate[:, r:r + 1] * o_cmp[r] + gate[:, REP + r:REP + r + 1] * o_slc
                                        + gate[:, 2 * REP + r:2 * REP + r + 1] * o_win)


def _nsa_prompt_call(qh, gat, cmp, kvh, mmat, emat, nb, seq, cmp_off):
    nq = seq // Q_TILE
    nc = cmp.shape[2]
    ns = seq // SEL_BLOCK
    nsp = mmat.shape[1]
    kern = functools.partial(_nsa_prompt_kernel, ns=ns, n_sel=min(N_SEL, ns))

    def kv_spec(slab0):
        return pl.BlockSpec((None, seq, HD), lambda b, g, i: (slab0 + g, b, 0))

    return pl.pallas_call(
        kern,
        grid=(nb, N_KV, nq),
        in_specs=[pl.BlockSpec((REP, Q_TILE, HD), lambda b, g, i: (g, b * nq + i, 0)),
                  pl.BlockSpec((Q_TILE, LANES), lambda b, g, i: (b * nq + i, g)),
                  pl.BlockSpec((None, None, nc, HD), lambda b, g, i: (cmp_off + b, g, 0, 0)),
                  pl.BlockSpec((None, None, nc, HD), lambda b, g, i: (cmp_off + b, 2 + g, 0, 0)),
                  kv_spec(0), kv_spec(2), kv_spec(4), kv_spec(6),
                  pl.BlockSpec(mmat.shape, lambda b, g, i: (0, 0)),
                  pl.BlockSpec(emat.shape, lambda b, g, i: (0, 0, 0))],
        out_specs=pl.BlockSpec((Q_TILE, REP * HD), lambda b, g, i: (b * nq + i, g)),
        out_shape=jax.ShapeDtypeStruct((nb * seq, D_NSA), F32),
        scratch_shapes=[pltpu.VMEM((Q_TILE, nsp), BF16),
                        pltpu.VMEM((REP * Q_TILE, 1), F32),
                        pltpu.VMEM((REP * Q_TILE, 1), F32),
                        pltpu.VMEM((REP * Q_TILE, HD), F32)],
        compiler_params=_cparams("parallel", "parallel", "arbitrary"),
        name="nsa_prompt",
    )(qh, gat, cmp, cmp, kvh, kvh, kvh, kvh, mmat, emat)


def _nsa_sample_kernel(q_ref, gat_ref, cmp_ref, ks_ref, new_ref, win_ref, mmat_ref, emat_ref, o_ref,
                       *, past, ts, ns, n_sel):
    nc = cmp_ref.shape[1]
    wb = win_ref.shape[0]
    rows = REP * ts
    n_tiles = past // KV_TILE
    t_idx = lax.broadcasted_iota(jnp.int32, (ts, 1), 0)
    qpos = jnp.concatenate([past + t_idx] * REP, axis=0)
    qpos_pad = past + lax.broadcasted_iota(jnp.int32, (Q_TILE, 1), 0)
    gate = jax.nn.sigmoid(gat_ref[...])
    pad_rows = jnp.zeros((Q_TILE - ts, HD), F32)

    def padded(slab):
        return jnp.concatenate([new_ref[slab], pad_rows], axis=0).astype(BF16)

    for g in range(N_KV):
        q = jnp.concatenate([q_ref[REP * g + r] for r in range(REP)], axis=0).astype(BF16)
        kc = cmp_ref[g].astype(BF16)
        vc = cmp_ref[2 + g].astype(BF16)
        c_end = lax.broadcasted_iota(jnp.int32, (1, nc), 1) * CMP_STRIDE + (2 * CMP_STRIDE - 1)
        p = _masked_softmax(_dot_nt(q, kc), c_end <= qpos)
        o_cmp = _dot(p.astype(BF16), vc)
        psum = p[0:ts]
        for r in range(1, REP):
            psum = psum + p[r * ts:(r + 1) * ts]
        psum = jnp.concatenate([psum, jnp.zeros((Q_TILE - ts, nc), F32)], axis=0)
        sel = _selection(psum, mmat_ref[...], qpos_pad, ns, n_sel)[0:ts]
        sel = jnp.concatenate([sel] * REP, axis=0)
        sel_bf = sel.astype(BF16)

        def body(j, carry):
            m_old, l_old, acc = carry
            k0 = pl.multiple_of(j * KV_TILE, KV_TILE)
            k_t = ks_ref[pl.ds(k0, KV_TILE), HD * g:HD * (g + 1)]
            v_t = ks_ref[pl.ds(k0, KV_TILE), 128 + HD * g:128 + HD * (g + 1)]
            valid = _dot(sel_bf, emat_ref[j]) > 0.5
            s = jnp.where(valid, _dot_nt(q, k_t), NEG_INF)
            m_new = jnp.maximum(m_old, jnp.max(s, axis=-1, keepdims=True))
            alpha = jnp.exp(m_old - m_new)
            pj = jnp.where(valid, jnp.exp(s - m_new), 0.0)
            return (m_new, alpha * l_old + jnp.sum(pj, axis=-1, keepdims=True),
                    alpha * acc + _dot(pj.astype(BF16), v_t))

        m_old, l_old, acc = lax.fori_loop(
            0, n_tiles, body,
            (jnp.full((rows, 1), NEG_INF, F32), jnp.zeros((rows, 1), F32), jnp.zeros((rows, HD), F32)))
        kidx = lax.broadcasted_iota(jnp.int32, (1, Q_TILE), 1)
        valid = (sel[:, ns - 1:ns] > 0.5) & (past + kidx <= qpos)
        s = jnp.where(valid, _dot_nt(q, padded(g)), NEG_INF)
        m_new = jnp.maximum(m_old, jnp.max(s, axis=-1, keepdims=True))
        alpha = jnp.exp(m_old - m_new)
        pj = jnp.where(valid, jnp.exp(s - m_new), 0.0)
        l_new = alpha * l_old + jnp.sum(pj, axis=-1, keepdims=True)
        acc = alpha * acc + _dot(pj.astype(BF16), padded(2 + g))
        o_slc = acc / jnp.maximum(l_new, 1e-30)

        s_w = jnp.concatenate([_dot_nt(q, win_ref[:, HD * g:HD * (g + 1)].astype(BF16)),
                               _dot_nt(q, padded(4 + g))], axis=1)
        kpos = past - wb + lax.broadcasted_iota(jnp.int32, (1, wb + Q_TILE), 1)
        dpos = qpos - kpos
        pw = _masked_softmax(s_w, (dpos >= 0) & (dpos < WINDOW) & (kpos >= 0))
        o_win = (_dot(pw[:, :wb].astype(BF16), win_ref[:, 128 + HD * g:128 + HD * (g + 1)].astype(BF16))
                 + _dot(pw[:, wb:].astype(BF16), padded(6 + g)))

        for r in range(REP):
            rs = slice(r * ts, (r + 1) * ts)
            c = LANES * g + r
            o_ref[:, HD * (REP * g + r):HD * (REP * g + r + 1)] = (
                gate[:, c:c + 1] * o_cmp[rs] + gate[:, c + REP:c + REP + 1] * o_slc[rs]
                + gate[:, c + 2 * REP:c + 2 * REP + 1] * o_win[rs])


def _nsa_sample_call(qh, gat, cmp, ks, kvh_new, win, mmat, emat, nb, ts, past, cmp_off):
    nc = cmp.shape[2]
    ns = past // SEL_BLOCK + 1
    kern = functools.partial(_nsa_sample_kernel, past=past, ts=ts, ns=ns, n_sel=min(N_SEL, ns))
    return pl.pallas_call(
        kern,
        grid=(nb,),
        in_specs=[pl.BlockSpec((8, ts, HD), lambda b: (0, b, 0)),
                  pl.BlockSpec((ts, 256), lambda b: (b, 0)),
                  pl.BlockSpec((None, 4, nc, HD), lambda b: (cmp_off + b, 0, 0, 0)),
                  pl.BlockSpec((None, past, 256), lambda b: (b, 0, 0)),
                  pl.BlockSpec((8, ts, HD), lambda b: (0, b, 0)),
                  pl.BlockSpec((None, win.shape[1], 256), lambda b: (b, 0, 0)),
                  pl.BlockSpec(mmat.shape, lambda b: (0, 0)),
                  pl.BlockSpec(emat.shape, lambda b: (0, 0, 0))],
        out_specs=pl.BlockSpec((ts, D_NSA), lambda b: (b, 0)),
        out_shape=jax.ShapeDtypeStruct((nb * ts, D_NSA), F32),
        compiler_params=_cparams("parallel"),
        name="nsa_sample",
    )(qh, gat, cmp, ks, kvh_new, win, mmat, emat)


def _ret_kernel(x_ref, s0_ref, dmat_ref, xi_ref, zeta_ref, gc_ref, gro_ref, o_ref, s_out_ref, s_sc, pad_sc,
                *, rows):
    @pl.when(pl.program_id(1) == 0)
    def _():
        s_sc[...] = s0_ref[...]

    if rows < RET_CHUNK:
        pad_sc[...] = jnp.zeros(pad_sc.shape, F32)
        pad_sc[0:rows, :] = x_ref[...]
        x = pad_sc
    else:
        x = x_ref
    for h in range(N_RET):
        q = x[:, DK_RET * h:DK_RET * (h + 1)].astype(BF16)
        k = x[:, 256 + DK_RET * h:256 + DK_RET * (h + 1)]
        v = x[:, 512 + DV_RET * h:512 + DV_RET * (h + 1)].astype(BF16)
        gr = x[:, 1024 + DV_RET * h:1024 + DV_RET * (h + 1)]
        att = _dot_nt(q, k.astype(BF16)) * dmat_ref[h]
        s_old = s_sc[h]
        o = _dot(att.astype(BF16), v) + _dot(q, s_old.astype(BF16)) * xi_ref[h]
        s_sc[h] = gc_ref[h] * s_old + _dot_tn((k * zeta_ref[h]).astype(BF16), v)
        mu = jnp.mean(o, axis=-1, keepdims=True)
        var = jnp.mean(jnp.square(o - mu), axis=-1, keepdims=True)
        y = (o - mu) * lax.rsqrt(var + EPS) * gro_ref[:, DV_RET * h:DV_RET * (h + 1)]
        res = gr * jax.nn.sigmoid(gr) * y
        o_ref[:, DV_RET * h:DV_RET * (h + 1)] = res[0:rows]
    s_out_ref[...] = s_sc[...]


def _ret_call(ret, s0, tabs, g_ret_out, nb, seq, rows):
    nchunks = seq // rows
    dmat, xi, zeta, gc = tabs
    kern = functools.partial(_ret_kernel, rows=rows)
    full3 = lambda b, c: (0, 0, 0)
    return pl.pallas_call(
        kern,
        grid=(nb, nchunks),
        in_specs=[pl.BlockSpec((rows, 1536), lambda b, c: (b * nchunks + c, 0)),
                  pl.BlockSpec((None, N_RET, DK_RET, DV_RET), lambda b, c: (b, 0, 0, 0)),
                  pl.BlockSpec(dmat.shape, full3), pl.BlockSpec(xi.shape, full3),
                  pl.BlockSpec(zeta.shape, full3), pl.BlockSpec(gc.shape, full3),
                  pl.BlockSpec((1, D_RET), lambda b, c: (0, 0))],
        out_specs=[pl.BlockSpec((rows, D_RET), lambda b, c: (b * nchunks + c, 0)),
                   pl.BlockSpec((None, N_RET, DK_RET, DV_RET), lambda b, c: (b, 0, 0, 0))],
        out_shape=[jax.ShapeDtypeStruct((nb * seq, D_RET), F32),
                   jax.ShapeDtypeStruct((nb, N_RET, DK_RET, DV_RET), F32)],
        scratch_shapes=[pltpu.VMEM((N_RET, DK_RET, DV_RET), F32),
                        pltpu.VMEM((RET_CHUNK, 1536), F32)],
        compiler_params=_cparams("parallel", "arbitrary"),
        name="retention",
    )(ret, s0, dmat, xi, zeta, gc, g_ret_out)


def _ret_tables(chunk):
    c = RET_CHUNK
    log_g = jnp.log1p(-jnp.power(2.0, -5.0 - jnp.arange(N_RET, dtype=F32)))
    i = jnp.arange(c, dtype=F32)
    diff = i[:, None] - i[None, :]
    dmat = jnp.where(diff >= 0, jnp.exp(jnp.maximum(diff, 0.0)[None] * log_g[:, None, None]), 0.0)
    xi = jnp.exp((i[None, :] + 1.0) * log_g[:, None])
    zeta = jnp.where(i[None, :] < chunk, jnp.exp((chunk - 1.0 - i)[None, :] * log_g[:, None]), 0.0)
    g_c = jnp.exp(chunk * log_g)
    return (dmat,
            jnp.broadcast_to(xi[:, :, None], (N_RET, c, DV_RET)),
            jnp.broadcast_to(zeta[:, :, None], (N_RET, c, DK_RET)),
            jnp.broadcast_to(g_c[:, None, None], (N_RET, DK_RET, DV_RET)))


def _outproj_kernel(on_ref, or_ref, x_ref, gt_ref, sc_ref, sh_ref, gn_ref, g2_ref, w_ref, x1_ref, h2_ref):
    a = _rms(on_ref[...], gn_ref[...])
    mix = _dot(a.astype(BF16), w_ref[0:D_NSA, :]) + _dot(or_ref[...].astype(BF16), w_ref[D_NSA:, :])
    x1 = x_ref[...] + gt_ref[...] * mix
    x1_ref[...] = x1
    h2_ref[...] = (_rms(x1, g2_ref[...]) * (1.0 + sc_ref[...]) + sh_ref[...]).astype(h2_ref.dtype)


def _outproj_call(o_nsa, o_ret, x2d, gt3, sc3, sh3, mod_idx, g_nsa, g2, w_out, tm, h_dtype):
    m = x2d.shape[0]
    mrows = gt3.shape[1]
    mod_spec = pl.BlockSpec((None, mrows, D_MODEL), lambda i: (mod_idx(i), 0, 0))
    return pl.pallas_call(
        _outproj_kernel,
        grid=(m // tm,),
        in_specs=[pl.BlockSpec((tm, D_NSA), lambda i: (i, 0)),
                  pl.BlockSpec((tm, D_RET), lambda i: (i, 0)),
                  pl.BlockSpec((tm, D_MODEL), lambda i: (i, 0)),
                  mod_spec, mod_spec, mod_spec,
                  pl.BlockSpec((1, D_NSA), lambda i: (0, 0)),
                  pl.BlockSpec((1, D_MODEL), lambda i: (0, 0)),
                  pl.BlockSpec((D_NSA + D_RET, D_MODEL), lambda i: (0, 0))],
        out_specs=[pl.BlockSpec((tm, D_MODEL), lambda i: (i, 0)),
                   pl.BlockSpec((tm, D_MODEL), lambda i: (i, 0))],
        out_shape=[jax.ShapeDtypeStruct((m, D_MODEL), F32),
                   jax.ShapeDtypeStruct((m, D_MODEL), h_dtype)],
        compiler_params=_cparams("parallel"),
        name="out_proj",
    )(o_nsa, o_ret, x2d, gt3, sc3, sh3, g_nsa, g2, w_out)


FF_TILE = D_FF // 2


def _ffn_up_kernel(h_ref, wa_ref, wb_ref, cwa_ref, cwb_ref, cba_ref, cbb_ref, sta_ref, stb_ref,
                   act_ref, csa_ref, csb_ref, prev_a, prev_b):
    @pl.when(pl.program_id(2) == 0)
    def _():
        prev_a[...] = sta_ref[...]
        prev_b[...] = stb_ref[...]

    h = h_ref[...].astype(BF16)
    tm = h.shape[0]
    row = lax.broadcasted_iota(jnp.int32, (tm, 1), 0)

    def half(w_ref, cw_ref, cb_ref, prev, cs_ref):
        u = _dot(h, w_ref[...])
        p = prev[...]
        back1 = pltpu.roll(u, 1, 0)
        back2 = pltpu.roll(u, 2, 0)
        u1 = jnp.where(row >= 1, back1, p[1:2])
        u2 = jnp.where(row >= 2, back2, jnp.where(row == 1, p[1:2], p[0:1]))
        y = cb_ref[...] + cw_ref[0:1] * u2 + cw_ref[1:2] * u1 + cw_ref[2:3] * u
        tail = back2[0:2]
        prev[...] = tail
        cs_ref[...] = tail
        return y

    a = half(wa_ref, cwa_ref, cba_ref, prev_a, csa_ref)
    b = half(wb_ref, cwb_ref, cbb_ref, prev_b, csb_ref)
    act_ref[...] = (a * jax.nn.sigmoid(a) * b).astype(act_ref.dtype)


def _ffn_up_call(h2, w_up_a, w_up_b, conv_w, conv_b, conv_state, nb, seq, tm, act_dtype):
    nrt = seq // tm
    nt = D_FF // FF_TILE
    return pl.pallas_call(
        _ffn_up_kernel,
        grid=(nt, nb, nrt),
        in_specs=[pl.BlockSpec((tm, D_MODEL), lambda j, b, i: (b * nrt + i, 0)),
                  pl.BlockSpec((D_MODEL, FF_TILE), lambda j, b, i: (0, j)),
                  pl.BlockSpec((D_MODEL, FF_TILE), lambda j, b, i: (0, j)),
                  pl.BlockSpec((3, FF_TILE), lambda j, b, i: (0, j)),
                  pl.BlockSpec((3, FF_TILE), lambda j, b, i: (0, nt + j)),
                  pl.BlockSpec((1, FF_TILE), lambda j, b, i: (0, j)),
                  pl.BlockSpec((1, FF_TILE), lambda j, b, i: (0, nt + j)),
                  pl.BlockSpec((None, 2, FF_TILE), lambda j, b, i: (b, 0, j)),
                  pl.BlockSpec((None, 2, FF_TILE), lambda j, b, i: (b, 0, nt + j))],
        out_specs=[pl.BlockSpec((tm, FF_TILE), lambda j, b, i: (b * nrt + i, j)),
                   pl.BlockSpec((None, 2, FF_TILE), lambda j, b, i: (b, 0, j)),
                   pl.BlockSpec((None, 2, FF_TILE), lambda j, b, i: (b, 0, j))],
        out_shape=[jax.ShapeDtypeStruct((nb * seq, D_FF), act_dtype),
                   jax.ShapeDtypeStruct((nb, 2, D_FF), F32),
                   jax.ShapeDtypeStruct((nb, 2, D_FF), F32)],
        scratch_shapes=[pltpu.VMEM((2, FF_TILE), F32), pltpu.VMEM((2, FF_TILE), F32)],
        compiler_params=_cparams("parallel", "parallel", "arbitrary"),
        name="ffn_up_conv",
    )(h2, w_up_a, w_up_b, conv_w, conv_w, conv_b, conv_b, conv_state, conv_state)


def _ffn_down_kernel(a_ref, x1_ref, gt_ref, w_ref, gf_ref, y_ref, *, final_norm):
    x2 = x1_ref[...] + gt_ref[...] * _dot(a_ref[...].astype(BF16), w_ref[...])
    y_ref[...] = _rms(x2, gf_ref[...]) if final_norm else x2


def _ffn_down_call(act, x1, gt3, mod_idx, w_down, g_final, tm, final_norm):
    m = x1.shape[0]
    mrows = gt3.shape[1]
    return pl.pallas_call(
        functools.partial(_ffn_down_kernel, final_norm=final_norm),
        grid=(m // tm,),
        in_specs=[pl.BlockSpec((tm, D_FF), lambda i: (i, 0)),
                  pl.BlockSpec((tm, D_MODEL), lambda i: (i, 0)),
                  pl.BlockSpec((None, mrows, D_MODEL), lambda i: (mod_idx(i), 0, 0)),
                  pl.BlockSpec((D_FF, D_MODEL), lambda i: (0, 0)),
                  pl.BlockSpec((1, D_MODEL), lambda i: (0, 0))],
        out_specs=pl.BlockSpec((tm, D_MODEL), lambda i: (i, 0)),
        out_shape=jax.ShapeDtypeStruct((m, D_MODEL), F32),
        compiler_params=_cparams("parallel"),
        name="ffn_down",
    )(act, x1, gt3, w_down, g_final)


def _relayout_w_in(w_in):
    q_n, kv_n, gate_n, q_r, k_r, v_r, g_r = jnp.split(w_in, [512, 1280, 1304, 1560, 1816, 2328], axis=1)
    gate = gate_n.reshape(D_MODEL, 3, N_KV, REP)
    gate = jnp.transpose(gate, (0, 2, 1, 3)).reshape(D_MODEL, N_KV, 3 * REP)
    gate = jnp.pad(gate, ((0, 0), (0, 0), (0, LANES - 3 * REP))).reshape(D_MODEL, N_KV * LANES)
    return jnp.concatenate([q_n, kv_n, q_r, k_r, v_r, g_r, gate], axis=1).astype(BF16)


def _relayout_cmp(w1, pos):
    w1r = w1.reshape(2, CMP_STRIDE, HD, HD)
    w = jnp.einsum('atdn,gh->tgdahn', w1r, jnp.eye(N_KV, dtype=w1.dtype)).reshape(CMP_STRIDE * N_KV * HD, 2 * N_KV * HD)
    posr = pos.reshape(2, CMP_STRIDE, 1, HD)
    prow = jnp.broadcast_to(posr, (2, CMP_STRIDE, N_KV, HD)).reshape(2, CMP_STRIDE * N_KV * HD)
    prow = jnp.pad(prow, ((0, 14), (0, 0)))
    return w.astype(BF16), prow.astype(BF16)


def _block_diag4(w2k, w2v):
    z = jnp.zeros((HD, HD), w2k.dtype)
    rows = [[w2k, z, z, z], [z, w2k, z, z], [z, z, w2v, z], [z, z, z, w2v]]
    return jnp.block(rows).astype(BF16)


def _rope_tables(pos):
    half = HD // 2
    inv = ROPE_THETA ** (-jnp.arange(half, dtype=F32) / half)
    ang = pos.astype(F32)[:, None] * inv[None, :]
    cos, sin = jnp.cos(ang), jnp.sin(ang)
    return jnp.tile(cos, (1, 4)), jnp.tile(jnp.concatenate([-sin, sin], axis=1), (1, 2))


def _importance_matrix(nc, nsp):
    n = np.arange(nc)[:, None]
    d = n - 4 * np.arange(nsp)[None, :]
    m = ((d >= 0) & (d <= 3)).astype(np.float32) + ((d >= -1) & (d <= 2)).astype(np.float32)
    return jnp.asarray(m, dtype=BF16)


def _expand_matrix_prompt(seq, nsp):
    key = np.arange(seq).reshape(seq // KV_TILE, 1, KV_TILE)
    blk = np.arange(nsp).reshape(1, nsp, 1)
    return jnp.asarray((blk == key // SEL_BLOCK).astype(np.float32), dtype=BF16)


def _expand_matrix_sample(past, nsp):
    row = np.arange(past)
    tok_blk = 2 * (row // PAGE) + (row % 8) // 4
    key = tok_blk.reshape(past // KV_TILE, 1, KV_TILE)
    blk = np.arange(nsp).reshape(1, nsp, 1)
    return jnp.asarray((blk == key).astype(np.float32), dtype=BF16)


def _round_up(x, m):
    return (x + m - 1) // m * m


def kernel(x_prompt, x_sample, c_prompt, c_sample, cache_nsa_kv, cache_win_kv, state_ret, state_conv, page_table,
           w_ada, b_ada, g_norm1, w_in, cmp_pos_k, cmp_w1_k, cmp_w2_k, cmp_pos_v, cmp_w1_v, cmp_w2_v,
           g_nsa_out, g_ret_out, w_out, g_norm2, w_up, conv_w, conv_b, w_down, g_final):
    nb_p, seq, _ = x_prompt.shape
    nb_s, ts, _ = x_sample.shape
    depth = w_ada.shape[0]
    n_pages = page_table.shape[1]
    past = n_pages * PAGE
    wb = cache_win_kv.shape[2]
    m_p, m_s = nb_p * seq, nb_s * ts
    tm_p = 512

    xp = x_prompt.reshape(m_p, D_MODEL)
    xs = x_sample.reshape(m_s, D_MODEL)
    n_c = nb_p + nb_s
    c_all = jnp.pad(jnp.concatenate([c_prompt, c_sample], axis=0), ((0, _round_up(n_c, 8) - n_c), (0, 0)))

    cos_p, sin_p = _rope_tables(jnp.arange(seq, dtype=jnp.int32))
    cos_s, sin_s = _rope_tables(jnp.tile(past + jnp.arange(ts, dtype=jnp.int32), nb_s))
    tabs_p = _ret_tables(min(RET_CHUNK, seq))
    tabs_s = _ret_tables(ts)
    nc_p, nc_s = seq // CMP_STRIDE, past // CMP_STRIDE
    nsp_p = _round_up(seq // SEL_BLOCK, LANES)
    nsp_s = _round_up(past // SEL_BLOCK + 1, LANES)
    mmat_p, emat_p = _importance_matrix(nc_p, nsp_p), _expand_matrix_prompt(seq, nsp_p)
    mmat_s, emat_s = _importance_matrix(nc_s, nsp_s), _expand_matrix_sample(past, nsp_s)

    tiles_per_batch = seq // tm_p
    idx_p = lambda i: i // tiles_per_batch
    tab_p = lambda i: i % tiles_per_batch
    idx_s = lambda i: 0

    outs = {k: [] for k in ('kv_p', 'kv_s', 'win_p', 'win_s', 'ret_p', 'ret_s', 'conv_p', 'conv_s')}
    for l in range(depth):
        mod = _mod_call(c_all, w_ada[l], b_ada[l])
        mods_p = [a.reshape(nb_p, 1, D_MODEL) for a in jnp.split(mod[:nb_p], 6, axis=1)]
        mods_s = [jnp.repeat(a, ts, axis=0).reshape(1, m_s, D_MODEL) for a in jnp.split(mod[nb_p:n_c], 6, axis=1)]
        w_in_re = _relayout_w_in(w_in[l])
        wk, pos_k = _relayout_cmp(cmp_w1_k[l], cmp_pos_k[l])
        wv, pos_v = _relayout_cmp(cmp_w1_v[l], cmp_pos_v[l])
        pos_kv = jnp.stack([pos_k, pos_v])
        w2bd = _block_diag4(cmp_w2_k[l], cmp_w2_v[l])
        g1 = g_norm1[l].reshape(1, -1)
        g2 = g_norm2[l].reshape(1, -1)
        g_nsa = g_nsa_out[l].reshape(1, -1)
        g_ret = g_ret_out[l].reshape(1, -1)
        w_out_bf = w_out[l].astype(BF16)
        w_up_a = w_up[l][:, :D_FF].astype(BF16)
        w_up_b = w_up[l][:, D_FF:].astype(BF16)
        w_down_bf = w_down[l].astype(BF16)

        sh1, sc1, gt1, sh2, sc2, gt2 = mods_p
        qh_p, kvn_p, kvw_p, kvh_p, ret_p, gat_p = _inproj_call(
            xp, sc1, sh1, idx_p, g1, w_in_re, cos_p, sin_p, tab_p, tm_p, BF16)
        sh1s, sc1s, gt1s, sh2s, sc2s, gt2s = mods_s
        qh_s, kvn_s, kvw_s, kvh_s, ret_s, gat_s = _inproj_call(
            xs, sc1s, sh1s, idx_s, g1, w_in_re, cos_s, sin_s, idx_s, m_s, F32)

        ab_p = _cmp_prompt_call(kvn_p, wk, wv, pos_kv).reshape(nb_p, nc_p, 512)
        ab_s, ks_s = _cmp_sample_call(cache_nsa_kv[:, l], page_table, wk, wv, pos_kv)
        cmp_p = _cmp_fin_call(ab_p, w2bd)
        cmp_s = _cmp_fin_call(ab_s, w2bd)

        o_nsa_p = _nsa_prompt_call(qh_p, gat_p, cmp_p, kvh_p, mmat_p, emat_p, nb_p, seq, 0)
        win_cache = cache_win_kv[:, l].reshape(nb_s, wb, 256)
        o_nsa_s = _nsa_sample_call(qh_s, gat_s, cmp_s, ks_s, kvh_s, win_cache, mmat_s, emat_s, nb_s, ts, past, 0)

        s0_p = jnp.zeros((nb_p, N_RET, DK_RET, DV_RET), F32)
        o_ret_p, s_new_p = _ret_call(ret_p, s0_p, tabs_p, g_ret, nb_p, seq, min(RET_CHUNK, seq))
        o_ret_s, s_new_s = _ret_call(ret_s, state_ret[:, l], tabs_s, g_ret, nb_s, ts, ts)

        x1_p, h2_p = _outproj_call(o_nsa_p, o_ret_p, xp, gt1, sc2, sh2, idx_p, g_nsa, g2, w_out_bf, tm_p, BF16)
        x1_s, h2_s = _outproj_call(o_nsa_s, o_ret_s, xs, gt1s, sc2s, sh2s, idx_s, g_nsa, g2, w_out_bf, m_s, F32)

        conv0_p = jnp.zeros((nb_p, 2, 2 * D_FF), F32)
        act_p, csa_p, csb_p = _ffn_up_call(h2_p, w_up_a, w_up_b, conv_w[l], conv_b[l].reshape(1, -1), conv0_p,
                                           nb_p, seq, tm_p, BF16)
        act_s, csa_s, csb_s = _ffn_up_call(h2_s, w_up_a, w_up_b, conv_w[l], conv_b[l].reshape(1, -1),
                                           state_conv[:, l], nb_s, ts, ts, F32)
        last = l == depth - 1
        gf = g_final.reshape(1, -1)
        xp = _ffn_down_call(act_p, x1_p, gt2, idx_p, w_down_bf, gf, tm_p, last)
        xs = _ffn_down_call(act_s, x1_s, gt2s, idx_s, w_down_bf, gf, m_s, last)

        outs['kv_p'].append(kvn_p.reshape(nb_p, seq, 4, N_KV, HD))
        outs['kv_s'].append(kvn_s.reshape(nb_s, ts, 4, N_KV, HD))
        win_rows_p = kvw_p.reshape(nb_p, seq, 2, N_KV, HD)
        outs['win_p'].append(win_rows_p[:, seq - min(WINDOW, seq):])
        win_all = jnp.concatenate([cache_win_kv[:, l], kvw_s.reshape(nb_s, ts, 2, N_KV, HD)], axis=1)
        outs['win_s'].append(win_all[:, win_all.shape[1] - min(WINDOW, past + ts):])
        outs['ret_p'].append(s_new_p)
        outs['ret_s'].append(s_new_s)
        outs['conv_p'].append(jnp.concatenate([csa_p, csb_p], axis=-1))
        outs['conv_s'].append(jnp.concatenate([csa_s, csb_s], axis=-1))

    st = lambda k: jnp.stack(outs[k], axis=1)
    return (xp.reshape(nb_p, seq, D_MODEL), xs.reshape(nb_s, ts, D_MODEL),
            st('kv_p'), st('kv_s'), st('win_p'), st('win_s'), st('ret_p'), st('ret_s'), st('conv_p'), st('conv_s'))
```

```python
import functools

import numpy as np
import jax
import jax.numpy as jnp
from jax import lax
from jax.experimental import pallas as pl
from jax.experimental.pallas import tpu as pltpu

F32 = jnp.float32
BF16 = jnp.bfloat16

D_MODEL = 1024
PAGE = 128
HD = 64
N_KV = 2
REP = 4
D_NSA = 512
CMP_STRIDE = 16
SEL_BLOCK = 64
N_SEL = 16
WINDOW = 512
Q_TILE = 512
KV_TILE = 512
FORCE_BONUS = 100.0
N_RET = 4
DK_RET = 64
DV_RET = 128
D_RET = 512
RET_CHUNK = 128
D_FF = 2816
ROPE_THETA = 10000.0
EPS = 1e-6
NEG_INF = -1e30
LANES = 128
W_IN_COLS = 3072
VMEM_LIMIT = 56 * 1024 * 1024


def _cparams(*sem):
    return pltpu.CompilerParams(dimension_semantics=sem, vmem_limit_bytes=VMEM_LIMIT)


def _dot(a, b):
    return jnp.dot(a, b, preferred_element_type=F32)


def _dot_nt(a, b):
    return lax.dot_general(a, b, (((1,), (1,)), ((), ())), preferred_element_type=F32)


def _dot_tn(a, b):
    return lax.dot_general(a, b, (((0,), (0,)), ((), ())), preferred_element_type=F32)


def _dot_split3(p, m_bf16):
    hi = p.astype(BF16)
    r1 = p - hi.astype(F32)
    mid = r1.astype(BF16)
    lo = (r1 - mid.astype(F32)).astype(BF16)
    return _dot(hi, m_bf16) + _dot(mid, m_bf16) + _dot(lo, m_bf16)


def _rms(x, g):
    return x * lax.rsqrt(jnp.mean(x * x, axis=-1, keepdims=True) + EPS) * g


def _masked_softmax(s, mask):
    s = jnp.where(mask, s, NEG_INF)
    m = jnp.max(s, axis=-1, keepdims=True)
    e = jnp.where(mask, jnp.exp(s - m), 0.0)
    return e / jnp.maximum(jnp.sum(e, axis=-1, keepdims=True), 1e-30)


def _topk_mask_t(score_t, n_sel):
    nb = score_t.shape[0]
    blk = lax.broadcasted_iota(jnp.int32, score_t.shape, 0)

    def body(_, carry):
        work, sel = carry
        m = jnp.max(work, axis=0, keepdims=True)
        idx = jnp.min(jnp.where(work == m, blk, nb), axis=0, keepdims=True)
        pick = blk == idx
        return jnp.where(pick, -jnp.inf, work), jnp.where(pick, 1.0, sel)

    _, sel = lax.fori_loop(0, n_sel, body, (score_t, jnp.zeros(score_t.shape, F32)))
    return sel


def _selection(psum, mmat, qpos, ns, n_sel):
    imp = _dot_split3(psum, mmat)
    blk = lax.broadcasted_iota(jnp.int32, (1, imp.shape[1]), 1)
    qblk = qpos >> 6
    valid = (blk * SEL_BLOCK <= qpos) & (blk < ns)
    forced = (blk == 0) | (blk == qblk) | (blk == qblk - 1)
    score = jnp.where(valid, imp + jnp.where(forced, FORCE_BONUS, 0.0), NEG_INF)
    score = jnp.where(blk < ns, score, -jnp.inf)
    sel = _topk_mask_t(score.T, n_sel).T
    return jnp.where(valid, sel, 0.0)


def _mod_kernel(c_ref, w_ref, b_ref, o_ref):
    c = c_ref[...]
    o_ref[...] = _dot(c * jax.nn.sigmoid(c), w_ref[...]) + b_ref[...]


def _mod_call(c_all, w_ada, b_ada):
    n = c_all.shape[0]
    tn = 1536
    return pl.pallas_call(
        _mod_kernel,
        grid=(w_ada.shape[1] // tn,),
        in_specs=[pl.BlockSpec((n, D_MODEL), lambda j: (0, 0)),
                  pl.BlockSpec((D_MODEL, tn), lambda j: (0, j)),
                  pl.BlockSpec((1, tn), lambda j: (0, j))],
        out_specs=pl.BlockSpec((n, tn), lambda j: (0, j)),
        out_shape=jax.ShapeDtypeStruct((n, w_ada.shape[1]), F32),
        compiler_params=_cparams("arbitrary"),
        name="adaln_mod",
    )(c_all, w_ada, b_ada.reshape(1, -1))


def _inproj_kernel(x_ref, sc_ref, sh_ref, g1_ref, w_ref, cos_ref, sin_ref,
                   qh_ref, kvn_ref, kvw_ref, kvh_ref, ret_ref, gat_ref):
    h = (_rms(x_ref[...], g1_ref[...]) * (1.0 + sc_ref[...]) + sh_ref[...]).astype(BF16)
    cos = cos_ref[...]
    sin = sin_ref[...]
    lane = lax.broadcasted_iota(jnp.int32, (1, LANES), 1)
    first_half = (lane & (HD - 1)) < HD // 2

    def seg(c0, n):
        return _dot(h, w_ref[:, c0:c0 + n])

    def rope(a):
        sw = jnp.where(first_half, pltpu.roll(a, LANES - HD // 2, 1), pltpu.roll(a, HD // 2, 1))
        return a * cos + sw * sin

    for j in range(4):
        q = rope(seg(LANES * j, LANES)) * (HD ** -0.5)
        qh_ref[2 * j] = q[:, :HD].astype(qh_ref.dtype)
        qh_ref[2 * j + 1] = q[:, HD:].astype(qh_ref.dtype)
    for s in range(6):
        a = seg(512 + LANES * s, LANES)
        if s % 2 == 0:
            a = rope(a)
        if s < 4:
            kvn_ref[:, LANES * s:LANES * (s + 1)] = a
        else:
            kvw_ref[:, LANES * (s - 4):LANES * (s - 3)] = a
        if s >= 2:
            kvh_ref[2 * (s - 2)] = a[:, :HD].astype(kvh_ref.dtype)
            kvh_ref[2 * (s - 2) + 1] = a[:, HD:].astype(kvh_ref.dtype)
    for j in range(2):
        ret_ref[:, LANES * j:LANES * (j + 1)] = rope(seg(1280 + LANES * j, LANES)) * (DK_RET ** -0.5)
        ret_ref[:, 256 + LANES * j:256 + LANES * (j + 1)] = rope(seg(1536 + LANES * j, LANES))
    ret_ref[:, 512:1024] = seg(1792, 512)
    ret_ref[:, 1024:1536] = seg(2304, 512)
    gat_ref[...] = seg(2816, 256)


def _inproj_call(x2d, sc3, sh3, mod_idx, g1, w_in_re, cos, sin, tab_idx, tm, head_dtype):
    m = x2d.shape[0]
    mrows = sc3.shape[1]
    return pl.pallas_call(
        _inproj_kernel,
        grid=(m // tm,),
        in_specs=[pl.BlockSpec((tm, D_MODEL), lambda i: (i, 0)),
                  pl.BlockSpec((None, mrows, D_MODEL), lambda i: (mod_idx(i), 0, 0)),
                  pl.BlockSpec((None, mrows, D_MODEL), lambda i: (mod_idx(i), 0, 0)),
                  pl.BlockSpec((1, D_MODEL), lambda i: (0, 0)),
                  pl.BlockSpec((D_MODEL, W_IN_COLS), lambda i: (0, 0)),
                  pl.BlockSpec((tm, LANES), lambda i: (tab_idx(i), 0)),
                  pl.BlockSpec((tm, LANES), lambda i: (tab_idx(i), 0))],
        out_specs=[pl.BlockSpec((8, tm, HD), lambda i: (0, i, 0)),
                   pl.BlockSpec((tm, 512), lambda i: (i, 0)),
                   pl.BlockSpec((tm, 256), lambda i: (i, 0)),
                   pl.BlockSpec((8, tm, HD), lambda i: (0, i, 0)),
                   pl.BlockSpec((tm, 1536), lambda i: (i, 0)),
                   pl.BlockSpec((tm, 256), lambda i: (i, 0))],
        out_shape=[jax.ShapeDtypeStruct((8, m, HD), head_dtype),
                   jax.ShapeDtypeStruct((m, 512), F32),
                   jax.ShapeDtypeStruct((m, 256), F32),
                   jax.ShapeDtypeStruct((8, m, HD), head_dtype),
                   jax.ShapeDtypeStruct((m, 1536), F32),
                   jax.ShapeDtypeStruct((m, 256), F32)],
        compiler_params=_cparams("parallel"),
        name="in_proj",
    )(x2d, sc3, sh3, g1, w_in_re, cos, sin)


def _cmp_partial(slab, wk_ref, wv_ref, pos_ref, ab_ref):
    for kind, w_ref in ((0, wk_ref), (1, wv_ref)):
        xcat = jnp.concatenate([slab(t, kind) for t in range(CMP_STRIDE)], axis=1).astype(BF16)
        r = _dot(xcat, w_ref[...])
        bias = _dot(pos_ref[kind], w_ref[...])
        ab_ref[:, LANES * kind:LANES * (kind + 1)] = r[:, :LANES] + bias[0:1, :LANES]
        ab_ref[:, 256 + LANES * kind:256 + LANES * (kind + 1)] = r[:, LANES:] + bias[1:2, LANES:]


def _cmp_prompt_kernel(x_ref, wk_ref, wv_ref, pos_ref, ab_ref):
    _cmp_partial(lambda t, kind: x_ref[:, 512 * t + LANES * kind:512 * t + LANES * (kind + 1)],
                 wk_ref, wv_ref, pos_ref, ab_ref)


def _cmp_prompt_call(kvn, wk, wv, pos):
    rows = kvn.shape[0] // CMP_STRIDE
    xv = kvn.reshape(rows, CMP_STRIDE * 512)
    tr = min(128, rows)
    return pl.pallas_call(
        _cmp_prompt_kernel,
        grid=(rows // tr,),
        in_specs=[pl.BlockSpec((tr, CMP_STRIDE * 512), lambda i: (i, 0)),
                  pl.BlockSpec(wk.shape, lambda i: (0, 0)),
                  pl.BlockSpec(wv.shape, lambda i: (0, 0)),
                  pl.BlockSpec(pos.shape, lambda i: (0, 0, 0))],
        out_specs=pl.BlockSpec((tr, 512), lambda i: (i, 0)),
        out_shape=jax.ShapeDtypeStruct((rows, 512), F32),
        compiler_params=_cparams("parallel"),
        name="cmp_partial_prompt",
    )(xv, wk, wv, pos)


PAGES_PER_STEP = 8


def _cmp_sample_kernel(pt_ref, *refs):
    pages = refs[:PAGES_PER_STEP]
    wk_ref, wv_ref, pos_ref, ab_ref, ks_ref = refs[PAGES_PER_STEP:]

    def slab(t, kind):
        c0 = 512 * t + LANES * kind
        return jnp.concatenate([p[:, c0:c0 + LANES] for p in pages], axis=0)

    _cmp_partial(slab, wk_ref, wv_ref, pos_ref, ab_ref)
    rows = [p[:, 512 * t + 256:512 * t + 512] for p in pages for t in range(CMP_STRIDE)]
    ks_ref[...] = jnp.concatenate(rows, axis=0).astype(BF16)


def _cmp_sample_call(cache, page_table, wk, wv, pos):
    n_pool = cache.shape[0]
    nb, n_pages = page_table.shape
    chunks = PAGE // CMP_STRIDE
    cv = cache.reshape(n_pool, chunks, CMP_STRIDE * 512)
    steps = n_pages // PAGES_PER_STEP

    def page_spec(k):
        return pl.BlockSpec((None, chunks, CMP_STRIDE * 512),
                            lambda b, j, pt: (pt[b, j * PAGES_PER_STEP + k], 0, 0))

    grid_spec = pltpu.PrefetchScalarGridSpec(
        num_scalar_prefetch=1,
        grid=(nb, steps),
        in_specs=[page_spec(k) for k in range(PAGES_PER_STEP)] + [
            pl.BlockSpec(wk.shape, lambda b, j, pt: (0, 0)),
            pl.BlockSpec(wv.shape, lambda b, j, pt: (0, 0)),
            pl.BlockSpec(pos.shape, lambda b, j, pt: (0, 0, 0))],
        out_specs=[pl.BlockSpec((None, PAGES_PER_STEP * chunks, 512), lambda b, j, pt: (b, j, 0)),
                   pl.BlockSpec((None, PAGES_PER_STEP * PAGE, 256), lambda b, j, pt: (b, j, 0))],
    )
    return pl.pallas_call(
        _cmp_sample_kernel,
        grid_spec=grid_spec,
        out_shape=[jax.ShapeDtypeStruct((nb, n_pages * chunks, 512), F32),
                   jax.ShapeDtypeStruct((nb, n_pages * PAGE, 256), BF16)],
        compiler_params=_cparams("parallel", "arbitrary"),
        name="cmp_partial_sample",
    )(page_table, *([cv] * PAGES_PER_STEP), wk, wv, pos)


def _cmp_fin_kernel(ab_ref, w2_ref, o_ref):
    nc = ab_ref.shape[0]
    a = ab_ref[:, :256]
    b_next = pltpu.roll(ab_ref[:, 256:], nc - 1, 0)
    hid = jax.nn.gelu(a + b_next)
    out = _dot(hid.astype(BF16), w2_ref[...])
    row = lax.broadcasted_iota(jnp.int32, (nc, 1), 0)
    out = jnp.where(row < nc - 1, out, 0.0)
    for s in range(4):
        o_ref[s] = out[:, HD * s:HD * (s + 1)]


def _cmp_fin_call(ab, w2bd):
    nb, nc, _ = ab.shape
    return pl.pallas_call(
        _cmp_fin_kernel,
        grid=(nb,),
        in_specs=[pl.BlockSpec((None, nc, 512), lambda b: (b, 0, 0)),
                  pl.BlockSpec((256, 256), lambda b: (0, 0))],
        out_specs=pl.BlockSpec((None, 4, nc, HD), lambda b: (b, 0, 0, 0)),
        out_shape=jax.ShapeDtypeStruct((nb, 4, nc, HD), F32),
        compiler_params=_cparams("parallel"),
        name="cmp_finish",
    )(ab, w2bd)


def _nsa_prompt_kernel(q_ref, gat_ref, kc_ref, vc_ref, ka_ref, va_ref, kw_ref, vw_ref, mmat_ref,
                       o_ref, qa_sc, s_sc, mcur_sc, m_sc, acc_sc, *, ns, n_sel):
    i = pl.program_id(2)
    start = i * Q_TILE
    qpos = start + lax.broadcasted_iota(jnp.int32, (Q_TILE, 1), 0)
    nc = kc_ref.shape[0]
    kc = kc_ref[...].astype(BF16)
    vc = vc_ref[...].astype(BF16)
    head_rows = [slice(r * Q_TILE, (r + 1) * Q_TILE) for r in range(REP)]

    c_end = lax.broadcasted_iota(jnp.int32, (1, nc), 1) * CMP_STRIDE + (2 * CMP_STRIDE - 1)
    cbias = jnp.where(c_end <= qpos, 0.0, NEG_INF)
    any_valid = qpos >= 2 * CMP_STRIDE - 1
    psum = jnp.zeros((Q_TILE, nc), F32)
    o_cmp = []
    for r in range(REP):
        s = _dot_nt(q_ref[r], kc) + cbias
        e = jnp.exp(s - jnp.max(s, axis=-1, keepdims=True))
        norm = jnp.where(any_valid, 1.0 / jnp.maximum(jnp.sum(e, axis=-1, keepdims=True), 1e-30), 0.0)
        p = e * norm
        psum = psum + p
        o_cmp.append(_dot(p.astype(BF16), vc))

    sel = _selection(psum, mmat_ref[...], qpos, ns, n_sel)
    selneg = jnp.where(sel > 0.5, 0.0, NEG_INF).astype(BF16)
    for r in range(REP):
        qa_sc[head_rows[r], :] = jnp.concatenate([selneg, q_ref[r]], axis=1)

    n_full = start // KV_TILE

    def scores(j):
        k0 = pl.multiple_of(j * KV_TILE, KV_TILE)
        return _dot_nt(qa_sc[...], ka_ref[pl.ds(k0, KV_TILE), :])

    def consume(slot, t):
        v0 = pl.multiple_of(t * KV_TILE, KV_TILE)
        v_t = va_ref[pl.ds(v0, KV_TILE), :]
        for r in range(REP):
            rows = head_rows[r]
            m_old = m_sc[rows]
            m_new = jnp.maximum(m_old, mcur_sc[slot, rows])
            p = jnp.exp(s_sc[slot, rows] - m_new).astype(BF16)
            acc_sc[rows] = jnp.exp(m_old - m_new) * acc_sc[rows] + _dot(p, v_t)
            m_sc[rows] = m_new

    kpos = n_full * KV_TILE + lax.broadcasted_iota(jnp.int32, (1, KV_TILE), 1)
    causal_bias = jnp.where(kpos <= qpos, 0.0, NEG_INF)
    s_diag = scores(n_full)
    for r in range(REP):
        s_r = s_diag[head_rows[r]] + causal_bias
        s_sc[0, head_rows[r]] = s_r
        mcur_sc[0, head_rows[r]] = jnp.max(s_r, axis=-1, keepdims=True)
    m_sc[...] = jnp.full(m_sc.shape, NEG_INF, F32)
    acc_sc[...] = jnp.zeros(acc_sc.shape, F32)

    def body(j, carry):
        consume(j & 1, jnp.where(j == 0, n_full, j - 1))
        s_new = scores(j)
        slot_new = (j + 1) & 1
        s_sc[slot_new] = s_new
        mcur_sc[slot_new] = jnp.max(s_new, axis=-1, keepdims=True)
        return carry

    lax.fori_loop(0, n_full, body, 0)
    consume(n_full & 1, jnp.maximum(n_full - 1, 0))

    base = pl.multiple_of(jnp.maximum(start - WINDOW, 0), Q_TILE)
    kw = kw_ref[pl.ds(base, WINDOW + Q_TILE), :]
    vw = vw_ref[pl.ds(base, WINDOW + Q_TILE), :]
    dpos = qpos - (base + lax.broadcasted_iota(jnp.int32, (1, WINDOW + Q_TILE), 1))
    wbias = jnp.where((dpos >= 0) & (dpos < WINDOW), 0.0, NEG_INF)

    gate = jax.nn.sigmoid(gat_ref[...])
    for r in range(REP):
        rows = head_rows[r]
        s = _dot_nt(q_ref[r], kw) + wbias
        e = jnp.exp(s - jnp.max(s, axis=-1, keepdims=True)).astype(BF16)
        ow = _dot(e, vw)
        o_win = ow[:, :HD] / jnp.maximum(ow[:, HD:HD + 1], 1e-30)
        acc = acc_sc[rows]
        o_slc = acc[:, :HD] / jnp.maximum(acc[:, HD:HD + 1], 1e-30)
        o_ref[:, HD * r:HD * (r + 1)] = (gate[:, r:r + 1] * o_cmp[r] + gate[:, REP + r:REP + r + 1] * o_slc
                                        + gate[:, 2 * REP + r:2 * REP + r + 1] * o_win)


def _nsa_prompt_call(qh, gat, cmp, kaug, vaug, kwin, vwaug, mmat, nb, seq, cmp_off):
    nq = seq // Q_TILE
    nc = cmp.shape[2]
    ns = seq // SEL_BLOCK
    assert ns <= LANES and mmat.shape[1] == LANES
    kern = functools.partial(_nsa_prompt_kernel, ns=ns, n_sel=min(N_SEL, ns))

    def kv_spec(width):
        return pl.BlockSpec((None, seq, width), lambda b, g, i: (g, b, 0))

    return pl.pallas_call(
        kern,
        grid=(nb, N_KV, nq),
        in_specs=[pl.BlockSpec((REP, Q_TILE, HD), lambda b, g, i: (g, b * nq + i, 0)),
                  pl.BlockSpec((Q_TILE, LANES), lambda b, g, i: (b * nq + i, g)),
                  pl.BlockSpec((None, None, nc, HD), lambda b, g, i: (cmp_off + b, g, 0, 0)),
                  pl.BlockSpec((None, None, nc, HD), lambda b, g, i: (cmp_off + b, 2 + g, 0, 0)),
                  kv_spec(LANES + HD), kv_spec(LANES), kv_spec(HD), kv_spec(LANES),
                  pl.BlockSpec(mmat.shape, lambda b, g, i: (0, 0))],
        out_specs=pl.BlockSpec((Q_TILE, REP * HD), lambda b, g, i: (b * nq + i, g)),
        out_shape=jax.ShapeDtypeStruct((nb * seq, D_NSA), F32),
        scratch_shapes=[pltpu.VMEM((REP * Q_TILE, LANES + HD), BF16),
                        pltpu.VMEM((2, REP * Q_TILE, KV_TILE), F32),
                        pltpu.VMEM((2, REP * Q_TILE, 1), F32),
                        pltpu.VMEM((REP * Q_TILE, 1), F32),
                        pltpu.VMEM((REP * Q_TILE, LANES), F32)],
        compiler_params=_cparams("parallel", "parallel", "arbitrary"),
        name="nsa_prompt",
    )(qh, gat, cmp, cmp, kaug, vaug, kwin, vwaug, mmat)


def _nsa_sample_kernel(q_ref, gat_ref, cmp_ref, ks_ref, new_ref, win_ref, mmat_ref, emat_ref, o_ref,
                       *, past, ts, ns, n_sel):
    nc = cmp_ref.shape[1]
    wb = win_ref.shape[0]
    rows = REP * ts
    n_tiles = past // KV_TILE
    t_idx = lax.broadcasted_iota(jnp.int32, (ts, 1), 0)
    qpos = jnp.concatenate([past + t_idx] * REP, axis=0)
    qpos_pad = past + lax.broadcasted_iota(jnp.int32, (LANES, 1), 0)
    gate = jax.nn.sigmoid(gat_ref[...])
    pad_rows = jnp.zeros((LANES - ts, HD), F32)

    def padded(slab):
        return jnp.concatenate([new_ref[slab], pad_rows], axis=0).astype(BF16)

    for g in range(N_KV):
        q = jnp.concatenate([q_ref[REP * g + r] for r in range(REP)], axis=0).astype(BF16)
        kc = cmp_ref[g].astype(BF16)
        vc = cmp_ref[2 + g].astype(BF16)
        c_end = lax.broadcasted_iota(jnp.int32, (1, nc), 1) * CMP_STRIDE + (2 * CMP_STRIDE - 1)
        p = _masked_softmax(_dot_nt(q, kc), c_end <= qpos)
        o_cmp = _dot(p.astype(BF16), vc)
        psum = p[0:ts]
        for r in range(1, REP):
            psum = psum + p[r * ts:(r + 1) * ts]
        psum = jnp.concatenate([psum, jnp.zeros((LANES - ts, nc), F32)], axis=0)
        sel = _selection(psum, mmat_ref[...], qpos_pad, ns, n_sel)[0:ts]
        sel = jnp.concatenate([sel] * REP, axis=0)
        sel_bf = sel.astype(BF16)

        def body(j, carry):
            m_old, l_old, acc = carry
            k0 = pl.multiple_of(j * KV_TILE, KV_TILE)
            k_t = ks_ref[pl.ds(k0, KV_TILE), HD * g:HD * (g + 1)]
            v_t = ks_ref[pl.ds(k0, KV_TILE), 128 + HD * g:128 + HD * (g + 1)]
            valid = _dot(sel_bf, emat_ref[j]) > 0.5
            s = jnp.where(valid, _dot_nt(q, k_t), NEG_INF)
            m_new = jnp.maximum(m_old, jnp.max(s, axis=-1, keepdims=True))
            alpha = jnp.exp(m_old - m_new)
            pj = jnp.where(valid, jnp.exp(s - m_new), 0.0)
            return (m_new, alpha * l_old + jnp.sum(pj, axis=-1, keepdims=True),
                    alpha * acc + _dot(pj.astype(BF16), v_t))

        m_old, l_old, acc = lax.fori_loop(
            0, n_tiles, body,
            (jnp.full((rows, 1), NEG_INF, F32), jnp.zeros((rows, 1), F32), jnp.zeros((rows, HD), F32)))
        kidx = lax.broadcasted_iota(jnp.int32, (1, LANES), 1)
        valid = (sel[:, ns - 1:ns] > 0.5) & (past + kidx <= qpos)
        s = jnp.where(valid, _dot_nt(q, padded(g)), NEG_INF)
        m_new = jnp.maximum(m_old, jnp.max(s, axis=-1, keepdims=True))
        alpha = jnp.exp(m_old - m_new)
        pj = jnp.where(valid, jnp.exp(s - m_new), 0.0)
        l_new = alpha * l_old + jnp.sum(pj, axis=-1, keepdims=True)
        acc = alpha * acc + _dot(pj.astype(BF16), padded(2 + g))
        o_slc = acc / jnp.maximum(l_new, 1e-30)

        s_w = jnp.concatenate([_dot_nt(q, win_ref[:, HD * g:HD * (g + 1)].astype(BF16)),
                               _dot_nt(q, padded(4 + g))], axis=1)
        kpos = past - wb + lax.broadcasted_iota(jnp.int32, (1, wb + LANES), 1)
        dpos = qpos - kpos
        pw = _masked_softmax(s_w, (dpos >= 0) & (dpos < WINDOW) & (kpos >= 0))
        o_win = (_dot(pw[:, :wb].astype(BF16), win_ref[:, 128 + HD * g:128 + HD * (g + 1)].astype(BF16))
                 + _dot(pw[:, wb:].astype(BF16), padded(6 + g)))

        for r in range(REP):
            rs = slice(r * ts, (r + 1) * ts)
            c = LANES * g + r
            o_ref[:, HD * (REP * g + r):HD * (REP * g + r + 1)] = (
                gate[:, c:c + 1] * o_cmp[rs] + gate[:, c + REP:c + REP + 1] * o_slc[rs]
                + gate[:, c + 2 * REP:c + 2 * REP + 1] * o_win[rs])


def _nsa_sample_call(qh, gat, cmp, ks, kvh_new, win, mmat, emat, nb, ts, past, cmp_off):
    nc = cmp.shape[2]
    ns = past // SEL_BLOCK + 1
    kern = functools.partial(_nsa_sample_kernel, past=past, ts=ts, ns=ns, n_sel=min(N_SEL, ns))
    return pl.pallas_call(
        kern,
        grid=(nb,),
        in_specs=[pl.BlockSpec((8, ts, HD), lambda b: (0, b, 0)),
                  pl.BlockSpec((ts, 256), lambda b: (b, 0)),
                  pl.BlockSpec((None, 4, nc, HD), lambda b: (cmp_off + b, 0, 0, 0)),
                  pl.BlockSpec((None, past, 256), lambda b: (b, 0, 0)),
                  pl.BlockSpec((8, ts, HD), lambda b: (0, b, 0)),
                  pl.BlockSpec((None, win.shape[1], 256), lambda b: (b, 0, 0)),
                  pl.BlockSpec(mmat.shape, lambda b: (0, 0)),
                  pl.BlockSpec(emat.shape, lambda b: (0, 0, 0))],
        out_specs=pl.BlockSpec((ts, D_NSA), lambda b: (b, 0)),
        out_shape=jax.ShapeDtypeStruct((nb * ts, D_NSA), F32),
        compiler_params=_cparams("parallel"),
        name="nsa_sample",
    )(qh, gat, cmp, ks, kvh_new, win, mmat, emat)


def _ret_kernel(x_ref, s0_ref, dmat_ref, xi_ref, zeta_ref, gc_ref, gro_ref, o_ref, s_out_ref, s_sc, pad_sc,
                *, rows):
    @pl.when(pl.program_id(1) == 0)
    def _():
        s_sc[...] = s0_ref[...]

    if rows < RET_CHUNK:
        pad_sc[...] = jnp.zeros(pad_sc.shape, F32)
        pad_sc[0:rows, :] = x_ref[...]
        x = pad_sc
    else:
        x = x_ref
    for h in range(N_RET):
        q = x[:, DK_RET * h:DK_RET * (h + 1)].astype(BF16)
        k = x[:, 256 + DK_RET * h:256 + DK_RET * (h + 1)]
        v = x[:, 512 + DV_RET * h:512 + DV_RET * (h + 1)].astype(BF16)
        gr = x[:, 1024 + DV_RET * h:1024 + DV_RET * (h + 1)]
        att = _dot_nt(q, k.astype(BF16)) * dmat_ref[h]
        s_old = s_sc[h]
        o = _dot(att.astype(BF16), v) + _dot(q, s_old.astype(BF16)) * xi_ref[h]
        s_sc[h] = gc_ref[h] * s_old + _dot_tn((k * zeta_ref[h]).astype(BF16), v)
        mu = jnp.mean(o, axis=-1, keepdims=True)
        var = jnp.mean(jnp.square(o - mu), axis=-1, keepdims=True)
        y = (o - mu) * lax.rsqrt(var + EPS) * gro_ref[:, DV_RET * h:DV_RET * (h + 1)]
        res = gr * jax.nn.sigmoid(gr) * y
        o_ref[:, DV_RET * h:DV_RET * (h + 1)] = res[0:rows]
    s_out_ref[...] = s_sc[...]


def _ret_call(ret, s0, tabs, g_ret_out, nb, seq, rows):
    nchunks = seq // rows
    dmat, xi, zeta, gc = tabs
    kern = functools.partial(_ret_kernel, rows=rows)
    full3 = lambda b, c: (0, 0, 0)
    return pl.pallas_call(
        kern,
        grid=(nb, nchunks),
        in_specs=[pl.BlockSpec((rows, 1536), lambda b, c: (b * nchunks + c, 0)),
                  pl.BlockSpec((None, N_RET, DK_RET, DV_RET), lambda b, c: (b, 0, 0, 0)),
                  pl.BlockSpec(dmat.shape, full3), pl.BlockSpec(xi.shape, full3),
                  pl.BlockSpec(zeta.shape, full3), pl.BlockSpec(gc.shape, full3),
                  pl.BlockSpec((1, D_RET), lambda b, c: (0, 0))],
        out_specs=[pl.BlockSpec((rows, D_RET), lambda b, c: (b * nchunks + c, 0)),
                   pl.BlockSpec((None, N_RET, DK_RET, DV_RET), lambda b, c: (b, 0, 0, 0))],
        out_shape=[jax.ShapeDtypeStruct((nb * seq, D_RET), F32),
                   jax.ShapeDtypeStruct((nb, N_RET, DK_RET, DV_RET), F32)],
        scratch_shapes=[pltpu.VMEM((N_RET, DK_RET, DV_RET), F32),
                        pltpu.VMEM((RET_CHUNK, 1536), F32)],
        compiler_params=_cparams("parallel", "arbitrary"),
        name="retention",
    )(ret, s0, dmat, xi, zeta, gc, g_ret_out)


def _ret_tables(chunk):
    c = RET_CHUNK
    log_g = jnp.log1p(-jnp.power(2.0, -5.0 - jnp.arange(N_RET, dtype=F32)))
    i = jnp.arange(c, dtype=F32)
    diff = i[:, None] - i[None, :]
    dmat = jnp.where(diff >= 0, jnp.exp(jnp.maximum(diff, 0.0)[None] * log_g[:, None, None]), 0.0)
    xi = jnp.exp((i[None, :] + 1.0) * log_g[:, None])
    zeta = jnp.where(i[None, :] < chunk, jnp.exp((chunk - 1.0 - i)[None, :] * log_g[:, None]), 0.0)
    g_c = jnp.exp(chunk * log_g)
    return (dmat,
            jnp.broadcast_to(xi[:, :, None], (N_RET, c, DV_RET)),
            jnp.broadcast_to(zeta[:, :, None], (N_RET, c, DK_RET)),
            jnp.broadcast_to(g_c[:, None, None], (N_RET, DK_RET, DV_RET)))


def _outproj_kernel(on_ref, or_ref, x_ref, gt_ref, sc_ref, sh_ref, gn_ref, g2_ref, w_ref, x1_ref, h2_ref):
    a = _rms(on_ref[...], gn_ref[...])
    mix = _dot(a.astype(BF16), w_ref[0:D_NSA, :]) + _dot(or_ref[...].astype(BF16), w_ref[D_NSA:, :])
    x1 = x_ref[...] + gt_ref[...] * mix
    x1_ref[...] = x1
    h2_ref[...] = (_rms(x1, g2_ref[...]) * (1.0 + sc_ref[...]) + sh_ref[...]).astype(h2_ref.dtype)


def _outproj_call(o_nsa, o_ret, x2d, gt3, sc3, sh3, mod_idx, g_nsa, g2, w_out, tm, h_dtype):
    m = x2d.shape[0]
    mrows = gt3.shape[1]
    mod_spec = pl.BlockSpec((None, mrows, D_MODEL), lambda i: (mod_idx(i), 0, 0))
    return pl.pallas_call(
        _outproj_kernel,
        grid=(m // tm,),
        in_specs=[pl.BlockSpec((tm, D_NSA), lambda i: (i, 0)),
                  pl.BlockSpec((tm, D_RET), lambda i: (i, 0)),
                  pl.BlockSpec((tm, D_MODEL), lambda i: (i, 0)),
                  mod_spec, mod_spec, mod_spec,
                  pl.BlockSpec((1, D_NSA), lambda i: (0, 0)),
                  pl.BlockSpec((1, D_MODEL), lambda i: (0, 0)),
                  pl.BlockSpec((D_NSA + D_RET, D_MODEL), lambda i: (0, 0))],
        out_specs=[pl.BlockSpec((tm, D_MODEL), lambda i: (i, 0)),
                   pl.BlockSpec((tm, D_MODEL), lambda i: (i, 0))],
        out_shape=[jax.ShapeDtypeStruct((m, D_MODEL), F32),
                   jax.ShapeDtypeStruct((m, D_MODEL), h_dtype)],
        compiler_params=_cparams("parallel"),
        name="out_proj",
    )(o_nsa, o_ret, x2d, gt3, sc3, sh3, g_nsa, g2, w_out)


FF_TILE = D_FF // 2


def _ffn_up_kernel(h_ref, wa_ref, wb_ref, cwa_ref, cwb_ref, cba_ref, cbb_ref, sta_ref, stb_ref,
                   act_ref, csa_ref, csb_ref, prev_a, prev_b):
    @pl.when(pl.program_id(2) == 0)
    def _():
        prev_a[...] = sta_ref[...]
        prev_b[...] = stb_ref[...]

    h = h_ref[...].astype(BF16)
    tm = h.shape[0]
    row = lax.broadcasted_iota(jnp.int32, (tm, 1), 0)

    def half(w_ref, cw_ref, cb_ref, prev, cs_ref):
        u = _dot(h, w_ref[...])
        p = prev[...]
        back1 = pltpu.roll(u, 1, 0)
        back2 = pltpu.roll(u, 2, 0)
        u1 = jnp.where(row >= 1, back1, p[1:2])
        u2 = jnp.where(row >= 2, back2, jnp.where(row == 1, p[1:2], p[0:1]))
        y = cb_ref[...] + cw_ref[0:1] * u2 + cw_ref[1:2] * u1 + cw_ref[2:3] * u
        tail = back2[0:2]
        prev[...] = tail
        cs_ref[...] = tail
        return y

    a = half(wa_ref, cwa_ref, cba_ref, prev_a, csa_ref)
    b = half(wb_ref, cwb_ref, cbb_ref, prev_b, csb_ref)
    act_ref[...] = (a * jax.nn.sigmoid(a) * b).astype(act_ref.dtype)


def _ffn_up_call(h2, w_up_a, w_up_b, conv_w, conv_b, conv_state, nb, seq, tm, act_dtype):
    nrt = seq // tm
    nt = D_FF // FF_TILE
    return pl.pallas_call(
        _ffn_up_kernel,
        grid=(nt, nb, nrt),
        in_specs=[pl.BlockSpec((tm, D_MODEL), lambda j, b, i: (b * nrt + i, 0)),
                  pl.BlockSpec((D_MODEL, FF_TILE), lambda j, b, i: (0, j)),
                  pl.BlockSpec((D_MODEL, FF_TILE), lambda j, b, i: (0, j)),
                  pl.BlockSpec((3, FF_TILE), lambda j, b, i: (0, j)),
                  pl.BlockSpec((3, FF_TILE), lambda j, b, i: (0, nt + j)),
                  pl.BlockSpec((1, FF_TILE), lambda j, b, i: (0, j)),
                  pl.BlockSpec((1, FF_TILE), lambda j, b, i: (0, nt + j)),
                  pl.BlockSpec((None, 2, FF_TILE), lambda j, b, i: (b, 0, j)),
                  pl.BlockSpec((None, 2, FF_TILE), lambda j, b, i: (b, 0, nt + j))],
        out_specs=[pl.BlockSpec((tm, FF_TILE), lambda j, b, i: (b * nrt + i, j)),
                   pl.BlockSpec((None, 2, FF_TILE), lambda j, b, i: (b, 0, j)),
                   pl.BlockSpec((None, 2, FF_TILE), lambda j, b, i: (b, 0, j))],
        out_shape=[jax.ShapeDtypeStruct((nb * seq, D_FF), act_dtype),
                   jax.ShapeDtypeStruct((nb, 2, D_FF), F32),
                   jax.ShapeDtypeStruct((nb, 2, D_FF), F32)],
        scratch_shapes=[pltpu.VMEM((2, FF_TILE), F32), pltpu.VMEM((2, FF_TILE), F32)],
        compiler_params=_cparams("parallel", "parallel", "arbitrary"),
        name="ffn_up_conv",
    )(h2, w_up_a, w_up_b, conv_w, conv_w, conv_b, conv_b, conv_state, conv_state)


def _ffn_down_kernel(a_ref, x1_ref, gt_ref, w_ref, gf_ref, y_ref, *, final_norm):
    x2 = x1_ref[...] + gt_ref[...] * _dot(a_ref[...].astype(BF16), w_ref[...])
    y_ref[...] = _rms(x2, gf_ref[...]) if final_norm else x2


def _ffn_down_call(act, x1, gt3, mod_idx, w_down, g_final, tm, final_norm):
    m = x1.shape[0]
    mrows = gt3.shape[1]
    return pl.pallas_call(
        functools.partial(_ffn_down_kernel, final_norm=final_norm),
        grid=(m // tm,),
        in_specs=[pl.BlockSpec((tm, D_FF), lambda i: (i, 0)),
                  pl.BlockSpec((tm, D_MODEL), lambda i: (i, 0)),
                  pl.BlockSpec((None, mrows, D_MODEL), lambda i: (mod_idx(i), 0, 0)),
                  pl.BlockSpec((D_FF, D_MODEL), lambda i: (0, 0)),
                  pl.BlockSpec((1, D_MODEL), lambda i: (0, 0))],
        out_specs=pl.BlockSpec((tm, D_MODEL), lambda i: (i, 0)),
        out_shape=jax.ShapeDtypeStruct((m, D_MODEL), F32),
        compiler_params=_cparams("parallel"),
        name="ffn_down",
    )(act, x1, gt3, w_down, g_final)


def _relayout_w_in(w_in):
    q_n, kv_n, gate_n, q_r, k_r, v_r, g_r = jnp.split(w_in, [512, 1280, 1304, 1560, 1816, 2328], axis=1)
    gate = gate_n.reshape(D_MODEL, 3, N_KV, REP)
    gate = jnp.transpose(gate, (0, 2, 1, 3)).reshape(D_MODEL, N_KV, 3 * REP)
    gate = jnp.pad(gate, ((0, 0), (0, 0), (0, LANES - 3 * REP))).reshape(D_MODEL, N_KV * LANES)
    return jnp.concatenate([q_n, kv_n, q_r, k_r, v_r, g_r, gate], axis=1).astype(BF16)


def _relayout_cmp(w1, pos):
    w1r = w1.reshape(2, CMP_STRIDE, HD, HD)
    w = jnp.einsum('atdn,gh->tgdahn', w1r, jnp.eye(N_KV, dtype=w1.dtype)).reshape(CMP_STRIDE * N_KV * HD, 2 * N_KV * HD)
    posr = pos.reshape(2, CMP_STRIDE, 1, HD)
    prow = jnp.broadcast_to(posr, (2, CMP_STRIDE, N_KV, HD)).reshape(2, CMP_STRIDE * N_KV * HD)
    prow = jnp.pad(prow, ((0, 14), (0, 0)))
    return w.astype(BF16), prow.astype(BF16)


def _block_diag4(w2k, w2v):
    z = jnp.zeros((HD, HD), w2k.dtype)
    rows = [[w2k, z, z, z], [z, w2k, z, z], [z, z, w2v, z], [z, z, z, w2v]]
    return jnp.block(rows).astype(BF16)


def _rope_tables(pos):
    half = HD // 2
    inv = ROPE_THETA ** (-jnp.arange(half, dtype=F32) / half)
    ang = pos.astype(F32)[:, None] * inv[None, :]
    cos, sin = jnp.cos(ang), jnp.sin(ang)
    return jnp.tile(cos, (1, 4)), jnp.tile(jnp.concatenate([-sin, sin], axis=1), (1, 2))


def _importance_matrix(nc, nsp):
    n = np.arange(nc)[:, None]
    d = n - 4 * np.arange(nsp)[None, :]
    m = ((d >= 0) & (d <= 3)).astype(np.float32) + ((d >= -1) & (d <= 2)).astype(np.float32)
    return jnp.asarray(m, dtype=BF16)


def _expand_matrix_sample(past, nsp):
    row = np.arange(past)
    tok_blk = 2 * (row // PAGE) + (row % 8) // 4
    key = tok_blk.reshape(past // KV_TILE, 1, KV_TILE)
    blk = np.arange(nsp).reshape(1, nsp, 1)
    return jnp.asarray((blk == key).astype(np.float32), dtype=BF16)


def _round_up(x, m):
    return (x + m - 1) // m * m


def kernel(x_prompt, x_sample, c_prompt, c_sample, cache_nsa_kv, cache_win_kv, state_ret, state_conv, page_table,
           w_ada, b_ada, g_norm1, w_in, cmp_pos_k, cmp_w1_k, cmp_w2_k, cmp_pos_v, cmp_w1_v, cmp_w2_v,
           g_nsa_out, g_ret_out, w_out, g_norm2, w_up, conv_w, conv_b, w_down, g_final):
    nb_p, seq, _ = x_prompt.shape
    nb_s, ts, _ = x_sample.shape
    depth = w_ada.shape[0]
    n_pages = page_table.shape[1]
    past = n_pages * PAGE
    wb = cache_win_kv.shape[2]
    m_p, m_s = nb_p * seq, nb_s * ts
    tm_p = 512

    xp = x_prompt.reshape(m_p, D_MODEL)
    xs = x_sample.reshape(m_s, D_MODEL)
    n_c = nb_p + nb_s
    c_all = jnp.pad(jnp.concatenate([c_prompt, c_sample], axis=0), ((0, _round_up(n_c, 8) - n_c), (0, 0)))

    cos_p, sin_p = _rope_tables(jnp.arange(seq, dtype=jnp.int32))
    cos_s, sin_s = _rope_tables(jnp.tile(past + jnp.arange(ts, dtype=jnp.int32), nb_s))
    tabs_p = _ret_tables(min(RET_CHUNK, seq))
    tabs_s = _ret_tables(ts)
    nc_p, nc_s = seq // CMP_STRIDE, past // CMP_STRIDE
    nsp_p = _round_up(seq // SEL_BLOCK, LANES)
    nsp_s = _round_up(past // SEL_BLOCK + 1, LANES)
    mmat_p = _importance_matrix(nc_p, nsp_p)
    blk_onehot_p = jnp.tile(jax.nn.one_hot(jnp.arange(seq) // SEL_BLOCK, LANES, dtype=BF16), (nb_p, 1))[None]
    mmat_s, emat_s = _importance_matrix(nc_s, nsp_s), _expand_matrix_sample(past, nsp_s)

    tiles_per_batch = seq // tm_p
    idx_p = lambda i: i // tiles_per_batch
    tab_p = lambda i: i % tiles_per_batch
    idx_s = lambda i: 0

    outs = {k: [] for k in ('kv_p', 'kv_s', 'win_p', 'win_s', 'ret_p', 'ret_s', 'conv_p', 'conv_s')}
    for l in range(depth):
        mod = _mod_call(c_all, w_ada[l], b_ada[l])
        mods_p = [a.reshape(nb_p, 1, D_MODEL) for a in jnp.split(mod[:nb_p], 6, axis=1)]
        mods_s = [jnp.repeat(a, ts, axis=0).reshape(1, m_s, D_MODEL) for a in jnp.split(mod[nb_p:n_c], 6, axis=1)]
        w_in_re = _relayout_w_in(w_in[l])
        wk, pos_k = _relayout_cmp(cmp_w1_k[l], cmp_pos_k[l])
        wv, pos_v = _relayout_cmp(cmp_w1_v[l], cmp_pos_v[l])
        pos_kv = jnp.stack([pos_k, pos_v])
        w2bd = _block_diag4(cmp_w2_k[l], cmp_w2_v[l])
        g1 = g_norm1[l].reshape(1, -1)
        g2 = g_norm2[l].reshape(1, -1)
        g_nsa = g_nsa_out[l].reshape(1, -1)
        g_ret = g_ret_out[l].reshape(1, -1)
        w_out_bf = w_out[l].astype(BF16)
        w_up_a = w_up[l][:, :D_FF].astype(BF16)
        w_up_b = w_up[l][:, D_FF:].astype(BF16)
        w_down_bf = w_down[l].astype(BF16)

        sh1, sc1, gt1, sh2, sc2, gt2 = mods_p
        qh_p, kvn_p, kvw_p, kvh_p, ret_p, gat_p = _inproj_call(
            xp, sc1, sh1, idx_p, g1, w_in_re, cos_p, sin_p, tab_p, tm_p, BF16)
        sh1s, sc1s, gt1s, sh2s, sc2s, gt2s = mods_s
        qh_s, kvn_s, kvw_s, kvh_s, ret_s, gat_s = _inproj_call(
            xs, sc1s, sh1s, idx_s, g1, w_in_re, cos_s, sin_s, idx_s, m_s, F32)

        ab_p = _cmp_prompt_call(kvn_p, wk, wv, pos_kv).reshape(nb_p, nc_p, 512)
        ab_s, ks_s = _cmp_sample_call(cache_nsa_kv[:, l], page_table, wk, wv, pos_kv)
        cmp_p = _cmp_fin_call(ab_p, w2bd)
        cmp_s = _cmp_fin_call(ab_s, w2bd)

        ones_col = jnp.concatenate([jnp.ones((N_KV, m_p, 1), BF16), jnp.zeros((N_KV, m_p, LANES - HD - 1), BF16)], axis=-1)
        kaug_p = jnp.concatenate([jnp.broadcast_to(blk_onehot_p, (N_KV, m_p, LANES)), kvh_p[0:2]], axis=-1)
        vaug_p = jnp.concatenate([kvh_p[2:4], ones_col], axis=-1)
        vwaug_p = jnp.concatenate([kvh_p[6:8], ones_col], axis=-1)
        o_nsa_p = _nsa_prompt_call(qh_p, gat_p, cmp_p, kaug_p, vaug_p, kvh_p[4:6], vwaug_p, mmat_p, nb_p, seq, 0)
        win_cache = cache_win_kv[:, l].reshape(nb_s, wb, 256)
        o_nsa_s = _nsa_sample_call(qh_s, gat_s, cmp_s, ks_s, kvh_s, win_cache, mmat_s, emat_s, nb_s, ts, past, 0)

        s0_p = jnp.zeros((nb_p, N_RET, DK_RET, DV_RET), F32)
        o_ret_p, s_new_p = _ret_call(ret_p, s0_p, tabs_p, g_ret, nb_p, seq, min(RET_CHUNK, seq))
        o_ret_s, s_new_s = _ret_call(ret_s, state_ret[:, l], tabs_s, g_ret, nb_s, ts, ts)

        x1_p, h2_p = _outproj_call(o_nsa_p, o_ret_p, xp, gt1, sc2, sh2, idx_p, g_nsa, g2, w_out_bf, tm_p, BF16)
        x1_s, h2_s = _outproj_call(o_nsa_s, o_ret_s, xs, gt1s, sc2s, sh2s, idx_s, g_nsa, g2, w_out_bf, m_s, F32)

        conv0_p = jnp.zeros((nb_p, 2, 2 * D_FF), F32)
        act_p, csa_p, csb_p = _ffn_up_call(h2_p, w_up_a, w_up_b, conv_w[l], conv_b[l].reshape(1, -1), conv0_p,
                                           nb_p, seq, tm_p, BF16)
        act_s, csa_s, csb_s = _ffn_up_call(h2_s, w_up_a, w_up_b, conv_w[l], conv_b[l].reshape(1, -1),
                                           state_conv[:, l], nb_s, ts, ts, F32)
        last = l == depth - 1
        gf = g_final.reshape(1, -1)
        xp = _ffn_down_call(act_p, x1_p, gt2, idx_p, w_down_bf, gf, tm_p, last)
        xs = _ffn_down_call(act_s, x1_s, gt2s, idx_s, w_down_bf, gf, m_s, last)

        outs['kv_p'].append(kvn_p.reshape(nb_p, seq, 4, N_KV, HD))
        outs['kv_s'].append(kvn_s.reshape(nb_s, ts, 4, N_KV, HD))
        win_rows_p = kvw_p.reshape(nb_p, seq, 2, N_KV, HD)
        outs['win_p'].append(win_rows_p[:, seq - min(WINDOW, seq):])
        win_all = jnp.concatenate([cache_win_kv[:, l], kvw_s.reshape(nb_s, ts, 2, N_KV, HD)], axis=1)
        outs['win_s'].append(win_all[:, win_all.shape[1] - min(WINDOW, past + ts):])
        outs['ret_p'].append(s_new_p)
        outs['ret_s'].append(s_new_s)
        outs['conv_p'].append(jnp.concatenate([csa_p, csb_p], axis=-1))
        outs['conv_s'].append(jnp.concatenate([csa_s, csb_s], axis=-1))

    st = lambda k: jnp.stack(outs[k], axis=1)
    return (xp.reshape(nb_p, seq, D_MODEL), xs.reshape(nb_s, ts, D_MODEL),
            st('kv_p'), st('kv_s'), st('win_p'), st('win_s'), st('ret_p'), st('ret_s'), st('conv_p'), st('conv_s'))
```

```python
import functools

import numpy as np
import jax
import jax.numpy as jnp
from jax import lax
from jax.experimental import pallas as pl
from jax.experimental.pallas import tpu as pltpu

F32 = jnp.float32
BF16 = jnp.bfloat16

D_MODEL = 1024
PAGE = 128
HD = 64
N_KV = 2
REP = 4
D_NSA = 512
CMP_STRIDE = 16
SEL_BLOCK = 64
N_SEL = 16
WINDOW = 512
Q_TILE = 512
KV_TILE = 512
FORCE_BONUS = 100.0
N_RET = 4
DK_RET = 64
DV_RET = 128
D_RET = 512
RET_CHUNK = 128
D_FF = 2816
ROPE_THETA = 10000.0
EPS = 1e-6
NEG_INF = -1e30
LANES = 128
W_IN_COLS = 3072
VMEM_LIMIT = 56 * 1024 * 1024


def _cparams(*sem):
    return pltpu.CompilerParams(dimension_semantics=sem, vmem_limit_bytes=VMEM_LIMIT)


def _dot(a, b):
    return jnp.dot(a, b, preferred_element_type=F32)


def _dot_nt(a, b):
    return lax.dot_general(a, b, (((1,), (1,)), ((), ())), preferred_element_type=F32)


def _dot_tn(a, b):
    return lax.dot_general(a, b, (((0,), (0,)), ((), ())), preferred_element_type=F32)


def _dot_split3(p, m_bf16):
    hi = p.astype(BF16)
    r1 = p - hi.astype(F32)
    mid = r1.astype(BF16)
    lo = (r1 - mid.astype(F32)).astype(BF16)
    return _dot(hi, m_bf16) + _dot(mid, m_bf16) + _dot(lo, m_bf16)


def _rms(x, g):
    return x * lax.rsqrt(jnp.mean(x * x, axis=-1, keepdims=True) + EPS) * g


def _masked_softmax(s, mask):
    s = jnp.where(mask, s, NEG_INF)
    m = jnp.max(s, axis=-1, keepdims=True)
    e = jnp.where(mask, jnp.exp(s - m), 0.0)
    return e / jnp.maximum(jnp.sum(e, axis=-1, keepdims=True), 1e-30)


def _topk_mask_t(score_t, n_sel):
    nb = score_t.shape[0]
    blk = lax.broadcasted_iota(jnp.int32, score_t.shape, 0)

    def body(_, work):
        m = jnp.max(work, axis=0, keepdims=True)
        idx = jnp.min(jnp.where(work == m, blk, nb), axis=0, keepdims=True)
        return jnp.where(blk == idx, -jnp.inf, work)

    work = lax.fori_loop(0, n_sel, body, score_t)
    return jnp.where(work == -jnp.inf, jnp.where(score_t == -jnp.inf, 0.0, 1.0), 0.0)


def _selection(psum, mmat, qpos, ns, n_sel):
    imp = _dot_split3(psum, mmat)
    blk = lax.broadcasted_iota(jnp.int32, (1, imp.shape[1]), 1)
    qblk = qpos >> 6
    valid = (blk * SEL_BLOCK <= qpos) & (blk < ns)
    forced = (blk == 0) | (blk == qblk) | (blk == qblk - 1)
    score = jnp.where(valid, imp + jnp.where(forced, FORCE_BONUS, 0.0), NEG_INF)
    score = jnp.where(blk < ns, score, -jnp.inf)
    sel = _topk_mask_t(score.T, n_sel).T
    return jnp.where(valid, sel, 0.0)


def _mod_kernel(c_ref, w_ref, b_ref, o_ref):
    c = c_ref[...]
    o_ref[...] = _dot(c * jax.nn.sigmoid(c), w_ref[...]) + b_ref[...]


def _mod_call(c_all, w_ada, b_ada):
    n = c_all.shape[0]
    tn = 1536
    return pl.pallas_call(
        _mod_kernel,
        grid=(w_ada.shape[1] // tn,),
        in_specs=[pl.BlockSpec((n, D_MODEL), lambda j: (0, 0)),
                  pl.BlockSpec((D_MODEL, tn), lambda j: (0, j)),
                  pl.BlockSpec((1, tn), lambda j: (0, j))],
        out_specs=pl.BlockSpec((n, tn), lambda j: (0, j)),
        out_shape=jax.ShapeDtypeStruct((n, w_ada.shape[1]), F32),
        compiler_params=_cparams("arbitrary"),
        name="adaln_mod",
    )(c_all, w_ada, b_ada.reshape(1, -1))


def _inproj_kernel(x_ref, sc_ref, sh_ref, g1_ref, w_ref, cos_ref, sin_ref,
                   qh_ref, kvn_ref, kvw_ref, kvh_ref, ret_ref, gat_ref, *maybe_kvc_ref, feature_major):
    h = (_rms(x_ref[...], g1_ref[...]) * (1.0 + sc_ref[...]) + sh_ref[...]).astype(BF16)
    cos = cos_ref[...]
    sin = sin_ref[...]
    lane = lax.broadcasted_iota(jnp.int32, (1, LANES), 1)
    first_half = (lane & (HD - 1)) < HD // 2

    def seg(c0, n):
        return _dot(h, w_ref[:, c0:c0 + n])

    def rope(a):
        sw = jnp.where(first_half, pltpu.roll(a, LANES - HD // 2, 1), pltpu.roll(a, HD // 2, 1))
        return a * cos + sw * sin

    for j in range(4):
        q = rope(seg(LANES * j, LANES)) * (HD ** -0.5)
        qh_ref[2 * j] = q[:, :HD].astype(qh_ref.dtype)
        qh_ref[2 * j + 1] = q[:, HD:].astype(qh_ref.dtype)
    for s in range(6):
        a = seg(512 + LANES * s, LANES)
        if s % 2 == 0:
            a = rope(a)
        out_ref, s_out = (kvn_ref, s) if s < 4 else (kvw_ref, s - 4)
        if feature_major:
            out_ref[LANES * s_out:LANES * (s_out + 1), :] = a.T
            if s < 2:
                maybe_kvc_ref[0][:, LANES * s:LANES * (s + 1)] = a
        else:
            out_ref[:, LANES * s_out:LANES * (s_out + 1)] = a
        if s >= 2:
            kvh_ref[2 * (s - 2)] = a[:, :HD].astype(kvh_ref.dtype)
            kvh_ref[2 * (s - 2) + 1] = a[:, HD:].astype(kvh_ref.dtype)
    for j in range(2):
        ret_ref[:, LANES * j:LANES * (j + 1)] = rope(seg(1280 + LANES * j, LANES)) * (DK_RET ** -0.5)
        ret_ref[:, 256 + LANES * j:256 + LANES * (j + 1)] = rope(seg(1536 + LANES * j, LANES))
    ret_ref[:, 512:1024] = seg(1792, 512)
    ret_ref[:, 1024:1536] = seg(2304, 512)
    gat_ref[...] = seg(2816, 256)


def _inproj_call(x2d, sc3, sh3, mod_idx, g1, w_in_re, cos, sin, tab_idx, tm, head_dtype, tiles_per_batch=None):
    m = x2d.shape[0]
    mrows = sc3.shape[1]
    feature_major = tiles_per_batch is not None
    if feature_major:
        nb, seq = m // (tiles_per_batch * tm), tiles_per_batch * tm
        t_idx = lambda i: (i // tiles_per_batch, 0, i % tiles_per_batch)
        kv_specs = [pl.BlockSpec((None, 512, tm), t_idx), pl.BlockSpec((None, 256, tm), t_idx)]
        kv_shapes = [jax.ShapeDtypeStruct((nb, 512, seq), F32), jax.ShapeDtypeStruct((nb, 256, seq), F32)]
        extra_specs = [pl.BlockSpec((tm, 256), lambda i: (i, 0))]
        extra_shapes = [jax.ShapeDtypeStruct((m, 256), F32)]
    else:
        kv_specs = [pl.BlockSpec((tm, 512), lambda i: (i, 0)), pl.BlockSpec((tm, 256), lambda i: (i, 0))]
        kv_shapes = [jax.ShapeDtypeStruct((m, 512), F32), jax.ShapeDtypeStruct((m, 256), F32)]
        extra_specs, extra_shapes = [], []
    return pl.pallas_call(
        functools.partial(_inproj_kernel, feature_major=feature_major),
        grid=(m // tm,),
        in_specs=[pl.BlockSpec((tm, D_MODEL), lambda i: (i, 0)),
                  pl.BlockSpec((None, mrows, D_MODEL), lambda i: (mod_idx(i), 0, 0)),
                  pl.BlockSpec((None, mrows, D_MODEL), lambda i: (mod_idx(i), 0, 0)),
                  pl.BlockSpec((1, D_MODEL), lambda i: (0, 0)),
                  pl.BlockSpec((D_MODEL, W_IN_COLS), lambda i: (0, 0)),
                  pl.BlockSpec((tm, LANES), lambda i: (tab_idx(i), 0)),
                  pl.BlockSpec((tm, LANES), lambda i: (tab_idx(i), 0))],
        out_specs=[pl.BlockSpec((8, tm, HD), lambda i: (0, i, 0))] + kv_specs + [
                   pl.BlockSpec((8, tm, HD), lambda i: (0, i, 0)),
                   pl.BlockSpec((tm, 1536), lambda i: (i, 0)),
                   pl.BlockSpec((tm, 256), lambda i: (i, 0))] + extra_specs,
        out_shape=[jax.ShapeDtypeStruct((8, m, HD), head_dtype)] + kv_shapes + [
                   jax.ShapeDtypeStruct((8, m, HD), head_dtype),
                   jax.ShapeDtypeStruct((m, 1536), F32),
                   jax.ShapeDtypeStruct((m, 256), F32)] + extra_shapes,
        compiler_params=_cparams("parallel"),
        name="in_proj",
    )(x2d, sc3, sh3, g1, w_in_re, cos, sin)


def _cmp_partial(slab, wk_ref, wv_ref, pos_ref, ab_ref):
    for kind, w_ref in ((0, wk_ref), (1, wv_ref)):
        xcat = jnp.concatenate([slab(t, kind) for t in range(CMP_STRIDE)], axis=1).astype(BF16)
        r = _dot(xcat, w_ref[...])
        bias = _dot(pos_ref[kind], w_ref[...])
        ab_ref[:, LANES * kind:LANES * (kind + 1)] = r[:, :LANES] + bias[0:1, :LANES]
        ab_ref[:, 256 + LANES * kind:256 + LANES * (kind + 1)] = r[:, LANES:] + bias[1:2, LANES:]


def _cmp_prompt_kernel(x_ref, wk_ref, wv_ref, pos_ref, ab_ref):
    _cmp_partial(lambda t, kind: x_ref[:, 256 * t + LANES * kind:256 * t + LANES * (kind + 1)],
                 wk_ref, wv_ref, pos_ref, ab_ref)


def _cmp_prompt_call(kvc, wk, wv, pos):
    rows = kvc.shape[0] // CMP_STRIDE
    xv = kvc.reshape(rows, CMP_STRIDE * 256)
    tr = min(128, rows)
    return pl.pallas_call(
        _cmp_prompt_kernel,
        grid=(rows // tr,),
        in_specs=[pl.BlockSpec((tr, CMP_STRIDE * 256), lambda i: (i, 0)),
                  pl.BlockSpec(wk.shape, lambda i: (0, 0)),
                  pl.BlockSpec(wv.shape, lambda i: (0, 0)),
                  pl.BlockSpec(pos.shape, lambda i: (0, 0, 0))],
        out_specs=pl.BlockSpec((tr, 512), lambda i: (i, 0)),
        out_shape=jax.ShapeDtypeStruct((rows, 512), F32),
        compiler_params=_cparams("parallel"),
        name="cmp_partial_prompt",
    )(xv, wk, wv, pos)


PAGES_PER_STEP = 16


def _cmp_sample_kernel(pt_ref, *refs):
    pages = refs[:PAGES_PER_STEP]
    wk_ref, wv_ref, pos_ref, ab_ref, ks_ref, x_sc = refs[PAGES_PER_STEP:]
    for p, page in enumerate(pages):
        for kind in range(2):
            x_sc[kind, PAGE * p:PAGE * (p + 1), :] = page[LANES * kind:LANES * (kind + 1), :].T
        ks_ref[:, PAGE * p:PAGE * (p + 1)] = page[256:512, :].astype(BF16)
    rows = PAGES_PER_STEP * (PAGE // CMP_STRIDE)
    _cmp_partial(lambda t, kind: x_sc[kind, pl.ds(t, rows, stride=CMP_STRIDE), :],
                 wk_ref, wv_ref, pos_ref, ab_ref)


def _cmp_sample_call(cache_t, page_table, wk, wv, pos):
    nb, n_pages = page_table.shape
    chunks = PAGE // CMP_STRIDE
    steps = n_pages // PAGES_PER_STEP

    def page_spec(k):
        return pl.BlockSpec((None, 512, PAGE), lambda b, j, pt: (pt[b, j * PAGES_PER_STEP + k], 0, 0))

    grid_spec = pltpu.PrefetchScalarGridSpec(
        num_scalar_prefetch=1,
        grid=(nb, steps),
        in_specs=[page_spec(k) for k in range(PAGES_PER_STEP)] + [
            pl.BlockSpec(wk.shape, lambda b, j, pt: (0, 0)),
            pl.BlockSpec(wv.shape, lambda b, j, pt: (0, 0)),
            pl.BlockSpec(pos.shape, lambda b, j, pt: (0, 0, 0))],
        out_specs=[pl.BlockSpec((None, PAGES_PER_STEP * chunks, 512), lambda b, j, pt: (b, j, 0)),
                   pl.BlockSpec((None, 256, PAGES_PER_STEP * PAGE), lambda b, j, pt: (b, 0, j))],
        scratch_shapes=[pltpu.VMEM((2, PAGES_PER_STEP * PAGE, LANES), F32)],
    )
    return pl.pallas_call(
        _cmp_sample_kernel,
        grid_spec=grid_spec,
        out_shape=[jax.ShapeDtypeStruct((nb, n_pages * chunks, 512), F32),
                   jax.ShapeDtypeStruct((nb, 256, n_pages * PAGE), BF16)],
        compiler_params=_cparams("parallel", "arbitrary"),
        name="cmp_partial_sample",
    )(page_table, *([cache_t] * PAGES_PER_STEP), wk, wv, pos)


def _cmp_fin_kernel(ab_ref, w2_ref, o_ref):
    nc = ab_ref.shape[0]
    a = ab_ref[:, :256]
    b_next = pltpu.roll(ab_ref[:, 256:], nc - 1, 0)
    hid = jax.nn.gelu(a + b_next)
    out = _dot(hid.astype(BF16), w2_ref[...])
    row = lax.broadcasted_iota(jnp.int32, (nc, 1), 0)
    out = jnp.where(row < nc - 1, out, 0.0)
    for s in range(4):
        o_ref[s] = out[:, HD * s:HD * (s + 1)]


def _cmp_fin_call(ab, w2bd):
    nb, nc, _ = ab.shape
    return pl.pallas_call(
        _cmp_fin_kernel,
        grid=(nb,),
        in_specs=[pl.BlockSpec((None, nc, 512), lambda b: (b, 0, 0)),
                  pl.BlockSpec((256, 256), lambda b: (0, 0))],
        out_specs=pl.BlockSpec((None, 4, nc, HD), lambda b: (b, 0, 0, 0)),
        out_shape=jax.ShapeDtypeStruct((nb, 4, nc, HD), F32),
        compiler_params=_cparams("parallel"),
        name="cmp_finish",
    )(ab, w2bd)


def _nsa_prompt_kernel(q_ref, gat_ref, kc_ref, vc_ref, ka_ref, va_ref, kw_ref, vw_ref, mmat_ref,
                       o_ref, qa_sc, s_sc, mcur_sc, m_sc, acc_sc, *, ns, n_sel):
    i = pl.program_id(2)
    start = i * Q_TILE
    qpos = start + lax.broadcasted_iota(jnp.int32, (Q_TILE, 1), 0)
    nc = kc_ref.shape[0]
    kc = kc_ref[...].astype(BF16)
    vc = vc_ref[...].astype(BF16)
    head_rows = [slice(r * Q_TILE, (r + 1) * Q_TILE) for r in range(REP)]

    c_end = lax.broadcasted_iota(jnp.int32, (1, nc), 1) * CMP_STRIDE + (2 * CMP_STRIDE - 1)
    cbias = jnp.where(c_end <= qpos, 0.0, NEG_INF)
    any_valid = qpos >= 2 * CMP_STRIDE - 1
    psum = jnp.zeros((Q_TILE, nc), F32)
    o_cmp = []
    for r in range(REP):
        s = _dot_nt(q_ref[r], kc) + cbias
        e = jnp.exp(s - jnp.max(s, axis=-1, keepdims=True))
        norm = jnp.where(any_valid, 1.0 / jnp.maximum(jnp.sum(e, axis=-1, keepdims=True), 1e-30), 0.0)
        p = e * norm
        psum = psum + p
        o_cmp.append(_dot(p.astype(BF16), vc))

    sel = _selection(psum, mmat_ref[...], qpos, ns, n_sel)
    selneg = jnp.where(sel > 0.5, 0.0, NEG_INF).astype(BF16)
    for r in range(REP):
        qa_sc[head_rows[r], :] = jnp.concatenate([selneg, q_ref[r]], axis=1)

    n_full = start // KV_TILE

    def scores(j):
        k0 = pl.multiple_of(j * KV_TILE, KV_TILE)
        return _dot_nt(qa_sc[...], ka_ref[pl.ds(k0, KV_TILE), :])

    def consume(slot, t):
        v0 = pl.multiple_of(t * KV_TILE, KV_TILE)
        v_t = va_ref[pl.ds(v0, KV_TILE), :]
        for r in range(REP):
            rows = head_rows[r]
            m_old = m_sc[rows]
            m_new = jnp.maximum(m_old, mcur_sc[slot, rows])
            p = jnp.exp(s_sc[slot, rows] - m_new).astype(BF16)
            acc_sc[rows] = jnp.exp(m_old - m_new) * acc_sc[rows] + _dot(p, v_t)
            m_sc[rows] = m_new

    kpos = n_full * KV_TILE + lax.broadcasted_iota(jnp.int32, (1, KV_TILE), 1)
    causal_bias = jnp.where(kpos <= qpos, 0.0, NEG_INF)
    s_diag = scores(n_full)
    for r in range(REP):
        s_r = s_diag[head_rows[r]] + causal_bias
        s_sc[0, head_rows[r]] = s_r
        mcur_sc[0, head_rows[r]] = jnp.max(s_r, axis=-1, keepdims=True)
    m_sc[...] = jnp.full(m_sc.shape, NEG_INF, F32)
    acc_sc[...] = jnp.zeros(acc_sc.shape, F32)

    def body(j, carry):
        consume(j & 1, jnp.where(j == 0, n_full, j - 1))
        s_new = scores(j)
        slot_new = (j + 1) & 1
        s_sc[slot_new] = s_new
        mcur_sc[slot_new] = jnp.max(s_new, axis=-1, keepdims=True)
        return carry

    lax.fori_loop(0, n_full, body, 0)
    consume(n_full & 1, jnp.maximum(n_full - 1, 0))

    base = pl.multiple_of(jnp.maximum(start - WINDOW, 0), Q_TILE)
    kw = kw_ref[pl.ds(base, WINDOW + Q_TILE), :]
    vw = vw_ref[pl.ds(base, WINDOW + Q_TILE), :]
    dpos = qpos - (base + lax.broadcasted_iota(jnp.int32, (1, WINDOW + Q_TILE), 1))
    wbias = jnp.where((dpos >= 0) & (dpos < WINDOW), 0.0, NEG_INF)

    gate = jax.nn.sigmoid(gat_ref[...])
    for r in range(REP):
        rows = head_rows[r]
        s = _dot_nt(q_ref[r], kw) + wbias
        e = jnp.exp(s - jnp.max(s, axis=-1, keepdims=True)).astype(BF16)
        ow = _dot(e, vw)
        o_win = ow[:, :HD] / jnp.maximum(ow[:, HD:HD + 1], 1e-30)
        acc = acc_sc[rows]
        o_slc = acc[:, :HD] / jnp.maximum(acc[:, HD:HD + 1], 1e-30)
        o_ref[:, HD * r:HD * (r + 1)] = (gate[:, r:r + 1] * o_cmp[r] + gate[:, REP + r:REP + r + 1] * o_slc
                                        + gate[:, 2 * REP + r:2 * REP + r + 1] * o_win)


def _nsa_prompt_call(qh, gat, cmp, kaug, vaug, kwin, vwaug, mmat, nb, seq, cmp_off):
    nq = seq // Q_TILE
    nc = cmp.shape[2]
    ns = seq // SEL_BLOCK
    assert ns <= LANES and mmat.shape[1] == LANES
    kern = functools.partial(_nsa_prompt_kernel, ns=ns, n_sel=min(N_SEL, ns))

    def kv_spec(width):
        return pl.BlockSpec((None, seq, width), lambda b, g, i: (g, b, 0))

    return pl.pallas_call(
        kern,
        grid=(nb, N_KV, nq),
        in_specs=[pl.BlockSpec((REP, Q_TILE, HD), lambda b, g, i: (g, b * nq + i, 0)),
                  pl.BlockSpec((Q_TILE, LANES), lambda b, g, i: (b * nq + i, g)),
                  pl.BlockSpec((None, None, nc, HD), lambda b, g, i: (cmp_off + b, g, 0, 0)),
                  pl.BlockSpec((None, None, nc, HD), lambda b, g, i: (cmp_off + b, 2 + g, 0, 0)),
                  kv_spec(LANES + HD), kv_spec(LANES), kv_spec(HD), kv_spec(LANES),
                  pl.BlockSpec(mmat.shape, lambda b, g, i: (0, 0))],
        out_specs=pl.BlockSpec((Q_TILE, REP * HD), lambda b, g, i: (b * nq + i, g)),
        out_shape=jax.ShapeDtypeStruct((nb * seq, D_NSA), F32),
        scratch_shapes=[pltpu.VMEM((REP * Q_TILE, LANES + HD), BF16),
                        pltpu.VMEM((2, REP * Q_TILE, KV_TILE), F32),
                        pltpu.VMEM((2, REP * Q_TILE, 1), F32),
                        pltpu.VMEM((REP * Q_TILE, 1), F32),
                        pltpu.VMEM((REP * Q_TILE, LANES), F32)],
        compiler_params=_cparams("parallel", "parallel", "arbitrary"),
        name="nsa_prompt",
    )(qh, gat, cmp, cmp, kaug, vaug, kwin, vwaug, mmat)


def _nsa_sample_kernel(q_ref, gat_ref, cmp_ref, ks_ref, new_ref, win_ref, mmat_ref, oht_ref, o_ref,
                       *, past, ts, ns, n_sel):
    nc = cmp_ref.shape[1]
    wb = win_ref.shape[1]
    t_idx = lax.broadcasted_iota(jnp.int32, (ts, 1), 0)
    qpos = jnp.concatenate([past + t_idx] * REP, axis=0)
    gate = jax.nn.sigmoid(gat_ref[...])
    pad_rows = jnp.zeros((LANES - ts, HD), F32)

    def padded(slab):
        return jnp.concatenate([new_ref[slab], pad_rows], axis=0).astype(BF16)

    c_end = lax.broadcasted_iota(jnp.int32, (1, nc), 1) * CMP_STRIDE + (2 * CMP_STRIDE - 1)
    qs, o_cmps, psums = [], [], []
    for g in range(N_KV):
        q = jnp.concatenate([q_ref[REP * g + r] for r in range(REP)], axis=0).astype(BF16)
        p = _masked_softmax(_dot_nt(q, cmp_ref[g].astype(BF16)), c_end <= qpos)
        o_cmps.append(_dot(p.astype(BF16), cmp_ref[2 + g].astype(BF16)))
        psum = p[0:ts]
        for r in range(1, REP):
            psum = psum + p[r * ts:(r + 1) * ts]
        qs.append(q)
        psums.append(psum)
    psum_all = jnp.concatenate(psums + [jnp.zeros((LANES - N_KV * ts, nc), F32)], axis=0)
    row = lax.broadcasted_iota(jnp.int32, (LANES, 1), 0)
    sel_all = _selection(psum_all, mmat_ref[...], past + lax.rem(row, ts), ns, n_sel)

    kidx = lax.broadcasted_iota(jnp.int32, (1, LANES), 1)
    kpos_w = past - wb + lax.broadcasted_iota(jnp.int32, (1, wb + LANES), 1)
    dpos_w = qpos - kpos_w
    wmask = (dpos_w >= 0) & (dpos_w < WINDOW) & (kpos_w >= 0)
    for g in range(N_KV):
        q = qs[g]
        sel = jnp.concatenate([sel_all[g * ts:(g + 1) * ts]] * REP, axis=0)
        selneg = jnp.where(sel[:, :LANES] > 0.5, 0.0, NEG_INF).astype(BF16)
        s_old = _dot(q, ks_ref[HD * g:HD * (g + 1), :]) + _dot(selneg, oht_ref[...])
        new_bias = jnp.where((sel[:, ns - 1:ns] > 0.5) & (past + kidx <= qpos), 0.0, NEG_INF)
        s_new = _dot_nt(q, padded(g)) + new_bias
        m = jnp.maximum(jnp.max(s_old, axis=-1, keepdims=True), jnp.max(s_new, axis=-1, keepdims=True))
        e_old = jnp.exp(s_old - m)
        e_new = jnp.exp(s_new - m)
        denom = jnp.sum(e_old, axis=-1, keepdims=True) + jnp.sum(e_new, axis=-1, keepdims=True)
        o_slc = (_dot_nt(e_old.astype(BF16), ks_ref[128 + HD * g:128 + HD * (g + 1), :])
                 + _dot(e_new.astype(BF16), padded(2 + g))) / jnp.maximum(denom, 1e-30)

        s_w = jnp.concatenate([_dot(q, win_ref[HD * g:HD * (g + 1), :].astype(BF16)),
                               _dot_nt(q, padded(4 + g))], axis=1)
        pw = _masked_softmax(s_w, wmask)
        o_win = (_dot_nt(pw[:, :wb].astype(BF16), win_ref[128 + HD * g:128 + HD * (g + 1), :].astype(BF16))
                 + _dot(pw[:, wb:].astype(BF16), padded(6 + g)))

        for r in range(REP):
            rs = slice(r * ts, (r + 1) * ts)
            c = LANES * g + r
            o_ref[:, HD * (REP * g + r):HD * (REP * g + r + 1)] = (
                gate[:, c:c + 1] * o_cmps[g][rs] + gate[:, c + REP:c + REP + 1] * o_slc[rs]
                + gate[:, c + 2 * REP:c + 2 * REP + 1] * o_win[rs])


def _nsa_sample_call(qh, gat, cmp, ks_t, kvh_new, win_t, mmat, oht, nb, ts, past, cmp_off):
    nc = cmp.shape[2]
    ns = past // SEL_BLOCK + 1
    assert ns - 1 <= LANES and N_KV * ts <= LANES
    kern = functools.partial(_nsa_sample_kernel, past=past, ts=ts, ns=ns, n_sel=min(N_SEL, ns))
    return pl.pallas_call(
        kern,
        grid=(nb,),
        in_specs=[pl.BlockSpec((8, ts, HD), lambda b: (0, b, 0)),
                  pl.BlockSpec((ts, 256), lambda b: (b, 0)),
                  pl.BlockSpec((None, 4, nc, HD), lambda b: (cmp_off + b, 0, 0, 0)),
                  pl.BlockSpec((None, 256, past), lambda b: (b, 0, 0)),
                  pl.BlockSpec((8, ts, HD), lambda b: (0, b, 0)),
                  pl.BlockSpec((None, 256, win_t.shape[2]), lambda b: (b, 0, 0)),
                  pl.BlockSpec(mmat.shape, lambda b: (0, 0)),
                  pl.BlockSpec(oht.shape, lambda b: (0, 0))],
        out_specs=pl.BlockSpec((ts, D_NSA), lambda b: (b, 0)),
        out_shape=jax.ShapeDtypeStruct((nb * ts, D_NSA), F32),
        compiler_params=_cparams("parallel"),
        name="nsa_sample",
    )(qh, gat, cmp, ks_t, kvh_new, win_t, mmat, oht)


def _ret_kernel(x_ref, s0_ref, dmat_ref, xi_ref, zeta_ref, gc_ref, gro_ref, o_ref, s_out_ref, s_sc, pad_sc,
                *, rows):
    @pl.when(pl.program_id(1) == 0)
    def _():
        s_sc[...] = s0_ref[...]

    if rows < RET_CHUNK:
        pad_sc[...] = jnp.zeros(pad_sc.shape, F32)
        pad_sc[0:rows, :] = x_ref[...]
        x = pad_sc
    else:
        x = x_ref
    for h in range(N_RET):
        q = x[:, DK_RET * h:DK_RET * (h + 1)].astype(BF16)
        k = x[:, 256 + DK_RET * h:256 + DK_RET * (h + 1)]
        v = x[:, 512 + DV_RET * h:512 + DV_RET * (h + 1)].astype(BF16)
        gr = x[:, 1024 + DV_RET * h:1024 + DV_RET * (h + 1)]
        att = _dot_nt(q, k.astype(BF16)) * dmat_ref[h]
        s_old = s_sc[h]
        o = _dot(att.astype(BF16), v) + _dot(q, s_old.astype(BF16)) * xi_ref[h]
        s_sc[h] = gc_ref[h] * s_old + _dot_tn((k * zeta_ref[h]).astype(BF16), v)
        mu = jnp.mean(o, axis=-1, keepdims=True)
        var = jnp.mean(jnp.square(o - mu), axis=-1, keepdims=True)
        y = (o - mu) * lax.rsqrt(var + EPS) * gro_ref[:, DV_RET * h:DV_RET * (h + 1)]
        res = gr * jax.nn.sigmoid(gr) * y
        o_ref[:, DV_RET * h:DV_RET * (h + 1)] = res[0:rows]
    s_out_ref[...] = s_sc[...]


def _ret_call(ret, s0, tabs, g_ret_out, nb, seq, rows):
    nchunks = seq // rows
    dmat, xi, zeta, gc = tabs
    kern = functools.partial(_ret_kernel, rows=rows)
    full3 = lambda b, c: (0, 0, 0)
    return pl.pallas_call(
        kern,
        grid=(nb, nchunks),
        in_specs=[pl.BlockSpec((rows, 1536), lambda b, c: (b * nchunks + c, 0)),
                  pl.BlockSpec((None, N_RET, DK_RET, DV_RET), lambda b, c: (b, 0, 0, 0)),
                  pl.BlockSpec(dmat.shape, full3), pl.BlockSpec(xi.shape, full3),
                  pl.BlockSpec(zeta.shape, full3), pl.BlockSpec(gc.shape, full3),
                  pl.BlockSpec((1, D_RET), lambda b, c: (0, 0))],
        out_specs=[pl.BlockSpec((rows, D_RET), lambda b, c: (b * nchunks + c, 0)),
                   pl.BlockSpec((None, N_RET, DK_RET, DV_RET), lambda b, c: (b, 0, 0, 0))],
        out_shape=[jax.ShapeDtypeStruct((nb * seq, D_RET), F32),
                   jax.ShapeDtypeStruct((nb, N_RET, DK_RET, DV_RET), F32)],
        scratch_shapes=[pltpu.VMEM((N_RET, DK_RET, DV_RET), F32),
                        pltpu.VMEM((RET_CHUNK, 1536), F32)],
        compiler_params=_cparams("parallel", "arbitrary"),
        name="retention",
    )(ret, s0, dmat, xi, zeta, gc, g_ret_out)


def _ret_tables(chunk):
    c = RET_CHUNK
    log_g = jnp.log1p(-jnp.power(2.0, -5.0 - jnp.arange(N_RET, dtype=F32)))
    i = jnp.arange(c, dtype=F32)
    diff = i[:, None] - i[None, :]
    dmat = jnp.where(diff >= 0, jnp.exp(jnp.maximum(diff, 0.0)[None] * log_g[:, None, None]), 0.0)
    xi = jnp.exp((i[None, :] + 1.0) * log_g[:, None])
    zeta = jnp.where(i[None, :] < chunk, jnp.exp((chunk - 1.0 - i)[None, :] * log_g[:, None]), 0.0)
    g_c = jnp.exp(chunk * log_g)
    return (dmat,
            jnp.broadcast_to(xi[:, :, None], (N_RET, c, DV_RET)),
            jnp.broadcast_to(zeta[:, :, None], (N_RET, c, DK_RET)),
            jnp.broadcast_to(g_c[:, None, None], (N_RET, DK_RET, DV_RET)))


def _outproj_kernel(on_ref, or_ref, x_ref, gt_ref, sc_ref, sh_ref, gn_ref, g2_ref, w_ref, x1_ref, h2_ref):
    a = _rms(on_ref[...], gn_ref[...])
    mix = _dot(a.astype(BF16), w_ref[0:D_NSA, :]) + _dot(or_ref[...].astype(BF16), w_ref[D_NSA:, :])
    x1 = x_ref[...] + gt_ref[...] * mix
    x1_ref[...] = x1
    h2_ref[...] = (_rms(x1, g2_ref[...]) * (1.0 + sc_ref[...]) + sh_ref[...]).astype(h2_ref.dtype)


def _outproj_call(o_nsa, o_ret, x2d, gt3, sc3, sh3, mod_idx, g_nsa, g2, w_out, tm, h_dtype):
    m = x2d.shape[0]
    mrows = gt3.shape[1]
    mod_spec = pl.BlockSpec((None, mrows, D_MODEL), lambda i: (mod_idx(i), 0, 0))
    return pl.pallas_call(
        _outproj_kernel,
        grid=(m // tm,),
        in_specs=[pl.BlockSpec((tm, D_NSA), lambda i: (i, 0)),
                  pl.BlockSpec((tm, D_RET), lambda i: (i, 0)),
                  pl.BlockSpec((tm, D_MODEL), lambda i: (i, 0)),
                  mod_spec, mod_spec, mod_spec,
                  pl.BlockSpec((1, D_NSA), lambda i: (0, 0)),
                  pl.BlockSpec((1, D_MODEL), lambda i: (0, 0)),
                  pl.BlockSpec((D_NSA + D_RET, D_MODEL), lambda i: (0, 0))],
        out_specs=[pl.BlockSpec((tm, D_MODEL), lambda i: (i, 0)),
                   pl.BlockSpec((tm, D_MODEL), lambda i: (i, 0))],
        out_shape=[jax.ShapeDtypeStruct((m, D_MODEL), F32),
                   jax.ShapeDtypeStruct((m, D_MODEL), h_dtype)],
        compiler_params=_cparams("parallel"),
        name="out_proj",
    )(o_nsa, o_ret, x2d, gt3, sc3, sh3, g_nsa, g2, w_out)


FF_TILE = D_FF // 2


def _ffn_up_kernel(h_ref, wa_ref, wb_ref, cwa_ref, cwb_ref, cba_ref, cbb_ref, sta_ref, stb_ref,
                   act_ref, csa_ref, csb_ref, prev_a, prev_b):
    @pl.when(pl.program_id(2) == 0)
    def _():
        prev_a[...] = sta_ref[...]
        prev_b[...] = stb_ref[...]

    h = h_ref[...].astype(BF16)
    tm = h.shape[0]
    row = lax.broadcasted_iota(jnp.int32, (tm, 1), 0)

    def half(w_ref, cw_ref, cb_ref, prev, cs_ref):
        u = _dot(h, w_ref[...])
        p = prev[...]
        back1 = pltpu.roll(u, 1, 0)
        back2 = pltpu.roll(u, 2, 0)
        u1 = jnp.where(row >= 1, back1, p[1:2])
        u2 = jnp.where(row >= 2, back2, jnp.where(row == 1, p[1:2], p[0:1]))
        y = cb_ref[...] + cw_ref[0:1] * u2 + cw_ref[1:2] * u1 + cw_ref[2:3] * u
        tail = back2[0:2]
        prev[...] = tail
        cs_ref[...] = tail
        return y

    a = half(wa_ref, cwa_ref, cba_ref, prev_a, csa_ref)
    b = half(wb_ref, cwb_ref, cbb_ref, prev_b, csb_ref)
    act_ref[...] = (a * jax.nn.sigmoid(a) * b).astype(act_ref.dtype)


def _ffn_up_call(h2, w_up_a, w_up_b, conv_w, conv_b, conv_state, nb, seq, tm, act_dtype):
    nrt = seq // tm
    nt = D_FF // FF_TILE
    return pl.pallas_call(
        _ffn_up_kernel,
        grid=(nt, nb, nrt),
        in_specs=[pl.BlockSpec((tm, D_MODEL), lambda j, b, i: (b * nrt + i, 0)),
                  pl.BlockSpec((D_MODEL, FF_TILE), lambda j, b, i: (0, j)),
                  pl.BlockSpec((D_MODEL, FF_TILE), lambda j, b, i: (0, j)),
                  pl.BlockSpec((3, FF_TILE), lambda j, b, i: (0, j)),
                  pl.BlockSpec((3, FF_TILE), lambda j, b, i: (0, nt + j)),
                  pl.BlockSpec((1, FF_TILE), lambda j, b, i: (0, j)),
                  pl.BlockSpec((1, FF_TILE), lambda j, b, i: (0, nt + j)),
                  pl.BlockSpec((None, 2, FF_TILE), lambda j, b, i: (b, 0, j)),
                  pl.BlockSpec((None, 2, FF_TILE), lambda j, b, i: (b, 0, nt + j))],
        out_specs=[pl.BlockSpec((tm, FF_TILE), lambda j, b, i: (b * nrt + i, j)),
                   pl.BlockSpec((None, 2, FF_TILE), lambda j, b, i: (b, 0, j)),
                   pl.BlockSpec((None, 2, FF_TILE), lambda j, b, i: (b, 0, j))],
        out_shape=[jax.ShapeDtypeStruct((nb * seq, D_FF), act_dtype),
                   jax.ShapeDtypeStruct((nb, 2, D_FF), F32),
                   jax.ShapeDtypeStruct((nb, 2, D_FF), F32)],
        scratch_shapes=[pltpu.VMEM((2, FF_TILE), F32), pltpu.VMEM((2, FF_TILE), F32)],
        compiler_params=_cparams("parallel", "parallel", "arbitrary"),
        name="ffn_up_conv",
    )(h2, w_up_a, w_up_b, conv_w, conv_w, conv_b, conv_b, conv_state, conv_state)


def _ffn_down_kernel(a_ref, x1_ref, gt_ref, w_ref, gf_ref, y_ref, *, final_norm):
    x2 = x1_ref[...] + gt_ref[...] * _dot(a_ref[...].astype(BF16), w_ref[...])
    y_ref[...] = _rms(x2, gf_ref[...]) if final_norm else x2


def _ffn_down_call(act, x1, gt3, mod_idx, w_down, g_final, tm, final_norm):
    m = x1.shape[0]
    mrows = gt3.shape[1]
    return pl.pallas_call(
        functools.partial(_ffn_down_kernel, final_norm=final_norm),
        grid=(m // tm,),
        in_specs=[pl.BlockSpec((tm, D_FF), lambda i: (i, 0)),
                  pl.BlockSpec((tm, D_MODEL), lambda i: (i, 0)),
                  pl.BlockSpec((None, mrows, D_MODEL), lambda i: (mod_idx(i), 0, 0)),
                  pl.BlockSpec((D_FF, D_MODEL), lambda i: (0, 0)),
                  pl.BlockSpec((1, D_MODEL), lambda i: (0, 0))],
        out_specs=pl.BlockSpec((tm, D_MODEL), lambda i: (i, 0)),
        out_shape=jax.ShapeDtypeStruct((m, D_MODEL), F32),
        compiler_params=_cparams("parallel"),
        name="ffn_down",
    )(act, x1, gt3, w_down, g_final)


def _relayout_w_in(w_in):
    q_n, kv_n, gate_n, q_r, k_r, v_r, g_r = jnp.split(w_in, [512, 1280, 1304, 1560, 1816, 2328], axis=1)
    gate = gate_n.reshape(D_MODEL, 3, N_KV, REP)
    gate = jnp.transpose(gate, (0, 2, 1, 3)).reshape(D_MODEL, N_KV, 3 * REP)
    gate = jnp.pad(gate, ((0, 0), (0, 0), (0, LANES - 3 * REP))).reshape(D_MODEL, N_KV * LANES)
    return jnp.concatenate([q_n, kv_n, q_r, k_r, v_r, g_r, gate], axis=1).astype(BF16)


def _relayout_cmp(w1, pos):
    w1r = w1.reshape(2, CMP_STRIDE, HD, HD)
    w = jnp.einsum('atdn,gh->tgdahn', w1r, jnp.eye(N_KV, dtype=w1.dtype)).reshape(CMP_STRIDE * N_KV * HD, 2 * N_KV * HD)
    posr = pos.reshape(2, CMP_STRIDE, 1, HD)
    prow = jnp.broadcast_to(posr, (2, CMP_STRIDE, N_KV, HD)).reshape(2, CMP_STRIDE * N_KV * HD)
    prow = jnp.pad(prow, ((0, 14), (0, 0)))
    return w.astype(BF16), prow.astype(BF16)


def _block_diag4(w2k, w2v):
    z = jnp.zeros((HD, HD), w2k.dtype)
    rows = [[w2k, z, z, z], [z, w2k, z, z], [z, z, w2v, z], [z, z, z, w2v]]
    return jnp.block(rows).astype(BF16)


def _rope_tables(pos):
    half = HD // 2
    inv = ROPE_THETA ** (-jnp.arange(half, dtype=F32) / half)
    ang = pos.astype(F32)[:, None] * inv[None, :]
    cos, sin = jnp.cos(ang), jnp.sin(ang)
    return jnp.tile(cos, (1, 4)), jnp.tile(jnp.concatenate([-sin, sin], axis=1), (1, 2))


def _importance_matrix(nc, nsp):
    n = np.arange(nc)[:, None]
    d = n - 4 * np.arange(nsp)[None, :]
    m = ((d >= 0) & (d <= 3)).astype(np.float32) + ((d >= -1) & (d <= 2)).astype(np.float32)
    return jnp.asarray(m, dtype=BF16)


def _block_onehot_t(past):
    blk = np.arange(LANES)[:, None]
    return jnp.asarray((blk == np.arange(past)[None, :] // SEL_BLOCK).astype(np.float32), dtype=BF16)


def _round_up(x, m):
    return (x + m - 1) // m * m


def kernel(x_prompt, x_sample, c_prompt, c_sample, cache_nsa_kv, cache_win_kv, state_ret, state_conv, page_table,
           w_ada, b_ada, g_norm1, w_in, cmp_pos_k, cmp_w1_k, cmp_w2_k, cmp_pos_v, cmp_w1_v, cmp_w2_v,
           g_nsa_out, g_ret_out, w_out, g_norm2, w_up, conv_w, conv_b, w_down, g_final):
    nb_p, seq, _ = x_prompt.shape
    nb_s, ts, _ = x_sample.shape
    depth = w_ada.shape[0]
    n_pages = page_table.shape[1]
    past = n_pages * PAGE
    wb = cache_win_kv.shape[2]
    m_p, m_s = nb_p * seq, nb_s * ts
    tm_p = 512

    xp = x_prompt.reshape(m_p, D_MODEL)
    xs = x_sample.reshape(m_s, D_MODEL)
    n_c = nb_p + nb_s
    c_all = jnp.pad(jnp.concatenate([c_prompt, c_sample], axis=0), ((0, _round_up(n_c, 8) - n_c), (0, 0)))

    cos_p, sin_p = _rope_tables(jnp.arange(seq, dtype=jnp.int32))
    cos_s, sin_s = _rope_tables(jnp.tile(past + jnp.arange(ts, dtype=jnp.int32), nb_s))
    tabs_p = _ret_tables(min(RET_CHUNK, seq))
    tabs_s = _ret_tables(ts)
    nc_p, nc_s = seq // CMP_STRIDE, past // CMP_STRIDE
    nsp_p = _round_up(seq // SEL_BLOCK, LANES)
    nsp_s = _round_up(past // SEL_BLOCK + 1, LANES)
    mmat_p = _importance_matrix(nc_p, nsp_p)
    blk_onehot_p = jnp.tile(jax.nn.one_hot(jnp.arange(seq) // SEL_BLOCK, LANES, dtype=BF16), (nb_p, 1))[None]
    mmat_s, oht_s = _importance_matrix(nc_s, nsp_s), _block_onehot_t(past)

    tiles_per_batch = seq // tm_p
    idx_p = lambda i: i // tiles_per_batch
    tab_p = lambda i: i % tiles_per_batch
    idx_s = lambda i: 0

    outs = {k: [] for k in ('kv_p', 'kv_s', 'win_p', 'win_s', 'ret_p', 'ret_s', 'conv_p', 'conv_s')}
    for l in range(depth):
        mod = _mod_call(c_all, w_ada[l], b_ada[l])
        mods_p = [a.reshape(nb_p, 1, D_MODEL) for a in jnp.split(mod[:nb_p], 6, axis=1)]
        mods_s = [jnp.repeat(a, ts, axis=0).reshape(1, m_s, D_MODEL) for a in jnp.split(mod[nb_p:n_c], 6, axis=1)]
        w_in_re = _relayout_w_in(w_in[l])
        wk, pos_k = _relayout_cmp(cmp_w1_k[l], cmp_pos_k[l])
        wv, pos_v = _relayout_cmp(cmp_w1_v[l], cmp_pos_v[l])
        pos_kv = jnp.stack([pos_k, pos_v])
        w2bd = _block_diag4(cmp_w2_k[l], cmp_w2_v[l])
        g1 = g_norm1[l].reshape(1, -1)
        g2 = g_norm2[l].reshape(1, -1)
        g_nsa = g_nsa_out[l].reshape(1, -1)
        g_ret = g_ret_out[l].reshape(1, -1)
        w_out_bf = w_out[l].astype(BF16)
        w_up_a = w_up[l][:, :D_FF].astype(BF16)
        w_up_b = w_up[l][:, D_FF:].astype(BF16)
        w_down_bf = w_down[l].astype(BF16)

        sh1, sc1, gt1, sh2, sc2, gt2 = mods_p
        qh_p, kvn_t, kvw_t, kvh_p, ret_p, gat_p, kvc_p = _inproj_call(
            xp, sc1, sh1, idx_p, g1, w_in_re, cos_p, sin_p, tab_p, tm_p, BF16, tiles_per_batch)
        sh1s, sc1s, gt1s, sh2s, sc2s, gt2s = mods_s
        qh_s, kvn_s, kvw_s, kvh_s, ret_s, gat_s = _inproj_call(
            xs, sc1s, sh1s, idx_s, g1, w_in_re, cos_s, sin_s, idx_s, m_s, F32)

        ab_p = _cmp_prompt_call(kvc_p, wk, wv, pos_kv).reshape(nb_p, nc_p, 512)
        cache_t = jnp.transpose(cache_nsa_kv[:, l], (0, 2, 3, 4, 1)).reshape(cache_nsa_kv.shape[0], 512, PAGE)
        win_t = jnp.transpose(cache_win_kv[:, l], (0, 2, 3, 4, 1)).reshape(nb_s, 256, wb)
        ab_s, ks_s = _cmp_sample_call(cache_t, page_table, wk, wv, pos_kv)
        cmp_p = _cmp_fin_call(ab_p, w2bd)
        cmp_s = _cmp_fin_call(ab_s, w2bd)

        ones_col = jnp.concatenate([jnp.ones((N_KV, m_p, 1), BF16), jnp.zeros((N_KV, m_p, LANES - HD - 1), BF16)], axis=-1)
        kaug_p = jnp.concatenate([jnp.broadcast_to(blk_onehot_p, (N_KV, m_p, LANES)), kvh_p[0:2]], axis=-1)
        vaug_p = jnp.concatenate([kvh_p[2:4], ones_col], axis=-1)
        vwaug_p = jnp.concatenate([kvh_p[6:8], ones_col], axis=-1)
        o_nsa_p = _nsa_prompt_call(qh_p, gat_p, cmp_p, kaug_p, vaug_p, kvh_p[4:6], vwaug_p, mmat_p, nb_p, seq, 0)
        o_nsa_s = _nsa_sample_call(qh_s, gat_s, cmp_s, ks_s, kvh_s, win_t, mmat_s, oht_s, nb_s, ts, past, 0)

        s0_p = jnp.zeros((nb_p, N_RET, DK_RET, DV_RET), F32)
        o_ret_p, s_new_p = _ret_call(ret_p, s0_p, tabs_p, g_ret, nb_p, seq, min(RET_CHUNK, seq))
        o_ret_s, s_new_s = _ret_call(ret_s, state_ret[:, l], tabs_s, g_ret, nb_s, ts, ts)

        x1_p, h2_p = _outproj_call(o_nsa_p, o_ret_p, xp, gt1, sc2, sh2, idx_p, g_nsa, g2, w_out_bf, tm_p, BF16)
        x1_s, h2_s = _outproj_call(o_nsa_s, o_ret_s, xs, gt1s, sc2s, sh2s, idx_s, g_nsa, g2, w_out_bf, m_s, F32)

        conv0_p = jnp.zeros((nb_p, 2, 2 * D_FF), F32)
        act_p, csa_p, csb_p = _ffn_up_call(h2_p, w_up_a, w_up_b, conv_w[l], conv_b[l].reshape(1, -1), conv0_p,
                                           nb_p, seq, tm_p, BF16)
        act_s, csa_s, csb_s = _ffn_up_call(h2_s, w_up_a, w_up_b, conv_w[l], conv_b[l].reshape(1, -1),
                                           state_conv[:, l], nb_s, ts, ts, F32)
        last = l == depth - 1
        gf = g_final.reshape(1, -1)
        xp = _ffn_down_call(act_p, x1_p, gt2, idx_p, w_down_bf, gf, tm_p, last)
        xs = _ffn_down_call(act_s, x1_s, gt2s, idx_s, w_down_bf, gf, m_s, last)

        outs['kv_p'].append(jnp.transpose(kvn_t.reshape(nb_p, 4, N_KV, HD, seq), (0, 4, 1, 2, 3)))
        outs['kv_s'].append(kvn_s.reshape(nb_s, ts, 4, N_KV, HD))
        keep_p = min(WINDOW, seq)
        win_t_p = kvw_t[:, :, seq - keep_p:].reshape(nb_p, 2, N_KV, HD, keep_p)
        outs['win_p'].append(jnp.transpose(win_t_p, (0, 4, 1, 2, 3)))
        win_all = jnp.concatenate([cache_win_kv[:, l], kvw_s.reshape(nb_s, ts, 2, N_KV, HD)], axis=1)
        outs['win_s'].append(win_all[:, win_all.shape[1] - min(WINDOW, past + ts):])
        outs['ret_p'].append(s_new_p)
        outs['ret_s'].append(s_new_s)
        outs['conv_p'].append(jnp.concatenate([csa_p, csb_p], axis=-1))
        outs['conv_s'].append(jnp.concatenate([csa_s, csb_s], axis=-1))

    st = lambda k: jnp.stack(outs[k], axis=1)
    return (xp.reshape(nb_p, seq, D_MODEL), xs.reshape(nb_s, ts, D_MODEL),
            st('kv_p'), st('kv_s'), st('win_p'), st('win_s'), st('ret_p'), st('ret_s'), st('conv_p'), st('conv_s'))
```

```python
import functools

import numpy as np
import jax
import jax.numpy as jnp
from jax import lax
from jax.experimental import pallas as pl
from jax.experimental.pallas import tpu as pltpu

F32 = jnp.float32
BF16 = jnp.bfloat16

D_MODEL = 1024
PAGE = 128
HD = 64
N_KV = 2
REP = 4
D_NSA = 512
CMP_STRIDE = 16
SEL_BLOCK = 64
N_SEL = 16
WINDOW = 512
Q_TILE = 512
KV_TILE = 512
FORCE_BONUS = 100.0
N_RET = 4
DK_RET = 64
DV_RET = 128
D_RET = 512
RET_CHUNK = 128
D_FF = 2816
ROPE_THETA = 10000.0
EPS = 1e-6
NEG_INF = -1e30
LOG2_E = 1.4426950408889634
LANES = 128
W_IN_COLS = 3072
VMEM_LIMIT = 56 * 1024 * 1024


def _cparams(*sem):
    return pltpu.CompilerParams(dimension_semantics=sem, vmem_limit_bytes=VMEM_LIMIT)


def _dot(a, b):
    return jnp.dot(a, b, preferred_element_type=F32)


def _dot_nt(a, b):
    return lax.dot_general(a, b, (((1,), (1,)), ((), ())), preferred_element_type=F32)


def _dot_tn(a, b):
    return lax.dot_general(a, b, (((0,), (0,)), ((), ())), preferred_element_type=F32)


def _dot_split3(p, m_bf16):
    hi = p.astype(BF16)
    r1 = p - hi.astype(F32)
    mid = r1.astype(BF16)
    lo = (r1 - mid.astype(F32)).astype(BF16)
    return _dot(hi, m_bf16) + _dot(mid, m_bf16) + _dot(lo, m_bf16)


def _rms(x, g):
    return x * lax.rsqrt(jnp.mean(x * x, axis=-1, keepdims=True) + EPS) * g


def _masked_softmax(s, mask):
    s = jnp.where(mask, s, NEG_INF)
    m = jnp.max(s, axis=-1, keepdims=True)
    e = jnp.where(mask, jnp.exp(s - m), 0.0)
    return e / jnp.maximum(jnp.sum(e, axis=-1, keepdims=True), 1e-30)


def _topk_mask_t(score_t, n_sel):
    nb = score_t.shape[0]
    blk = lax.broadcasted_iota(jnp.int32, score_t.shape, 0)

    def body(_, work):
        m = jnp.max(work, axis=0, keepdims=True)
        idx = jnp.min(jnp.where(work == m, blk, nb), axis=0, keepdims=True)
        return jnp.where(blk == idx, -jnp.inf, work)

    work = lax.fori_loop(0, n_sel, body, score_t)
    return jnp.where(work == -jnp.inf, jnp.where(score_t == -jnp.inf, 0.0, 1.0), 0.0)


def _selection(psum, mmat, qpos, ns, n_sel):
    imp = _dot_split3(psum, mmat)
    blk = lax.broadcasted_iota(jnp.int32, (1, imp.shape[1]), 1)
    qblk = qpos >> 6
    valid = (blk * SEL_BLOCK <= qpos) & (blk < ns)
    forced = (blk == 0) | (blk == qblk) | (blk == qblk - 1)
    score = jnp.where(valid, imp + jnp.where(forced, FORCE_BONUS, 0.0), NEG_INF)
    score = jnp.where(blk < ns, score, -jnp.inf)
    sel = _topk_mask_t(score.T, n_sel).T
    return jnp.where(valid, sel, 0.0)


def _mod_kernel(c_ref, w_ref, b_ref, o_ref):
    c = c_ref[...]
    o_ref[...] = _dot(c * jax.nn.sigmoid(c), w_ref[...]) + b_ref[...]


def _mod_call(c_all, w_ada, b_ada):
    n = c_all.shape[0]
    tn = 1536
    return pl.pallas_call(
        _mod_kernel,
        grid=(w_ada.shape[1] // tn,),
        in_specs=[pl.BlockSpec((n, D_MODEL), lambda j: (0, 0)),
                  pl.BlockSpec((D_MODEL, tn), lambda j: (0, j)),
                  pl.BlockSpec((1, tn), lambda j: (0, j))],
        out_specs=pl.BlockSpec((n, tn), lambda j: (0, j)),
        out_shape=jax.ShapeDtypeStruct((n, w_ada.shape[1]), F32),
        compiler_params=_cparams("arbitrary"),
        name="adaln_mod",
    )(c_all, w_ada, b_ada.reshape(1, -1))


def _inproj_kernel(x_ref, sc_ref, sh_ref, g1_ref, w_ref, cos_ref, sin_ref, qh_ref, kvn_ref, kvw_ref, *rest,
                   q_scale, tiles_per_batch):
    prompt = tiles_per_batch is not None
    if prompt:
        kaug_ref, vaug_ref, kwin_ref, vwaug_ref, ret_ref, gat_ref, kvc_ref = rest
        tm = x_ref.shape[0]
        pos = ((pl.program_id(0) % tiles_per_batch) * tm + lax.broadcasted_iota(jnp.int32, (tm, 1), 0))
        blk_onehot = jnp.where(lax.broadcasted_iota(jnp.int32, (1, LANES), 1) == (pos >> (SEL_BLOCK.bit_length() - 1)),
                               1.0, 0.0).astype(BF16)
        ones_cols = jnp.where(lax.broadcasted_iota(jnp.int32, (tm, HD), 1) == 0, 1.0, 0.0).astype(BF16)
    else:
        kvh_ref, ret_ref, gat_ref = rest
    h = (_rms(x_ref[...], g1_ref[...]) * (1.0 + sc_ref[...]) + sh_ref[...]).astype(BF16)
    cos = cos_ref[...]
    sin = sin_ref[...]
    lane = lax.broadcasted_iota(jnp.int32, (1, LANES), 1)
    first_half = (lane & (HD - 1)) < HD // 2

    def seg(c0, n):
        return _dot(h, w_ref[:, c0:c0 + n])

    def rope(a):
        sw = jnp.where(first_half, pltpu.roll(a, LANES - HD // 2, 1), pltpu.roll(a, HD // 2, 1))
        return a * cos + sw * sin

    for j in range(4):
        q = rope(seg(LANES * j, LANES)) * q_scale
        qh_ref[2 * j] = q[:, :HD].astype(qh_ref.dtype)
        qh_ref[2 * j + 1] = q[:, HD:].astype(qh_ref.dtype)
    for s in range(6):
        a = seg(512 + LANES * s, LANES)
        if s % 2 == 0:
            a = rope(a)
        out_ref, s_out = (kvn_ref, s) if s < 4 else (kvw_ref, s - 4)
        if not prompt:
            out_ref[:, LANES * s_out:LANES * (s_out + 1)] = a
            if s >= 2:
                kvh_ref[2 * (s - 2)] = a[:, :HD].astype(kvh_ref.dtype)
                kvh_ref[2 * (s - 2) + 1] = a[:, HD:].astype(kvh_ref.dtype)
            continue
        out_ref[LANES * s_out:LANES * (s_out + 1), :] = a.T
        if s < 2:
            kvc_ref[s] = a
            continue
        for g in range(N_KV):
            part = a[:, HD * g:HD * (g + 1)].astype(BF16)
            if s == 2:
                kaug_ref[g] = jnp.concatenate([blk_onehot, part], axis=1)
            elif s == 3:
                vaug_ref[g] = jnp.concatenate([part, ones_cols], axis=1)
            elif s == 4:
                kwin_ref[g] = part
            else:
                vwaug_ref[g] = jnp.concatenate([part, ones_cols], axis=1)
    for j in range(2):
        ret_ref[:, LANES * j:LANES * (j + 1)] = rope(seg(1280 + LANES * j, LANES)) * (DK_RET ** -0.5)
        ret_ref[:, 256 + LANES * j:256 + LANES * (j + 1)] = rope(seg(1536 + LANES * j, LANES))
    ret_ref[:, 512:1024] = seg(1792, 512)
    ret_ref[:, 1024:1536] = seg(2304, 512)
    gat_ref[...] = seg(2816, 256)


def _inproj_call(x2d, sc3, sh3, mod_idx, g1, w_in_re, cos, sin, tab_idx, tm, q_scale, tiles_per_batch=None):
    m = x2d.shape[0]
    mrows = sc3.shape[1]
    rows = lambda width: pl.BlockSpec((tm, width), lambda i: (i, 0))
    slabs = lambda n, width: pl.BlockSpec((n, tm, width), lambda i: (0, i, 0))
    tail_specs = [rows(1536), rows(256)]
    tail_shapes = [jax.ShapeDtypeStruct((m, 1536), F32), jax.ShapeDtypeStruct((m, 256), F32)]
    if tiles_per_batch is not None:
        nb, seq = m // (tiles_per_batch * tm), tiles_per_batch * tm
        t_idx = lambda i: (i // tiles_per_batch, 0, i % tiles_per_batch)
        out_specs = ([slabs(8, HD), pl.BlockSpec((None, 512, tm), t_idx), pl.BlockSpec((None, 256, tm), t_idx),
                      slabs(N_KV, LANES + HD), slabs(N_KV, LANES), slabs(N_KV, HD), slabs(N_KV, LANES)]
                     + tail_specs + [slabs(2, LANES)])
        out_shape = ([jax.ShapeDtypeStruct((8, m, HD), BF16),
                      jax.ShapeDtypeStruct((nb, 512, seq), F32), jax.ShapeDtypeStruct((nb, 256, seq), F32),
                      jax.ShapeDtypeStruct((N_KV, m, LANES + HD), BF16), jax.ShapeDtypeStruct((N_KV, m, LANES), BF16),
                      jax.ShapeDtypeStruct((N_KV, m, HD), BF16), jax.ShapeDtypeStruct((N_KV, m, LANES), BF16)]
                     + tail_shapes + [jax.ShapeDtypeStruct((2, m, LANES), F32)])
    else:
        out_specs = [slabs(8, HD), rows(512), rows(256), slabs(8, HD)] + tail_specs
        out_shape = [jax.ShapeDtypeStruct((8, m, HD), F32), jax.ShapeDtypeStruct((m, 512), F32),
                     jax.ShapeDtypeStruct((m, 256), F32), jax.ShapeDtypeStruct((8, m, HD), F32)] + tail_shapes
    return pl.pallas_call(
        functools.partial(_inproj_kernel, q_scale=q_scale, tiles_per_batch=tiles_per_batch),
        grid=(m // tm,),
        in_specs=[pl.BlockSpec((tm, D_MODEL), lambda i: (i, 0)),
                  pl.BlockSpec((None, mrows, D_MODEL), lambda i: (mod_idx(i), 0, 0)),
                  pl.BlockSpec((None, mrows, D_MODEL), lambda i: (mod_idx(i), 0, 0)),
                  pl.BlockSpec((1, D_MODEL), lambda i: (0, 0)),
                  pl.BlockSpec((D_MODEL, W_IN_COLS), lambda i: (0, 0)),
                  pl.BlockSpec((tm, LANES), lambda i: (tab_idx(i), 0)),
                  pl.BlockSpec((tm, LANES), lambda i: (tab_idx(i), 0))],
        out_specs=out_specs,
        out_shape=out_shape,
        compiler_params=_cparams("parallel"),
        name="in_proj",
    )(x2d, sc3, sh3, g1, w_in_re, cos, sin)


def _cmp_partial(slab, wk_ref, wv_ref, pos_ref, ab_ref):
    for kind, w_ref in ((0, wk_ref), (1, wv_ref)):
        xcat = jnp.concatenate([slab(t, kind) for t in range(CMP_STRIDE)], axis=1).astype(BF16)
        r = _dot(xcat, w_ref[...])
        bias = _dot(pos_ref[kind], w_ref[...])
        ab_ref[:, LANES * kind:LANES * (kind + 1)] = r[:, :LANES] + bias[0:1, :LANES]
        ab_ref[:, 256 + LANES * kind:256 + LANES * (kind + 1)] = r[:, LANES:] + bias[1:2, LANES:]


def _cmp_prompt_kernel(x_ref, wk_ref, wv_ref, pos_ref, ab_ref):
    rows = ab_ref.shape[0]
    _cmp_partial(lambda t, kind: x_ref[kind, pl.ds(t, rows, stride=CMP_STRIDE), :],
                 wk_ref, wv_ref, pos_ref, ab_ref)


def _cmp_prompt_call(kvc, wk, wv, pos):
    rows = kvc.shape[1] // CMP_STRIDE
    tr = min(128, rows)
    return pl.pallas_call(
        _cmp_prompt_kernel,
        grid=(rows // tr,),
        in_specs=[pl.BlockSpec((2, tr * CMP_STRIDE, LANES), lambda i: (0, i, 0)),
                  pl.BlockSpec(wk.shape, lambda i: (0, 0)),
                  pl.BlockSpec(wv.shape, lambda i: (0, 0)),
                  pl.BlockSpec(pos.shape, lambda i: (0, 0, 0))],
        out_specs=pl.BlockSpec((tr, 512), lambda i: (i, 0)),
        out_shape=jax.ShapeDtypeStruct((rows, 512), F32),
        compiler_params=_cparams("parallel"),
        name="cmp_partial_prompt",
    )(kvc, wk, wv, pos)


PAGES_PER_STEP = 16


def _cmp_sample_kernel(pt_ref, *refs):
    pages = refs[:PAGES_PER_STEP]
    wk_ref, wv_ref, pos_ref, ab_ref, ks_ref, x_sc = refs[PAGES_PER_STEP:]
    for p, page in enumerate(pages):
        for kind in range(2):
            x_sc[kind, PAGE * p:PAGE * (p + 1), :] = page[LANES * kind:LANES * (kind + 1), :].T
        ks_ref[:, PAGE * p:PAGE * (p + 1)] = page[256:512, :].astype(BF16)
    rows = PAGES_PER_STEP * (PAGE // CMP_STRIDE)
    _cmp_partial(lambda t, kind: x_sc[kind, pl.ds(t, rows, stride=CMP_STRIDE), :],
                 wk_ref, wv_ref, pos_ref, ab_ref)


def _cmp_sample_call(cache_t, page_table, wk, wv, pos):
    nb, n_pages = page_table.shape
    chunks = PAGE // CMP_STRIDE
    steps = n_pages // PAGES_PER_STEP

    def page_spec(k):
        return pl.BlockSpec((None, 512, PAGE), lambda b, j, pt: (pt[b, j * PAGES_PER_STEP + k], 0, 0))

    grid_spec = pltpu.PrefetchScalarGridSpec(
        num_scalar_prefetch=1,
        grid=(nb, steps),
        in_specs=[page_spec(k) for k in range(PAGES_PER_STEP)] + [
            pl.BlockSpec(wk.shape, lambda b, j, pt: (0, 0)),
            pl.BlockSpec(wv.shape, lambda b, j, pt: (0, 0)),
            pl.BlockSpec(pos.shape, lambda b, j, pt: (0, 0, 0))],
        out_specs=[pl.BlockSpec((None, PAGES_PER_STEP * chunks, 512), lambda b, j, pt: (b, j, 0)),
                   pl.BlockSpec((None, 256, PAGES_PER_STEP * PAGE), lambda b, j, pt: (b, 0, j))],
        scratch_shapes=[pltpu.VMEM((2, PAGES_PER_STEP * PAGE, LANES), F32)],
    )
    return pl.pallas_call(
        _cmp_sample_kernel,
        grid_spec=grid_spec,
        out_shape=[jax.ShapeDtypeStruct((nb, n_pages * chunks, 512), F32),
                   jax.ShapeDtypeStruct((nb, 256, n_pages * PAGE), BF16)],
        compiler_params=_cparams("parallel", "arbitrary"),
        name="cmp_partial_sample",
    )(page_table, *([cache_t] * PAGES_PER_STEP), wk, wv, pos)


def _cmp_fin_kernel(ab_ref, w2_ref, o_ref):
    nc = ab_ref.shape[0]
    a = ab_ref[:, :256]
    b_next = pltpu.roll(ab_ref[:, 256:], nc - 1, 0)
    hid = jax.nn.gelu(a + b_next)
    out = _dot(hid.astype(BF16), w2_ref[...])
    row = lax.broadcasted_iota(jnp.int32, (nc, 1), 0)
    out = jnp.where(row < nc - 1, out, 0.0)
    for s in range(4):
        o_ref[s] = out[:, HD * s:HD * (s + 1)]


def _cmp_fin_call(ab, w2bd):
    nb, nc, _ = ab.shape
    return pl.pallas_call(
        _cmp_fin_kernel,
        grid=(nb,),
        in_specs=[pl.BlockSpec((None, nc, 512), lambda b: (b, 0, 0)),
                  pl.BlockSpec((256, 256), lambda b: (0, 0))],
        out_specs=pl.BlockSpec((None, 4, nc, HD), lambda b: (b, 0, 0, 0)),
        out_shape=jax.ShapeDtypeStruct((nb, 4, nc, HD), F32),
        compiler_params=_cparams("parallel"),
        name="cmp_finish",
    )(ab, w2bd)


def _nsa_prompt_kernel(q_ref, gat_ref, kc_ref, vc_ref, ka_ref, va_ref, kw_ref, vw_ref, mmat_ref,
                       o_ref, qa_sc, s_sc, mcur_sc, m_sc, acc_sc, *, ns, n_sel):
    i = pl.program_id(2)
    start = i * Q_TILE
    qpos = start + lax.broadcasted_iota(jnp.int32, (Q_TILE, 1), 0)
    nc = kc_ref.shape[0]
    kc = kc_ref[...].astype(BF16)
    vc = vc_ref[...].astype(BF16)
    head_rows = [slice(r * Q_TILE, (r + 1) * Q_TILE) for r in range(REP)]

    c_end = lax.broadcasted_iota(jnp.int32, (1, nc), 1) * CMP_STRIDE + (2 * CMP_STRIDE - 1)
    cbias = jnp.where(c_end <= qpos, 0.0, NEG_INF)
    any_valid = qpos >= 2 * CMP_STRIDE - 1
    psum = jnp.zeros((Q_TILE, nc), F32)
    o_cmp = []
    for r in range(REP):
        s = _dot_nt(q_ref[r], kc) + cbias
        e = jnp.exp2(s - jnp.max(s, axis=-1, keepdims=True))
        norm = jnp.where(any_valid, 1.0 / jnp.maximum(jnp.sum(e, axis=-1, keepdims=True), 1e-30), 0.0)
        p = e * norm
        psum = psum + p
        o_cmp.append(_dot(p.astype(BF16), vc))

    sel = _selection(psum, mmat_ref[...], qpos, ns, n_sel)
    selneg = jnp.where(sel > 0.5, 0.0, NEG_INF).astype(BF16)
    for r in range(REP):
        qa_sc[head_rows[r], :] = jnp.concatenate([selneg, q_ref[r]], axis=1)

    n_full = start // KV_TILE

    def scores(j):
        k0 = pl.multiple_of(j * KV_TILE, KV_TILE)
        return _dot_nt(qa_sc[...], ka_ref[pl.ds(k0, KV_TILE), :])

    def consume(slot, t):
        v0 = pl.multiple_of(t * KV_TILE, KV_TILE)
        v_t = va_ref[pl.ds(v0, KV_TILE), :]
        for r in range(REP):
            rows = head_rows[r]
            m_old = m_sc[rows]
            m_new = jnp.maximum(m_old, mcur_sc[slot, rows])
            p = jnp.exp2((s_sc[slot, rows] - m_new).astype(BF16))
            acc_sc[rows] = jnp.exp2(m_old - m_new) * acc_sc[rows] + _dot(p, v_t)
            m_sc[rows] = m_new

    kpos = n_full * KV_TILE + lax.broadcasted_iota(jnp.int32, (1, KV_TILE), 1)
    causal_bias = jnp.where(kpos <= qpos, 0.0, NEG_INF)
    s_diag = scores(n_full)
    for r in range(REP):
        s_r = s_diag[head_rows[r]] + causal_bias
        s_sc[0, head_rows[r]] = s_r
        mcur_sc[0, head_rows[r]] = jnp.max(s_r, axis=-1, keepdims=True)
    m_sc[...] = jnp.full(m_sc.shape, NEG_INF, F32)
    acc_sc[...] = jnp.zeros(acc_sc.shape, F32)

    def body(j, carry):
        consume(j & 1, jnp.where(j == 0, n_full, j - 1))
        s_new = scores(j)
        slot_new = (j + 1) & 1
        s_sc[slot_new] = s_new
        mcur_sc[slot_new] = jnp.max(s_new, axis=-1, keepdims=True)
        return carry

    lax.fori_loop(0, n_full, body, 0)
    consume(n_full & 1, jnp.maximum(n_full - 1, 0))

    base = pl.multiple_of(jnp.maximum(start - WINDOW, 0), Q_TILE)
    kw = kw_ref[pl.ds(base, WINDOW + Q_TILE), :]
    vw = vw_ref[pl.ds(base, WINDOW + Q_TILE), :]
    dpos = qpos - (base + lax.broadcasted_iota(jnp.int32, (1, WINDOW + Q_TILE), 1))
    wbias = jnp.where((dpos >= 0) & (dpos < WINDOW), 0.0, NEG_INF)

    gate = jax.nn.sigmoid(gat_ref[...])
    for r in range(REP):
        rows = head_rows[r]
        s = _dot_nt(q_ref[r], kw) + wbias
        e = jnp.exp2((s - jnp.max(s, axis=-1, keepdims=True)).astype(BF16))
        ow = _dot(e, vw)
        o_win = ow[:, :HD] / jnp.maximum(ow[:, HD:HD + 1], 1e-30)
        acc = acc_sc[rows]
        o_slc = acc[:, :HD] / jnp.maximum(acc[:, HD:HD + 1], 1e-30)
        o_ref[:, HD * r:HD * (r + 1)] = (gate[:, r:r + 1] * o_cmp[r] + gate[:, REP + r:REP + r + 1] * o_slc
                                        + gate[:, 2 * REP + r:2 * REP + r + 1] * o_win)


def _nsa_prompt_call(qh, gat, cmp, kaug, vaug, kwin, vwaug, mmat, nb, seq, cmp_off):
    nq = seq // Q_TILE
    nc = cmp.shape[2]
    ns = seq // SEL_BLOCK
    assert ns <= LANES and mmat.shape[1] == LANES
    kern = functools.partial(_nsa_prompt_kernel, ns=ns, n_sel=min(N_SEL, ns))

    def kv_spec(width):
        return pl.BlockSpec((None, seq, width), lambda b, g, i: (g, b, 0))

    return pl.pallas_call(
        kern,
        grid=(nb, N_KV, nq),
        in_specs=[pl.BlockSpec((REP, Q_TILE, HD), lambda b, g, i: (g, b * nq + i, 0)),
                  pl.BlockSpec((Q_TILE, LANES), lambda b, g, i: (b * nq + i, g)),
                  pl.BlockSpec((None, None, nc, HD), lambda b, g, i: (cmp_off + b, g, 0, 0)),
                  pl.BlockSpec((None, None, nc, HD), lambda b, g, i: (cmp_off + b, 2 + g, 0, 0)),
                  kv_spec(LANES + HD), kv_spec(LANES), kv_spec(HD), kv_spec(LANES),
                  pl.BlockSpec(mmat.shape, lambda b, g, i: (0, 0))],
        out_specs=pl.BlockSpec((Q_TILE, REP * HD), lambda b, g, i: (b * nq + i, g)),
        out_shape=jax.ShapeDtypeStruct((nb * seq, D_NSA), F32),
        scratch_shapes=[pltpu.VMEM((REP * Q_TILE, LANES + HD), BF16),
                        pltpu.VMEM((2, REP * Q_TILE, KV_TILE), F32),
                        pltpu.VMEM((2, REP * Q_TILE, 1), F32),
                        pltpu.VMEM((REP * Q_TILE, 1), F32),
                        pltpu.VMEM((REP * Q_TILE, LANES), F32)],
        compiler_params=_cparams("parallel", "parallel", "arbitrary"),
        name="nsa_prompt",
    )(qh, gat, cmp, cmp, kaug, vaug, kwin, vwaug, mmat)


def _nsa_sample_kernel(q_ref, gat_ref, cmp_ref, ks_ref, new_ref, win_ref, mmat_ref, oht_ref, o_ref,
                       *, past, ts, ns, n_sel):
    nc = cmp_ref.shape[1]
    wb = win_ref.shape[1]
    t_idx = lax.broadcasted_iota(jnp.int32, (ts, 1), 0)
    qpos = jnp.concatenate([past + t_idx] * REP, axis=0)
    gate = jax.nn.sigmoid(gat_ref[...])
    pad_rows = jnp.zeros((LANES - ts, HD), F32)

    def padded(slab):
        return jnp.concatenate([new_ref[slab], pad_rows], axis=0).astype(BF16)

    c_end = lax.broadcasted_iota(jnp.int32, (1, nc), 1) * CMP_STRIDE + (2 * CMP_STRIDE - 1)
    qs, o_cmps, psums = [], [], []
    for g in range(N_KV):
        q = jnp.concatenate([q_ref[REP * g + r] for r in range(REP)], axis=0).astype(BF16)
        p = _masked_softmax(_dot_nt(q, cmp_ref[g].astype(BF16)), c_end <= qpos)
        o_cmps.append(_dot(p.astype(BF16), cmp_ref[2 + g].astype(BF16)))
        psum = p[0:ts]
        for r in range(1, REP):
            psum = psum + p[r * ts:(r + 1) * ts]
        qs.append(q)
        psums.append(psum)
    psum_all = jnp.concatenate(psums + [jnp.zeros((LANES - N_KV * ts, nc), F32)], axis=0)
    row = lax.broadcasted_iota(jnp.int32, (LANES, 1), 0)
    sel_all = _selection(psum_all, mmat_ref[...], past + lax.rem(row, ts), ns, n_sel)

    kidx = lax.broadcasted_iota(jnp.int32, (1, LANES), 1)
    kpos_w = past - wb + lax.broadcasted_iota(jnp.int32, (1, wb + LANES), 1)
    dpos_w = qpos - kpos_w
    wmask = (dpos_w >= 0) & (dpos_w < WINDOW) & (kpos_w >= 0)
    for g in range(N_KV):
        q = qs[g]
        sel = jnp.concatenate([sel_all[g * ts:(g + 1) * ts]] * REP, axis=0)
        selneg = jnp.where(sel[:, :LANES] > 0.5, 0.0, NEG_INF).astype(BF16)
        s_old = _dot(q, ks_ref[HD * g:HD * (g + 1), :]) + _dot(selneg, oht_ref[...])
        new_bias = jnp.where((sel[:, ns - 1:ns] > 0.5) & (past + kidx <= qpos), 0.0, NEG_INF)
        s_new = _dot_nt(q, padded(g)) + new_bias
        m = jnp.maximum(jnp.max(s_old, axis=-1, keepdims=True), jnp.max(s_new, axis=-1, keepdims=True))
        e_old = jnp.exp(s_old - m)
        e_new = jnp.exp(s_new - m)
        denom = jnp.sum(e_old, axis=-1, keepdims=True) + jnp.sum(e_new, axis=-1, keepdims=True)
        o_slc = (_dot_nt(e_old.astype(BF16), ks_ref[128 + HD * g:128 + HD * (g + 1), :])
                 + _dot(e_new.astype(BF16), padded(2 + g))) / jnp.maximum(denom, 1e-30)

        s_w = jnp.concatenate([_dot(q, win_ref[HD * g:HD * (g + 1), :].astype(BF16)),
                               _dot_nt(q, padded(4 + g))], axis=1)
        pw = _masked_softmax(s_w, wmask)
        o_win = (_dot_nt(pw[:, :wb].astype(BF16), win_ref[128 + HD * g:128 + HD * (g + 1), :].astype(BF16))
                 + _dot(pw[:, wb:].astype(BF16), padded(6 + g)))

        for r in range(REP):
            rs = slice(r * ts, (r + 1) * ts)
            c = LANES * g + r
            o_ref[:, HD * (REP * g + r):HD * (REP * g + r + 1)] = (
                gate[:, c:c + 1] * o_cmps[g][rs] + gate[:, c + REP:c + REP + 1] * o_slc[rs]
                + gate[:, c + 2 * REP:c + 2 * REP + 1] * o_win[rs])


def _nsa_sample_call(qh, gat, cmp, ks_t, kvh_new, win_t, mmat, oht, nb, ts, past, cmp_off):
    nc = cmp.shape[2]
    ns = past // SEL_BLOCK + 1
    assert ns - 1 <= LANES and N_KV * ts <= LANES
    kern = functools.partial(_nsa_sample_kernel, past=past, ts=ts, ns=ns, n_sel=min(N_SEL, ns))
    return pl.pallas_call(
        kern,
        grid=(nb,),
        in_specs=[pl.BlockSpec((8, ts, HD), lambda b: (0, b, 0)),
                  pl.BlockSpec((ts, 256), lambda b: (b, 0)),
                  pl.BlockSpec((None, 4, nc, HD), lambda b: (cmp_off + b, 0, 0, 0)),
                  pl.BlockSpec((None, 256, past), lambda b: (b, 0, 0)),
                  pl.BlockSpec((8, ts, HD), lambda b: (0, b, 0)),
                  pl.BlockSpec((None, 256, win_t.shape[2]), lambda b: (b, 0, 0)),
                  pl.BlockSpec(mmat.shape, lambda b: (0, 0)),
                  pl.BlockSpec(oht.shape, lambda b: (0, 0))],
        out_specs=pl.BlockSpec((ts, D_NSA), lambda b: (b, 0)),
        out_shape=jax.ShapeDtypeStruct((nb * ts, D_NSA), F32),
        compiler_params=_cparams("parallel"),
        name="nsa_sample",
    )(qh, gat, cmp, ks_t, kvh_new, win_t, mmat, oht)


def _ret_kernel(x_ref, s0_ref, dmat_ref, xi_ref, zeta_ref, gc_ref, gro_ref, o_ref, s_out_ref, s_sc, pad_sc,
                *, rows):
    @pl.when(pl.program_id(1) == 0)
    def _():
        s_sc[...] = s0_ref[...]

    if rows < RET_CHUNK:
        pad_sc[...] = jnp.zeros(pad_sc.shape, F32)
        pad_sc[0:rows, :] = x_ref[...]
        x = pad_sc
    else:
        x = x_ref
    for c in range(max(rows // RET_CHUNK, 1)):
        r0 = RET_CHUNK * c
        for h in range(N_RET):
            q = x[r0:r0 + RET_CHUNK, DK_RET * h:DK_RET * (h + 1)].astype(BF16)
            k = x[r0:r0 + RET_CHUNK, 256 + DK_RET * h:256 + DK_RET * (h + 1)]
            v = x[r0:r0 + RET_CHUNK, 512 + DV_RET * h:512 + DV_RET * (h + 1)].astype(BF16)
            gr = x[r0:r0 + RET_CHUNK, 1024 + DV_RET * h:1024 + DV_RET * (h + 1)]
            att = _dot_nt(q, k.astype(BF16)) * dmat_ref[h]
            s_old = s_sc[h]
            o = _dot(att.astype(BF16), v) + _dot(q, s_old.astype(BF16)) * xi_ref[h]
            s_sc[h] = gc_ref[h] * s_old + _dot_tn((k * zeta_ref[h]).astype(BF16), v)
            mu = jnp.mean(o, axis=-1, keepdims=True)
            var = jnp.mean(jnp.square(o - mu), axis=-1, keepdims=True)
            y = (o - mu) * lax.rsqrt(var + EPS) * gro_ref[:, DV_RET * h:DV_RET * (h + 1)]
            res = gr * jax.nn.sigmoid(gr) * y
            n_out = min(rows, RET_CHUNK)
            o_ref[r0:r0 + n_out, DV_RET * h:DV_RET * (h + 1)] = res[0:n_out]
    s_out_ref[...] = s_sc[...]


def _ret_call(ret, s0, tabs, g_ret_out, nb, seq, rows):
    nchunks = seq // rows
    dmat, xi, zeta, gc = tabs
    kern = functools.partial(_ret_kernel, rows=rows)
    full3 = lambda b, c: (0, 0, 0)
    return pl.pallas_call(
        kern,
        grid=(nb, nchunks),
        in_specs=[pl.BlockSpec((rows, 1536), lambda b, c: (b * nchunks + c, 0)),
                  pl.BlockSpec((None, N_RET, DK_RET, DV_RET), lambda b, c: (b, 0, 0, 0)),
                  pl.BlockSpec(dmat.shape, full3), pl.BlockSpec(xi.shape, full3),
                  pl.BlockSpec(zeta.shape, full3), pl.BlockSpec(gc.shape, full3),
                  pl.BlockSpec((1, D_RET), lambda b, c: (0, 0))],
        out_specs=[pl.BlockSpec((rows, D_RET), lambda b, c: (b * nchunks + c, 0)),
                   pl.BlockSpec((None, N_RET, DK_RET, DV_RET), lambda b, c: (b, 0, 0, 0))],
        out_shape=[jax.ShapeDtypeStruct((nb * seq, D_RET), F32),
                   jax.ShapeDtypeStruct((nb, N_RET, DK_RET, DV_RET), F32)],
        scratch_shapes=[pltpu.VMEM((N_RET, DK_RET, DV_RET), F32),
                        pltpu.VMEM((RET_CHUNK, 1536), F32)],
        compiler_params=_cparams("parallel", "arbitrary"),
        name="retention",
    )(ret, s0, dmat, xi, zeta, gc, g_ret_out)


def _ret_tables(chunk):
    c = RET_CHUNK
    log_g = jnp.log1p(-jnp.power(2.0, -5.0 - jnp.arange(N_RET, dtype=F32)))
    i = jnp.arange(c, dtype=F32)
    diff = i[:, None] - i[None, :]
    dmat = jnp.where(diff >= 0, jnp.exp(jnp.maximum(diff, 0.0)[None] * log_g[:, None, None]), 0.0)
    xi = jnp.exp((i[None, :] + 1.0) * log_g[:, None])
    zeta = jnp.where(i[None, :] < chunk, jnp.exp((chunk - 1.0 - i)[None, :] * log_g[:, None]), 0.0)
    g_c = jnp.exp(chunk * log_g)
    return (dmat,
            jnp.broadcast_to(xi[:, :, None], (N_RET, c, DV_RET)),
            jnp.broadcast_to(zeta[:, :, None], (N_RET, c, DK_RET)),
            jnp.broadcast_to(g_c[:, None, None], (N_RET, DK_RET, DV_RET)))


def _outproj_kernel(on_ref, or_ref, x_ref, gt_ref, sc_ref, sh_ref, gn_ref, g2_ref, w_ref, x1_ref, h2_ref):
    a = _rms(on_ref[...], gn_ref[...])
    mix = _dot(a.astype(BF16), w_ref[0:D_NSA, :]) + _dot(or_ref[...].astype(BF16), w_ref[D_NSA:, :])
    x1 = x_ref[...] + gt_ref[...] * mix
    x1_ref[...] = x1
    h2_ref[...] = (_rms(x1, g2_ref[...]) * (1.0 + sc_ref[...]) + sh_ref[...]).astype(h2_ref.dtype)


def _outproj_call(o_nsa, o_ret, x2d, gt3, sc3, sh3, mod_idx, g_nsa, g2, w_out, tm, h_dtype):
    m = x2d.shape[0]
    mrows = gt3.shape[1]
    mod_spec = pl.BlockSpec((None, mrows, D_MODEL), lambda i: (mod_idx(i), 0, 0))
    return pl.pallas_call(
        _outproj_kernel,
        grid=(m // tm,),
        in_specs=[pl.BlockSpec((tm, D_NSA), lambda i: (i, 0)),
                  pl.BlockSpec((tm, D_RET), lambda i: (i, 0)),
                  pl.BlockSpec((tm, D_MODEL), lambda i: (i, 0)),
                  mod_spec, mod_spec, mod_spec,
                  pl.BlockSpec((1, D_NSA), lambda i: (0, 0)),
                  pl.BlockSpec((1, D_MODEL), lambda i: (0, 0)),
                  pl.BlockSpec((D_NSA + D_RET, D_MODEL), lambda i: (0, 0))],
        out_specs=[pl.BlockSpec((tm, D_MODEL), lambda i: (i, 0)),
                   pl.BlockSpec((tm, D_MODEL), lambda i: (i, 0))],
        out_shape=[jax.ShapeDtypeStruct((m, D_MODEL), F32),
                   jax.ShapeDtypeStruct((m, D_MODEL), h_dtype)],
        compiler_params=_cparams("parallel"),
        name="out_proj",
    )(o_nsa, o_ret, x2d, gt3, sc3, sh3, g_nsa, g2, w_out)


FF_TILE = D_FF // 2


def _ffn_up_kernel(h_ref, wa_ref, wb_ref, cwa_ref, cwb_ref, cba_ref, cbb_ref, sta_ref, stb_ref,
                   act_ref, csa_ref, csb_ref, prev_a, prev_b):
    @pl.when(pl.program_id(2) == 0)
    def _():
        prev_a[...] = sta_ref[...]
        prev_b[...] = stb_ref[...]

    h = h_ref[...].astype(BF16)
    tm = h.shape[0]
    row = lax.broadcasted_iota(jnp.int32, (tm, 1), 0)

    def half(w_ref, cw_ref, cb_ref, prev, cs_ref):
        u = _dot(h, w_ref[...])
        p = prev[...]
        back1 = pltpu.roll(u, 1, 0)
        back2 = pltpu.roll(u, 2, 0)
        u1 = jnp.where(row >= 1, back1, p[1:2])
        u2 = jnp.where(row >= 2, back2, jnp.where(row == 1, p[1:2], p[0:1]))
        y = cb_ref[...] + cw_ref[0:1] * u2 + cw_ref[1:2] * u1 + cw_ref[2:3] * u
        tail = back2[0:2]
        prev[...] = tail
        cs_ref[...] = tail
        return y

    a = half(wa_ref, cwa_ref, cba_ref, prev_a, csa_ref)
    b = half(wb_ref, cwb_ref, cbb_ref, prev_b, csb_ref)
    act_ref[...] = (a * jax.nn.sigmoid(a) * b).astype(act_ref.dtype)


def _ffn_up_call(h2, w_up_a, w_up_b, conv_w, conv_b, conv_state, nb, seq, tm, act_dtype):
    nrt = seq // tm
    nt = D_FF // FF_TILE
    return pl.pallas_call(
        _ffn_up_kernel,
        grid=(nt, nb, nrt),
        in_specs=[pl.BlockSpec((tm, D_MODEL), lambda j, b, i: (b * nrt + i, 0)),
                  pl.BlockSpec((D_MODEL, FF_TILE), lambda j, b, i: (0, j)),
                  pl.BlockSpec((D_MODEL, FF_TILE), lambda j, b, i: (0, j)),
                  pl.BlockSpec((3, FF_TILE), lambda j, b, i: (0, j)),
                  pl.BlockSpec((3, FF_TILE), lambda j, b, i: (0, nt + j)),
                  pl.BlockSpec((1, FF_TILE), lambda j, b, i: (0, j)),
                  pl.BlockSpec((1, FF_TILE), lambda j, b, i: (0, nt + j)),
                  pl.BlockSpec((None, 2, FF_TILE), lambda j, b, i: (b, 0, j)),
                  pl.BlockSpec((None, 2, FF_TILE), lambda j, b, i: (b, 0, nt + j))],
        out_specs=[pl.BlockSpec((tm, FF_TILE), lambda j, b, i: (b * nrt + i, j)),
                   pl.BlockSpec((None, 2, FF_TILE), lambda j, b, i: (b, 0, j)),
                   pl.BlockSpec((None, 2, FF_TILE), lambda j, b, i: (b, 0, j))],
        out_shape=[jax.ShapeDtypeStruct((nb * seq, D_FF), act_dtype),
                   jax.ShapeDtypeStruct((nb, 2, D_FF), F32),
                   jax.ShapeDtypeStruct((nb, 2, D_FF), F32)],
        scratch_shapes=[pltpu.VMEM((2, FF_TILE), F32), pltpu.VMEM((2, FF_TILE), F32)],
        compiler_params=_cparams("parallel", "parallel", "arbitrary"),
        name="ffn_up_conv",
    )(h2, w_up_a, w_up_b, conv_w, conv_w, conv_b, conv_b, conv_state, conv_state)


def _ffn_down_kernel(a_ref, x1_ref, gt_ref, w_ref, gf_ref, y_ref, *, final_norm):
    x2 = x1_ref[...] + gt_ref[...] * _dot(a_ref[...].astype(BF16), w_ref[...])
    y_ref[...] = _rms(x2, gf_ref[...]) if final_norm else x2


def _ffn_down_call(act, x1, gt3, mod_idx, w_down, g_final, tm, final_norm):
    m = x1.shape[0]
    mrows = gt3.shape[1]
    return pl.pallas_call(
        functools.partial(_ffn_down_kernel, final_norm=final_norm),
        grid=(m // tm,),
        in_specs=[pl.BlockSpec((tm, D_FF), lambda i: (i, 0)),
                  pl.BlockSpec((tm, D_MODEL), lambda i: (i, 0)),
                  pl.BlockSpec((None, mrows, D_MODEL), lambda i: (mod_idx(i), 0, 0)),
                  pl.BlockSpec((D_FF, D_MODEL), lambda i: (0, 0)),
                  pl.BlockSpec((1, D_MODEL), lambda i: (0, 0))],
        out_specs=pl.BlockSpec((tm, D_MODEL), lambda i: (i, 0)),
        out_shape=jax.ShapeDtypeStruct((m, D_MODEL), F32),
        compiler_params=_cparams("parallel"),
        name="ffn_down",
    )(act, x1, gt3, w_down, g_final)


def _relayout_w_in(w_in):
    q_n, kv_n, gate_n, q_r, k_r, v_r, g_r = jnp.split(w_in, [512, 1280, 1304, 1560, 1816, 2328], axis=1)
    gate = gate_n.reshape(D_MODEL, 3, N_KV, REP)
    gate = jnp.transpose(gate, (0, 2, 1, 3)).reshape(D_MODEL, N_KV, 3 * REP)
    gate = jnp.pad(gate, ((0, 0), (0, 0), (0, LANES - 3 * REP))).reshape(D_MODEL, N_KV * LANES)
    return jnp.concatenate([q_n, kv_n, q_r, k_r, v_r, g_r, gate], axis=1).astype(BF16)


def _relayout_cmp(w1, pos):
    w1r = w1.reshape(2, CMP_STRIDE, HD, HD)
    w = jnp.einsum('atdn,gh->tgdahn', w1r, jnp.eye(N_KV, dtype=w1.dtype)).reshape(CMP_STRIDE * N_KV * HD, 2 * N_KV * HD)
    posr = pos.reshape(2, CMP_STRIDE, 1, HD)
    prow = jnp.broadcast_to(posr, (2, CMP_STRIDE, N_KV, HD)).reshape(2, CMP_STRIDE * N_KV * HD)
    prow = jnp.pad(prow, ((0, 14), (0, 0)))
    return w.astype(BF16), prow.astype(BF16)


def _block_diag4(w2k, w2v):
    z = jnp.zeros((HD, HD), w2k.dtype)
    rows = [[w2k, z, z, z], [z, w2k, z, z], [z, z, w2v, z], [z, z, z, w2v]]
    return jnp.block(rows).astype(BF16)


def _rope_tables(pos):
    half = HD // 2
    inv = ROPE_THETA ** (-jnp.arange(half, dtype=F32) / half)
    ang = pos.astype(F32)[:, None] * inv[None, :]
    cos, sin = jnp.cos(ang), jnp.sin(ang)
    return jnp.tile(cos, (1, 4)), jnp.tile(jnp.concatenate([-sin, sin], axis=1), (1, 2))


def _importance_matrix(nc, nsp):
    n = np.arange(nc)[:, None]
    d = n - 4 * np.arange(nsp)[None, :]
    m = ((d >= 0) & (d <= 3)).astype(np.float32) + ((d >= -1) & (d <= 2)).astype(np.float32)
    return jnp.asarray(m, dtype=BF16)


def _block_onehot_t(past):
    blk = np.arange(LANES)[:, None]
    return jnp.asarray((blk == np.arange(past)[None, :] // SEL_BLOCK).astype(np.float32), dtype=BF16)


def _round_up(x, m):
    return (x + m - 1) // m * m


def kernel(x_prompt, x_sample, c_prompt, c_sample, cache_nsa_kv, cache_win_kv, state_ret, state_conv, page_table,
           w_ada, b_ada, g_norm1, w_in, cmp_pos_k, cmp_w1_k, cmp_w2_k, cmp_pos_v, cmp_w1_v, cmp_w2_v,
           g_nsa_out, g_ret_out, w_out, g_norm2, w_up, conv_w, conv_b, w_down, g_final):
    nb_p, seq, _ = x_prompt.shape
    nb_s, ts, _ = x_sample.shape
    depth = w_ada.shape[0]
    n_pages = page_table.shape[1]
    past = n_pages * PAGE
    wb = cache_win_kv.shape[2]
    m_p, m_s = nb_p * seq, nb_s * ts
    tm_p = 512

    xp = x_prompt.reshape(m_p, D_MODEL)
    xs = x_sample.reshape(m_s, D_MODEL)
    n_c = nb_p + nb_s
    c_all = jnp.pad(jnp.concatenate([c_prompt, c_sample], axis=0), ((0, _round_up(n_c, 8) - n_c), (0, 0)))

    cos_p, sin_p = _rope_tables(jnp.arange(seq, dtype=jnp.int32))
    cos_s, sin_s = _rope_tables(jnp.tile(past + jnp.arange(ts, dtype=jnp.int32), nb_s))
    tabs_p = _ret_tables(min(RET_CHUNK, seq))
    tabs_s = _ret_tables(ts)
    nc_p, nc_s = seq // CMP_STRIDE, past // CMP_STRIDE
    nsp_p = _round_up(seq // SEL_BLOCK, LANES)
    nsp_s = _round_up(past // SEL_BLOCK + 1, LANES)
    mmat_p = _importance_matrix(nc_p, nsp_p)
    mmat_s, oht_s = _importance_matrix(nc_s, nsp_s), _block_onehot_t(past)

    tiles_per_batch = seq // tm_p
    idx_p = lambda i: i // tiles_per_batch
    tab_p = lambda i: i % tiles_per_batch
    idx_s = lambda i: 0

    outs = {k: [] for k in ('kv_p', 'kv_s', 'win_p', 'win_s', 'ret_p', 'ret_s', 'conv_p', 'conv_s')}
    for l in range(depth):
        mod = _mod_call(c_all, w_ada[l], b_ada[l])
        mods_p = [a.reshape(nb_p, 1, D_MODEL) for a in jnp.split(mod[:nb_p], 6, axis=1)]
        mods_s = [jnp.repeat(a, ts, axis=0).reshape(1, m_s, D_MODEL) for a in jnp.split(mod[nb_p:n_c], 6, axis=1)]
        w_in_re = _relayout_w_in(w_in[l])
        wk, pos_k = _relayout_cmp(cmp_w1_k[l], cmp_pos_k[l])
        wv, pos_v = _relayout_cmp(cmp_w1_v[l], cmp_pos_v[l])
        pos_kv = jnp.stack([pos_k, pos_v])
        w2bd = _block_diag4(cmp_w2_k[l], cmp_w2_v[l])
        g1 = g_norm1[l].reshape(1, -1)
        g2 = g_norm2[l].reshape(1, -1)
        g_nsa = g_nsa_out[l].reshape(1, -1)
        g_ret = g_ret_out[l].reshape(1, -1)
        w_out_bf = w_out[l].astype(BF16)
        w_up_a = w_up[l][:, :D_FF].astype(BF16)
        w_up_b = w_up[l][:, D_FF:].astype(BF16)
        w_down_bf = w_down[l].astype(BF16)

        sh1, sc1, gt1, sh2, sc2, gt2 = mods_p
        qh_p, kvn_t, kvw_t, kaug_p, vaug_p, kwin_p, vwaug_p, ret_p, gat_p, kvc_p = _inproj_call(
            xp, sc1, sh1, idx_p, g1, w_in_re, cos_p, sin_p, tab_p, tm_p, HD ** -0.5 * LOG2_E, tiles_per_batch)
        sh1s, sc1s, gt1s, sh2s, sc2s, gt2s = mods_s
        qh_s, kvn_s, kvw_s, kvh_s, ret_s, gat_s = _inproj_call(
            xs, sc1s, sh1s, idx_s, g1, w_in_re, cos_s, sin_s, idx_s, m_s, HD ** -0.5)

        ab_p = _cmp_prompt_call(kvc_p, wk, wv, pos_kv).reshape(nb_p, nc_p, 512)
        cache_t = jnp.transpose(cache_nsa_kv[:, l], (0, 2, 3, 4, 1)).reshape(cache_nsa_kv.shape[0], 512, PAGE)
        win_t = jnp.transpose(cache_win_kv[:, l], (0, 2, 3, 4, 1)).reshape(nb_s, 256, wb)
        ab_s, ks_s = _cmp_sample_call(cache_t, page_table, wk, wv, pos_kv)
        cmp_p = _cmp_fin_call(ab_p, w2bd)
        cmp_s = _cmp_fin_call(ab_s, w2bd)

        o_nsa_p = _nsa_prompt_call(qh_p, gat_p, cmp_p, kaug_p, vaug_p, kwin_p, vwaug_p, mmat_p, nb_p, seq, 0)
        o_nsa_s = _nsa_sample_call(qh_s, gat_s, cmp_s, ks_s, kvh_s, win_t, mmat_s, oht_s, nb_s, ts, past, 0)

        s0_p = jnp.zeros((nb_p, N_RET, DK_RET, DV_RET), F32)
        o_ret_p, s_new_p = _ret_call(ret_p, s0_p, tabs_p, g_ret, nb_p, seq, min(4 * RET_CHUNK, seq))
        o_ret_s, s_new_s = _ret_call(ret_s, state_ret[:, l], tabs_s, g_ret, nb_s, ts, ts)

        x1_p, h2_p = _outproj_call(o_nsa_p, o_ret_p, xp, gt1, sc2, sh2, idx_p, g_nsa, g2, w_out_bf, tm_p, BF16)
        x1_s, h2_s = _outproj_call(o_nsa_s, o_ret_s, xs, gt1s, sc2s, sh2s, idx_s, g_nsa, g2, w_out_bf, m_s, F32)

        conv0_p = jnp.zeros((nb_p, 2, 2 * D_FF), F32)
        act_p, csa_p, csb_p = _ffn_up_call(h2_p, w_up_a, w_up_b, conv_w[l], conv_b[l].reshape(1, -1), conv0_p,
                                           nb_p, seq, tm_p, BF16)
        act_s, csa_s, csb_s = _ffn_up_call(h2_s, w_up_a, w_up_b, conv_w[l], conv_b[l].reshape(1, -1),
                                           state_conv[:, l], nb_s, ts, ts, F32)
        last = l == depth - 1
        gf = g_final.reshape(1, -1)
        xp = _ffn_down_call(act_p, x1_p, gt2, idx_p, w_down_bf, gf, tm_p, last)
        xs = _ffn_down_call(act_s, x1_s, gt2s, idx_s, w_down_bf, gf, m_s, last)

        outs['kv_p'].append(jnp.transpose(kvn_t.reshape(nb_p, 4, N_KV, HD, seq), (0, 4, 1, 2, 3)))
        outs['kv_s'].append(kvn_s.reshape(nb_s, ts, 4, N_KV, HD))
        keep_p = min(WINDOW, seq)
        win_t_p = kvw_t[:, :, seq - keep_p:].reshape(nb_p, 2, N_KV, HD, keep_p)
        outs['win_p'].append(jnp.transpose(win_t_p, (0, 4, 1, 2, 3)))
        win_all = jnp.concatenate([cache_win_kv[:, l], kvw_s.reshape(nb_s, ts, 2, N_KV, HD)], axis=1)
        outs['win_s'].append(win_all[:, win_all.shape[1] - min(WINDOW, past + ts):])
        outs['ret_p'].append(s_new_p)
        outs['ret_s'].append(s_new_s)
        outs['conv_p'].append(jnp.concatenate([csa_p, csb_p], axis=-1))
        outs['conv_s'].append(jnp.concatenate([csa_s, csb_s], axis=-1))

    st = lambda k: jnp.stack(outs[k], axis=1)
    return (xp.reshape(nb_p, seq, D_MODEL), xs.reshape(nb_s, ts, D_MODEL),
            st('kv_p'), st('kv_s'), st('win_p'), st('win_s'), st('ret_p'), st('ret_s'), st('conv_p'), st('conv_s'))
```

```python
import functools

import numpy as np
import jax
import jax.numpy as jnp
from jax import lax
from jax.experimental import pallas as pl
from jax.experimental.pallas import tpu as pltpu

F32 = jnp.float32
BF16 = jnp.bfloat16

D_MODEL = 1024
PAGE = 128
HD = 64
N_KV = 2
REP = 4
D_NSA = 512
CMP_STRIDE = 16
SEL_BLOCK = 64
N_SEL = 16
WINDOW = 512
Q_TILE = 512
KV_TILE = 1024
FORCE_BONUS = 100.0
N_RET = 4
DK_RET = 64
DV_RET = 128
D_RET = 512
RET_CHUNK = 128
D_FF = 2816
ROPE_THETA = 10000.0
EPS = 1e-6
NEG_INF = -1e30
LOG2_E = 1.4426950408889634
LANES = 128
W_IN_COLS = 3072
VMEM_LIMIT = 56 * 1024 * 1024


def _cparams(*sem):
    return pltpu.CompilerParams(dimension_semantics=sem, vmem_limit_bytes=VMEM_LIMIT)


def _dot(a, b):
    return jnp.dot(a, b, preferred_element_type=F32)


def _dot_nt(a, b):
    return lax.dot_general(a, b, (((1,), (1,)), ((), ())), preferred_element_type=F32)


def _dot_tn(a, b):
    return lax.dot_general(a, b, (((0,), (0,)), ((), ())), preferred_element_type=F32)


def _dot_split3(p, m_bf16):
    hi = p.astype(BF16)
    r1 = p - hi.astype(F32)
    mid = r1.astype(BF16)
    lo = (r1 - mid.astype(F32)).astype(BF16)
    return _dot(hi, m_bf16) + _dot(mid, m_bf16) + _dot(lo, m_bf16)


def _rms(x, g):
    return x * lax.rsqrt(jnp.mean(x * x, axis=-1, keepdims=True) + EPS) * g


def _masked_softmax(s, mask):
    s = jnp.where(mask, s, NEG_INF)
    m = jnp.max(s, axis=-1, keepdims=True)
    e = jnp.where(mask, jnp.exp(s - m), 0.0)
    return e / jnp.maximum(jnp.sum(e, axis=-1, keepdims=True), 1e-30)


def _topk_mask_t(score_t, n_sel):
    nb = score_t.shape[0]
    blk = lax.broadcasted_iota(jnp.int32, score_t.shape, 0)

    def body(_, work):
        m = jnp.max(work, axis=0, keepdims=True)
        idx = jnp.min(jnp.where(work == m, blk, nb), axis=0, keepdims=True)
        return jnp.where(blk == idx, -jnp.inf, work)

    work = lax.fori_loop(0, n_sel, body, score_t)
    return jnp.where(work == -jnp.inf, jnp.where(score_t == -jnp.inf, 0.0, 1.0), 0.0)


def _selection(psum, mmat, qpos, ns, n_sel):
    imp = _dot_split3(psum, mmat)
    blk = lax.broadcasted_iota(jnp.int32, (1, imp.shape[1]), 1)
    qblk = qpos >> 6
    valid = (blk * SEL_BLOCK <= qpos) & (blk < ns)
    forced = (blk == 0) | (blk == qblk) | (blk == qblk - 1)
    score = jnp.where(valid, imp + jnp.where(forced, FORCE_BONUS, 0.0), NEG_INF)
    score = jnp.where(blk < ns, score, -jnp.inf)
    sel = _topk_mask_t(score.T, n_sel).T
    return jnp.where(valid, sel, 0.0)


def _mod_kernel(c_ref, w_ref, b_ref, o_ref):
    c = c_ref[...]
    o_ref[...] = _dot(c * jax.nn.sigmoid(c), w_ref[...]) + b_ref[...]


def _mod_call(c_all, w_ada, b_ada):
    n = c_all.shape[0]
    tn = 1536
    return pl.pallas_call(
        _mod_kernel,
        grid=(w_ada.shape[1] // tn,),
        in_specs=[pl.BlockSpec((n, D_MODEL), lambda j: (0, 0)),
                  pl.BlockSpec((D_MODEL, tn), lambda j: (0, j)),
                  pl.BlockSpec((1, tn), lambda j: (0, j))],
        out_specs=pl.BlockSpec((n, tn), lambda j: (0, j)),
        out_shape=jax.ShapeDtypeStruct((n, w_ada.shape[1]), F32),
        compiler_params=_cparams("arbitrary"),
        name="adaln_mod",
    )(c_all, w_ada, b_ada.reshape(1, -1))


def _inproj_kernel(x_ref, sc_ref, sh_ref, g1_ref, w_ref, cos_ref, sin_ref, qh_ref, kvn_ref, kvw_ref, *rest,
                   q_scale, tiles_per_batch):
    prompt = tiles_per_batch is not None
    if prompt:
        kaug_ref, vaug_ref, kwin_ref, vwaug_ref, ret_ref, gat_ref, kvc_ref = rest
        tm = x_ref.shape[0]
        pos = ((pl.program_id(0) % tiles_per_batch) * tm + lax.broadcasted_iota(jnp.int32, (tm, 1), 0))
        blk_onehot = jnp.where(lax.broadcasted_iota(jnp.int32, (1, LANES), 1) == (pos >> (SEL_BLOCK.bit_length() - 1)),
                               1.0, 0.0).astype(BF16)
        ones_cols = jnp.where(lax.broadcasted_iota(jnp.int32, (tm, HD), 1) == 0, 1.0, 0.0).astype(BF16)
    else:
        kvh_ref, ret_ref, gat_ref = rest
    h = (_rms(x_ref[...], g1_ref[...]) * (1.0 + sc_ref[...]) + sh_ref[...]).astype(BF16)
    cos = cos_ref[...]
    sin = sin_ref[...]
    lane = lax.broadcasted_iota(jnp.int32, (1, LANES), 1)
    first_half = (lane & (HD - 1)) < HD // 2

    def seg(c0, n):
        return _dot(h, w_ref[:, c0:c0 + n])

    def rope(a):
        sw = jnp.where(first_half, pltpu.roll(a, LANES - HD // 2, 1), pltpu.roll(a, HD // 2, 1))
        return a * cos + sw * sin

    for j in range(4):
        q = rope(seg(LANES * j, LANES)) * q_scale
        qh_ref[2 * j] = q[:, :HD].astype(qh_ref.dtype)
        qh_ref[2 * j + 1] = q[:, HD:].astype(qh_ref.dtype)
    for s in range(6):
        a = seg(512 + LANES * s, LANES)
        if s % 2 == 0:
            a = rope(a)
        out_ref, s_out = (kvn_ref, s) if s < 4 else (kvw_ref, s - 4)
        if not prompt:
            out_ref[:, LANES * s_out:LANES * (s_out + 1)] = a
            if s >= 2:
                kvh_ref[2 * (s - 2)] = a[:, :HD].astype(kvh_ref.dtype)
                kvh_ref[2 * (s - 2) + 1] = a[:, HD:].astype(kvh_ref.dtype)
            continue
        out_ref[LANES * s_out:LANES * (s_out + 1), :] = a.T
        if s < 2:
            kvc_ref[s] = a
            continue
        for g in range(N_KV):
            part = a[:, HD * g:HD * (g + 1)].astype(BF16)
            if s == 2:
                kaug_ref[g] = jnp.concatenate([blk_onehot, part], axis=1)
            elif s == 3:
                vaug_ref[g] = jnp.concatenate([part, ones_cols], axis=1)
            elif s == 4:
                kwin_ref[g] = part
            else:
                vwaug_ref[g] = jnp.concatenate([part, ones_cols], axis=1)
    for j in range(2):
        ret_ref[:, LANES * j:LANES * (j + 1)] = rope(seg(1280 + LANES * j, LANES)) * (DK_RET ** -0.5)
        ret_ref[:, 256 + LANES * j:256 + LANES * (j + 1)] = rope(seg(1536 + LANES * j, LANES))
    ret_ref[:, 512:1024] = seg(1792, 512)
    ret_ref[:, 1024:1536] = seg(2304, 512)
    gat_ref[...] = seg(2816, 256)


def _inproj_call(x2d, sc3, sh3, mod_idx, g1, w_in_re, cos, sin, tab_idx, tm, q_scale, tiles_per_batch=None):
    m = x2d.shape[0]
    mrows = sc3.shape[1]
    rows = lambda width: pl.BlockSpec((tm, width), lambda i: (i, 0))
    slabs = lambda n, width: pl.BlockSpec((n, tm, width), lambda i: (0, i, 0))
    tail_specs = [rows(1536), rows(256)]
    tail_shapes = [jax.ShapeDtypeStruct((m, 1536), F32), jax.ShapeDtypeStruct((m, 256), F32)]
    if tiles_per_batch is not None:
        nb, seq = m // (tiles_per_batch * tm), tiles_per_batch * tm
        t_idx = lambda i: (i // tiles_per_batch, 0, i % tiles_per_batch)
        out_specs = ([slabs(8, HD), pl.BlockSpec((None, 512, tm), t_idx), pl.BlockSpec((None, 256, tm), t_idx),
                      slabs(N_KV, LANES + HD), slabs(N_KV, LANES), slabs(N_KV, HD), slabs(N_KV, LANES)]
                     + tail_specs + [slabs(2, LANES)])
        out_shape = ([jax.ShapeDtypeStruct((8, m, HD), BF16),
                      jax.ShapeDtypeStruct((nb, 512, seq), F32), jax.ShapeDtypeStruct((nb, 256, seq), F32),
                      jax.ShapeDtypeStruct((N_KV, m, LANES + HD), BF16), jax.ShapeDtypeStruct((N_KV, m, LANES), BF16),
                      jax.ShapeDtypeStruct((N_KV, m, HD), BF16), jax.ShapeDtypeStruct((N_KV, m, LANES), BF16)]
                     + tail_shapes + [jax.ShapeDtypeStruct((2, m, LANES), F32)])
    else:
        out_specs = [slabs(8, HD), rows(512), rows(256), slabs(8, HD)] + tail_specs
        out_shape = [jax.ShapeDtypeStruct((8, m, HD), F32), jax.ShapeDtypeStruct((m, 512), F32),
                     jax.ShapeDtypeStruct((m, 256), F32), jax.ShapeDtypeStruct((8, m, HD), F32)] + tail_shapes
    return pl.pallas_call(
        functools.partial(_inproj_kernel, q_scale=q_scale, tiles_per_batch=tiles_per_batch),
        grid=(m // tm,),
        in_specs=[pl.BlockSpec((tm, D_MODEL), lambda i: (i, 0)),
                  pl.BlockSpec((None, mrows, D_MODEL), lambda i: (mod_idx(i), 0, 0)),
                  pl.BlockSpec((None, mrows, D_MODEL), lambda i: (mod_idx(i), 0, 0)),
                  pl.BlockSpec((1, D_MODEL), lambda i: (0, 0)),
                  pl.BlockSpec((D_MODEL, W_IN_COLS), lambda i: (0, 0)),
                  pl.BlockSpec((tm, LANES), lambda i: (tab_idx(i), 0)),
                  pl.BlockSpec((tm, LANES), lambda i: (tab_idx(i), 0))],
        out_specs=out_specs,
        out_shape=out_shape,
        compiler_params=_cparams("parallel"),
        name="in_proj",
    )(x2d, sc3, sh3, g1, w_in_re, cos, sin)


def _cmp_partial(slab, wk_ref, wv_ref, pos_ref, ab_ref):
    for kind, w_ref in ((0, wk_ref), (1, wv_ref)):
        xcat = jnp.concatenate([slab(t, kind) for t in range(CMP_STRIDE)], axis=1).astype(BF16)
        r = _dot(xcat, w_ref[...])
        bias = _dot(pos_ref[kind], w_ref[...])
        ab_ref[:, LANES * kind:LANES * (kind + 1)] = r[:, :LANES] + bias[0:1, :LANES]
        ab_ref[:, 256 + LANES * kind:256 + LANES * (kind + 1)] = r[:, LANES:] + bias[1:2, LANES:]


def _cmp_prompt_kernel(x_ref, wk_ref, wv_ref, pos_ref, ab_ref):
    rows = ab_ref.shape[0]
    _cmp_partial(lambda t, kind: x_ref[kind, pl.ds(t, rows, stride=CMP_STRIDE), :],
                 wk_ref, wv_ref, pos_ref, ab_ref)


def _cmp_prompt_call(kvc, wk, wv, pos):
    rows = kvc.shape[1] // CMP_STRIDE
    tr = min(128, rows)
    return pl.pallas_call(
        _cmp_prompt_kernel,
        grid=(rows // tr,),
        in_specs=[pl.BlockSpec((2, tr * CMP_STRIDE, LANES), lambda i: (0, i, 0)),
                  pl.BlockSpec(wk.shape, lambda i: (0, 0)),
                  pl.BlockSpec(wv.shape, lambda i: (0, 0)),
                  pl.BlockSpec(pos.shape, lambda i: (0, 0, 0))],
        out_specs=pl.BlockSpec((tr, 512), lambda i: (i, 0)),
        out_shape=jax.ShapeDtypeStruct((rows, 512), F32),
        compiler_params=_cparams("parallel"),
        name="cmp_partial_prompt",
    )(kvc, wk, wv, pos)


PAGES_PER_STEP = 16


def _cmp_sample_kernel(pt_ref, *refs):
    pages = refs[:PAGES_PER_STEP]
    perm_ref, wk_ref, wv_ref, pos_ref, ab_ref, ks_ref, x_sc = refs[PAGES_PER_STEP:]
    chunks = PAGE // CMP_STRIDE
    for p, page in enumerate(pages):
        x_sc[p] = _dot_nt(perm_ref[...], page[0:256, :].astype(BF16))
        ks_ref[:, PAGE * p:PAGE * (p + 1)] = page[256:512, :].astype(BF16)

    def slab(t, kind):
        return jnp.concatenate([x_sc[p, chunks * t:chunks * (t + 1), LANES * kind:LANES * (kind + 1)]
                                for p in range(PAGES_PER_STEP)], axis=0)

    _cmp_partial(slab, wk_ref, wv_ref, pos_ref, ab_ref)


def _cmp_sample_call(cache_t, page_table, wk, wv, pos):
    nb, n_pages = page_table.shape
    chunks = PAGE // CMP_STRIDE
    steps = n_pages // PAGES_PER_STEP

    def page_spec(k):
        return pl.BlockSpec((None, 512, PAGE), lambda b, j, pt: (pt[b, j * PAGES_PER_STEP + k], 0, 0))

    grid_spec = pltpu.PrefetchScalarGridSpec(
        num_scalar_prefetch=1,
        grid=(nb, steps),
        in_specs=[page_spec(k) for k in range(PAGES_PER_STEP)] + [
            pl.BlockSpec((PAGE, PAGE), lambda b, j, pt: (0, 0)),
            pl.BlockSpec(wk.shape, lambda b, j, pt: (0, 0)),
            pl.BlockSpec(wv.shape, lambda b, j, pt: (0, 0)),
            pl.BlockSpec(pos.shape, lambda b, j, pt: (0, 0, 0))],
        out_specs=[pl.BlockSpec((None, PAGES_PER_STEP * chunks, 512), lambda b, j, pt: (b, j, 0)),
                   pl.BlockSpec((None, 256, PAGES_PER_STEP * PAGE), lambda b, j, pt: (b, 0, j))],
        scratch_shapes=[pltpu.VMEM((PAGES_PER_STEP, PAGE, 256), F32)],
    )
    tok = np.arange(PAGE)
    perm = np.zeros((PAGE, PAGE), np.float32)
    perm[(tok % CMP_STRIDE) * chunks + tok // CMP_STRIDE, tok] = 1.0
    return pl.pallas_call(
        _cmp_sample_kernel,
        grid_spec=grid_spec,
        out_shape=[jax.ShapeDtypeStruct((nb, n_pages * chunks, 512), F32),
                   jax.ShapeDtypeStruct((nb, 256, n_pages * PAGE), BF16)],
        compiler_params=_cparams("parallel", "arbitrary"),
        name="cmp_partial_sample",
    )(page_table, *([cache_t] * PAGES_PER_STEP), jnp.asarray(perm, dtype=BF16), wk, wv, pos)


def _cmp_fin_kernel(ab_ref, w2_ref, o_ref):
    nc = ab_ref.shape[0]
    a = ab_ref[:, :256]
    b_next = pltpu.roll(ab_ref[:, 256:], nc - 1, 0)
    hid = jax.nn.gelu(a + b_next)
    out = _dot(hid.astype(BF16), w2_ref[...])
    row = lax.broadcasted_iota(jnp.int32, (nc, 1), 0)
    out = jnp.where(row < nc - 1, out, 0.0)
    for s in range(4):
        o_ref[s] = out[:, HD * s:HD * (s + 1)]


def _cmp_fin_call(ab, w2bd):
    nb, nc, _ = ab.shape
    return pl.pallas_call(
        _cmp_fin_kernel,
        grid=(nb,),
        in_specs=[pl.BlockSpec((None, nc, 512), lambda b: (b, 0, 0)),
                  pl.BlockSpec((256, 256), lambda b: (0, 0))],
        out_specs=pl.BlockSpec((None, 4, nc, HD), lambda b: (b, 0, 0, 0)),
        out_shape=jax.ShapeDtypeStruct((nb, 4, nc, HD), F32),
        compiler_params=_cparams("parallel"),
        name="cmp_finish",
    )(ab, w2bd)


def _nsa_prompt_kernel(q_ref, gat_ref, kc_ref, vc_ref, ka_ref, va_ref, kw_ref, vw_ref, mmat_ref,
                       o_ref, qa_sc, s_sc, mcur_sc, m_sc, acc_sc, *, ns, n_sel):
    i = pl.program_id(2)
    start = i * Q_TILE
    qpos = start + lax.broadcasted_iota(jnp.int32, (Q_TILE, 1), 0)
    nc = kc_ref.shape[0]
    head_rows =[slice(r * Q_TILE, (r + 1) * Q_TILE) for r in range(REP)]

    kc = kc_ref[...].astype(BF16)
    vc = vc_ref[...].astype(BF16)
    c_end = lax.broadcasted_iota(jnp.int32, (1, nc), 1) * CMP_STRIDE + (2 * CMP_STRIDE - 1)
    cbias = jnp.where(c_end <= qpos, 0.0, NEG_INF)
    any_valid = qpos >= 2 * CMP_STRIDE - 1
    psum = jnp.zeros((Q_TILE, nc), F32)
    o_cmp = []
    for r in range(REP):
        s = _dot_nt(q_ref[r], kc) + cbias
        e = jnp.exp2(s - jnp.max(s, axis=-1, keepdims=True))
        norm = jnp.where(any_valid, 1.0 / jnp.maximum(jnp.sum(e, axis=-1, keepdims=True), 1e-30), 0.0)
        p = e * norm
        psum = psum + p
        o_cmp.append(_dot(p.astype(BF16), vc))

    sel = _selection(psum, mmat_ref[...], qpos, ns, n_sel)
    selneg = jnp.where(sel > 0.5, 0.0, NEG_INF).astype(BF16)
    for r in range(REP):
        qa_sc[head_rows[r], :] = jnp.concatenate([selneg, q_ref[r]], axis=1)

    base = pl.multiple_of(jnp.maximum(start - WINDOW, 0), Q_TILE)
    kw = kw_ref[pl.ds(base, WINDOW + Q_TILE), :]
    vw = vw_ref[pl.ds(base, WINDOW + Q_TILE), :]
    dpos = qpos - (base + lax.broadcasted_iota(jnp.int32, (1, WINDOW + Q_TILE), 1))
    wbias = jnp.where((dpos >= 0) & (dpos < WINDOW), 0.0, NEG_INF)
    o_win = []
    for r in range(REP):
        s = _dot_nt(q_ref[r], kw) + wbias
        e = jnp.exp2((s - jnp.max(s, axis=-1, keepdims=True)).astype(BF16))
        ow = _dot(e, vw)
        o_win.append(ow[:, :HD] / jnp.maximum(ow[:, HD:HD + 1], 1e-30))

    n_full = start // KV_TILE

    def scores(j):
        k0 = pl.multiple_of(j * KV_TILE, KV_TILE)
        return _dot_nt(qa_sc[...], ka_ref[pl.ds(k0, KV_TILE), :])

    def consume(slot, t):
        v0 = pl.multiple_of(t * KV_TILE, KV_TILE)
        v_t = va_ref[pl.ds(v0, KV_TILE), :]
        for r in range(REP):
            rows = head_rows[r]
            m_old = m_sc[rows]
            m_new = jnp.maximum(m_old, mcur_sc[slot, rows])
            p = jnp.exp2((s_sc[slot, rows] - m_new).astype(BF16))
            acc_sc[rows] = jnp.exp2(m_old - m_new) * acc_sc[rows] + _dot(p, v_t)
            m_sc[rows] = m_new

    kpos = n_full * KV_TILE + lax.broadcasted_iota(jnp.int32, (1, KV_TILE), 1)
    causal_bias = jnp.where(kpos <= qpos, 0.0, NEG_INF)
    s_diag = scores(n_full)
    for r in range(REP):
        s_r = s_diag[head_rows[r]] + causal_bias
        s_sc[0, head_rows[r]] = s_r
        mcur_sc[0, head_rows[r]] = jnp.max(s_r, axis=-1, keepdims=True)
    m_sc[...] = jnp.full(m_sc.shape, NEG_INF, F32)
    acc_sc[...] = jnp.zeros(acc_sc.shape, F32)

    def body(j, carry):
        consume(j & 1, jnp.where(j == 0, n_full, j - 1))
        s_new = scores(j)
        slot_new = (j + 1) & 1
        s_sc[slot_new] = s_new
        mcur_sc[slot_new] = jnp.max(s_new, axis=-1, keepdims=True)
        return carry

    lax.fori_loop(0, n_full, body, 0)
    consume(n_full & 1, jnp.maximum(n_full - 1, 0))

    gate = jax.nn.sigmoid(gat_ref[...])
    for r in range(REP):
        acc = acc_sc[head_rows[r]]
        o_slc = acc[:, :HD] / jnp.maximum(acc[:, HD:HD + 1], 1e-30)
        o_ref[:, HD * r:HD * (r + 1)] = (gate[:, r:r + 1] * o_cmp[r] + gate[:, REP + r:REP + r + 1] * o_slc
                                        + gate[:, 2 * REP + r:2 * REP + r + 1] * o_win[r])


def _nsa_prompt_call(qh, gat, cmp, kaug, vaug, kwin, vwaug, mmat, nb, seq, cmp_off):
    nq = seq // Q_TILE
    nc = cmp.shape[2]
    ns = seq // SEL_BLOCK
    assert ns <= LANES and mmat.shape[1] == LANES
    kern = functools.partial(_nsa_prompt_kernel, ns=ns, n_sel=min(N_SEL, ns))

    def kv_spec(width):
        return pl.BlockSpec((None, seq, width), lambda b, g, i: (g, b, 0))

    return pl.pallas_call(
        kern,
        grid=(nb, N_KV, nq),
        in_specs=[pl.BlockSpec((REP, Q_TILE, HD), lambda b, g, i: (g, b * nq + i, 0)),
                  pl.BlockSpec((Q_TILE, LANES), lambda b, g, i: (b * nq + i, g)),
                  pl.BlockSpec((None, None, nc, HD), lambda b, g, i: (cmp_off + b, g, 0, 0)),
                  pl.BlockSpec((None, None, nc, HD), lambda b, g, i: (cmp_off + b, 2 + g, 0, 0)),
                  kv_spec(LANES + HD), kv_spec(LANES), kv_spec(HD), kv_spec(LANES),
                  pl.BlockSpec(mmat.shape, lambda b, g, i: (0, 0))],
        out_specs=pl.BlockSpec((Q_TILE, REP * HD), lambda b, g, i: (b * nq + i, g)),
        out_shape=jax.ShapeDtypeStruct((nb * seq, D_NSA), F32),
        scratch_shapes=[pltpu.VMEM((REP * Q_TILE, LANES + HD), BF16),
                        pltpu.VMEM((2, REP * Q_TILE, KV_TILE), F32),
                        pltpu.VMEM((2, REP * Q_TILE, 1), F32),
                        pltpu.VMEM((REP * Q_TILE, 1), F32),
                        pltpu.VMEM((REP * Q_TILE, LANES), F32)],
        compiler_params=_cparams("parallel", "parallel", "arbitrary"),
        name="nsa_prompt",
    )(qh, gat, cmp, cmp, kaug, vaug, kwin, vwaug, mmat)


def _nsa_sample_kernel(q_ref, gat_ref, cmp_ref, ks_ref, new_ref, win_ref, mmat_ref, oht_ref, o_ref,
                       *, past, ts, ns, n_sel):
    nc = cmp_ref.shape[1]
    wb = win_ref.shape[1]
    t_idx = lax.broadcasted_iota(jnp.int32, (ts, 1), 0)
    qpos = jnp.concatenate([past + t_idx] * REP, axis=0)
    gate = jax.nn.sigmoid(gat_ref[...])
    pad_rows = jnp.zeros((LANES - ts, HD), F32)

    def padded(slab):
        return jnp.concatenate([new_ref[slab], pad_rows], axis=0).astype(BF16)

    c_end = lax.broadcasted_iota(jnp.int32, (1, nc), 1) * CMP_STRIDE + (2 * CMP_STRIDE - 1)
    qs, o_cmps, psums = [], [], []
    for g in range(N_KV):
        q = jnp.concatenate([q_ref[REP * g + r] for r in range(REP)], axis=0).astype(BF16)
        p = _masked_softmax(_dot_nt(q, cmp_ref[g].astype(BF16)), c_end <= qpos)
        o_cmps.append(_dot(p.astype(BF16), cmp_ref[2 + g].astype(BF16)))
        psum = p[0:ts]
        for r in range(1, REP):
            psum = psum + p[r * ts:(r + 1) * ts]
        qs.append(q)
        psums.append(psum)
    psum_all = jnp.concatenate(psums + [jnp.zeros((LANES - N_KV * ts, nc), F32)], axis=0)
    row = lax.broadcasted_iota(jnp.int32, (LANES, 1), 0)
    sel_all = _selection(psum_all, mmat_ref[...], past + lax.rem(row, ts), ns, n_sel)

    kidx = lax.broadcasted_iota(jnp.int32, (1, LANES), 1)
    kpos_w = past - wb + lax.broadcasted_iota(jnp.int32, (1, wb + LANES), 1)
    dpos_w = qpos - kpos_w
    wmask = (dpos_w >= 0) & (dpos_w < WINDOW) & (kpos_w >= 0)
    for g in range(N_KV):
        q = qs[g]
        sel = jnp.concatenate([sel_all[g * ts:(g + 1) * ts]] * REP, axis=0)
        selneg = jnp.where(sel[:, :LANES] > 0.5, 0.0, NEG_INF).astype(BF16)
        s_old = _dot(q, ks_ref[HD * g:HD * (g + 1), :]) + _dot(selneg, oht_ref[...])
        new_bias = jnp.where((sel[:, ns - 1:ns] > 0.5) & (past + kidx <= qpos), 0.0, NEG_INF)
        s_new = _dot_nt(q, padded(g)) + new_bias
        m = jnp.maximum(jnp.max(s_old, axis=-1, keepdims=True), jnp.max(s_new, axis=-1, keepdims=True))
        e_old = jnp.exp(s_old - m)
        e_new = jnp.exp(s_new - m)
        denom = jnp.sum(e_old, axis=-1, keepdims=True) + jnp.sum(e_new, axis=-1, keepdims=True)
        o_slc = (_dot_nt(e_old.astype(BF16), ks_ref[128 + HD * g:128 + HD * (g + 1), :])
                 + _dot(e_new.astype(BF16), padded(2 + g))) / jnp.maximum(denom, 1e-30)

        s_w = jnp.concatenate([_dot(q, win_ref[HD * g:HD * (g + 1), :].astype(BF16)),
                               _dot_nt(q, padded(4 + g))], axis=1)
        pw = _masked_softmax(s_w, wmask)
        o_win = (_dot_nt(pw[:, :wb].astype(BF16), win_ref[128 + HD * g:128 + HD * (g + 1), :].astype(BF16))
                 + _dot(pw[:, wb:].astype(BF16), padded(6 + g)))

        for r in range(REP):
            rs = slice(r * ts, (r + 1) * ts)
            c = LANES * g + r
            o_ref[:, HD * (REP * g + r):HD * (REP * g + r + 1)] = (
                gate[:, c:c + 1] * o_cmps[g][rs] + gate[:, c + REP:c + REP + 1] * o_slc[rs]
                + gate[:, c + 2 * REP:c + 2 * REP + 1] * o_win[rs])


def _nsa_sample_call(qh, gat, cmp, ks_t, kvh_new, win_t, mmat, oht, nb, ts, past, cmp_off):
    nc = cmp.shape[2]
    ns = past // SEL_BLOCK + 1
    assert ns - 1 <= LANES and N_KV * ts <= LANES
    kern = functools.partial(_nsa_sample_kernel, past=past, ts=ts, ns=ns, n_sel=min(N_SEL, ns))
    return pl.pallas_call(
        kern,
        grid=(nb,),
        in_specs=[pl.BlockSpec((8, ts, HD), lambda b: (0, b, 0)),
                  pl.BlockSpec((ts, 256), lambda b: (b, 0)),
                  pl.BlockSpec((None, 4, nc, HD), lambda b: (cmp_off + b, 0, 0, 0)),
                  pl.BlockSpec((None, 256, past), lambda b: (b, 0, 0)),
                  pl.BlockSpec((8, ts, HD), lambda b: (0, b, 0)),
                  pl.BlockSpec((None, 256, win_t.shape[2]), lambda b: (b, 0, 0)),
                  pl.BlockSpec(mmat.shape, lambda b: (0, 0)),
                  pl.BlockSpec(oht.shape, lambda b: (0, 0))],
        out_specs=pl.BlockSpec((ts, D_NSA), lambda b: (b, 0)),
        out_shape=jax.ShapeDtypeStruct((nb * ts, D_NSA), F32),
        compiler_params=_cparams("parallel"),
        name="nsa_sample",
    )(qh, gat, cmp, ks_t, kvh_new, win_t, mmat, oht)


def _ret_kernel(x_ref, s0_ref, dmat_ref, xi_ref, zeta_ref, gc_ref, gro_ref, o_ref, s_out_ref, s_sc, pad_sc,
                *, rows):
    @pl.when(pl.program_id(1) == 0)
    def _():
        s_sc[...] = s0_ref[...]

    if rows < RET_CHUNK:
        pad_sc[...] = jnp.zeros(pad_sc.shape, F32)
        pad_sc[0:rows, :] = x_ref[...]
        x = pad_sc
    else:
        x = x_ref
    for c in range(max(rows // RET_CHUNK, 1)):
        r0 = RET_CHUNK * c
        for h in range(N_RET):
            q = x[r0:r0 + RET_CHUNK, DK_RET * h:DK_RET * (h + 1)].astype(BF16)
            k = x[r0:r0 + RET_CHUNK, 256 + DK_RET * h:256 + DK_RET * (h + 1)]
            v = x[r0:r0 + RET_CHUNK, 512 + DV_RET * h:512 + DV_RET * (h + 1)].astype(BF16)
            gr = x[r0:r0 + RET_CHUNK, 1024 + DV_RET * h:1024 + DV_RET * (h + 1)]
            att = _dot_nt(q, k.astype(BF16)) * dmat_ref[h]
            s_old = s_sc[h]
            o = _dot(att.astype(BF16), v) + _dot(q, s_old.astype(BF16)) * xi_ref[h]
            s_sc[h] = gc_ref[h] * s_old + _dot_tn((k * zeta_ref[h]).astype(BF16), v)
            mu = jnp.mean(o, axis=-1, keepdims=True)
            var = jnp.mean(jnp.square(o - mu), axis=-1, keepdims=True)
            y = (o - mu) * lax.rsqrt(var + EPS) * gro_ref[:, DV_RET * h:DV_RET * (h + 1)]
            res = gr * jax.nn.sigmoid(gr) * y
            n_out = min(rows, RET_CHUNK)
            o_ref[r0:r0 + n_out, DV_RET * h:DV_RET * (h + 1)] = res[0:n_out]
    s_out_ref[...] = s_sc[...]


def _ret_call(ret, s0, tabs, g_ret_out, nb, seq, rows):
    nchunks = seq // rows
    dmat, xi, zeta, gc = tabs
    kern = functools.partial(_ret_kernel, rows=rows)
    full3 = lambda b, c: (0, 0, 0)
    return pl.pallas_call(
        kern,
        grid=(nb, nchunks),
        in_specs=[pl.BlockSpec((rows, 1536), lambda b, c: (b * nchunks + c, 0)),
                  pl.BlockSpec((None, N_RET, DK_RET, DV_RET), lambda b, c: (b, 0, 0, 0)),
                  pl.BlockSpec(dmat.shape, full3), pl.BlockSpec(xi.shape, full3),
                  pl.BlockSpec(zeta.shape, full3), pl.BlockSpec(gc.shape, full3),
                  pl.BlockSpec((1, D_RET), lambda b, c: (0, 0))],
        out_specs=[pl.BlockSpec((rows, D_RET), lambda b, c: (b * nchunks + c, 0)),
                   pl.BlockSpec((None, N_RET, DK_RET, DV_RET), lambda b, c: (b, 0, 0, 0))],
        out_shape=[jax.ShapeDtypeStruct((nb * seq, D_RET), F32),
                   jax.ShapeDtypeStruct((nb, N_RET, DK_RET, DV_RET), F32)],
        scratch_shapes=[pltpu.VMEM((N_RET, DK_RET, DV_RET), F32),
                        pltpu.VMEM((RET_CHUNK, 1536), F32)],
        compiler_params=_cparams("parallel", "arbitrary"),
        name="retention",
    )(ret, s0, dmat, xi, zeta, gc, g_ret_out)


def _ret_tables(chunk):
    c = RET_CHUNK
    log_g = jnp.log1p(-jnp.power(2.0, -5.0 - jnp.arange(N_RET, dtype=F32)))
    i = jnp.arange(c, dtype=F32)
    diff = i[:, None] - i[None, :]
    dmat = jnp.where(diff >= 0, jnp.exp(jnp.maximum(diff, 0.0)[None] * log_g[:, None, None]), 0.0)
    xi = jnp.exp((i[None, :] + 1.0) * log_g[:, None])
    zeta = jnp.where(i[None, :] < chunk, jnp.exp((chunk - 1.0 - i)[None, :] * log_g[:, None]), 0.0)
    g_c = jnp.exp(chunk * log_g)
    return (dmat,
            jnp.broadcast_to(xi[:, :, None], (N_RET, c, DV_RET)),
            jnp.broadcast_to(zeta[:, :, None], (N_RET, c, DK_RET)),
            jnp.broadcast_to(g_c[:, None, None], (N_RET, DK_RET, DV_RET)))


def _outproj_kernel(on_ref, or_ref, x_ref, gt_ref, sc_ref, sh_ref, gn_ref, g2_ref, w_ref, x1_ref, h2_ref):
    a = _rms(on_ref[...], gn_ref[...])
    mix = _dot(a.astype(BF16), w_ref[0:D_NSA, :]) + _dot(or_ref[...].astype(BF16), w_ref[D_NSA:, :])
    x1 = x_ref[...] + gt_ref[...] * mix
    x1_ref[...] = x1
    h2_ref[...] = (_rms(x1, g2_ref[...]) * (1.0 + sc_ref[...]) + sh_ref[...]).astype(h2_ref.dtype)


def _outproj_call(o_nsa, o_ret, x2d, gt3, sc3, sh3, mod_idx, g_nsa, g2, w_out, tm, h_dtype):
    m = x2d.shape[0]
    mrows = gt3.shape[1]
    mod_spec = pl.BlockSpec((None, mrows, D_MODEL), lambda i: (mod_idx(i), 0, 0))
    return pl.pallas_call(
        _outproj_kernel,
        grid=(m // tm,),
        in_specs=[pl.BlockSpec((tm, D_NSA), lambda i: (i, 0)),
                  pl.BlockSpec((tm, D_RET), lambda i: (i, 0)),
                  pl.BlockSpec((tm, D_MODEL), lambda i: (i, 0)),
                  mod_spec, mod_spec, mod_spec,
                  pl.BlockSpec((1, D_NSA), lambda i: (0, 0)),
                  pl.BlockSpec((1, D_MODEL), lambda i: (0, 0)),
                  pl.BlockSpec((D_NSA + D_RET, D_MODEL), lambda i: (0, 0))],
        out_specs=[pl.BlockSpec((tm, D_MODEL), lambda i: (i, 0)),
                   pl.BlockSpec((tm, D_MODEL), lambda i: (i, 0))],
        out_shape=[jax.ShapeDtypeStruct((m, D_MODEL), F32),
                   jax.ShapeDtypeStruct((m, D_MODEL), h_dtype)],
        compiler_params=_cparams("parallel"),
        name="out_proj",
    )(o_nsa, o_ret, x2d, gt3, sc3, sh3, g_nsa, g2, w_out)


FF_TILE = D_FF // 2


def _ffn_up_kernel(h_ref, wa_ref, wb_ref, cwa_ref, cwb_ref, cba_ref, cbb_ref, sta_ref, stb_ref,
                   act_ref, csa_ref, csb_ref, prev_a, prev_b):
    @pl.when(pl.program_id(2) == 0)
    def _():
        prev_a[...] = sta_ref[...]
        prev_b[...] = stb_ref[...]

    h = h_ref[...].astype(BF16)
    tm = h.shape[0]
    row = lax.broadcasted_iota(jnp.int32, (tm, 1), 0)

    def half(w_ref, cw_ref, cb_ref, prev, cs_ref):
        u = _dot(h, w_ref[...])
        p = prev[...]
        back1 = pltpu.roll(u, 1, 0)
        back2 = pltpu.roll(u, 2, 0)
        u1 = jnp.where(row >= 1, back1, p[1:2])
        u2 = jnp.where(row >= 2, back2, jnp.where(row == 1, p[1:2], p[0:1]))
        y = cb_ref[...] + cw_ref[0:1] * u2 + cw_ref[1:2] * u1 + cw_ref[2:3] * u
        tail = back2[0:2]
        prev[...] = tail
        cs_ref[...] = tail
        return y

    a = half(wa_ref, cwa_ref, cba_ref, prev_a, csa_ref)
    b = half(wb_ref, cwb_ref, cbb_ref, prev_b, csb_ref)
    act_ref[...] = (a * jax.nn.sigmoid(a) * b).astype(act_ref.dtype)


def _ffn_up_call(h2, w_up_a, w_up_b, conv_w, conv_b, conv_state, nb, seq, tm, act_dtype):
    nrt = seq // tm
    nt = D_FF // FF_TILE
    return pl.pallas_call(
        _ffn_up_kernel,
        grid=(nt, nb, nrt),
        in_specs=[pl.BlockSpec((tm, D_MODEL), lambda j, b, i: (b * nrt + i, 0)),
                  pl.BlockSpec((D_MODEL, FF_TILE), lambda j, b, i: (0, j)),
                  pl.BlockSpec((D_MODEL, FF_TILE), lambda j, b, i: (0, j)),
                  pl.BlockSpec((3, FF_TILE), lambda j, b, i: (0, j)),
                  pl.BlockSpec((3, FF_TILE), lambda j, b, i: (0, nt + j)),
                  pl.BlockSpec((1, FF_TILE), lambda j, b, i: (0, j)),
                  pl.BlockSpec((1, FF_TILE), lambda j, b, i: (0, nt + j)),
                  pl.BlockSpec((None, 2, FF_TILE), lambda j, b, i: (b, 0, j)),
                  pl.BlockSpec((None, 2, FF_TILE), lambda j, b, i: (b, 0, nt + j))],
        out_specs=[pl.BlockSpec((tm, FF_TILE), lambda j, b, i: (b * nrt + i, j)),
                   pl.BlockSpec((None, 2, FF_TILE), lambda j, b, i: (b, 0, j)),
                   pl.BlockSpec((None, 2, FF_TILE), lambda j, b, i: (b, 0, j))],
        out_shape=[jax.ShapeDtypeStruct((nb * seq, D_FF), act_dtype),
                   jax.ShapeDtypeStruct((nb, 2, D_FF), F32),
                   jax.ShapeDtypeStruct((nb, 2, D_FF), F32)],
        scratch_shapes=[pltpu.VMEM((2, FF_TILE), F32), pltpu.VMEM((2, FF_TILE), F32)],
        compiler_params=_cparams("parallel", "parallel", "arbitrary"),
        name="ffn_up_conv",
    )(h2, w_up_a, w_up_b, conv_w, conv_w, conv_b, conv_b, conv_state, conv_state)


def _ffn_down_kernel(a_ref, x1_ref, gt_ref, w_ref, gf_ref, y_ref, *, final_norm):
    x2 = x1_ref[...] + gt_ref[...] * _dot(a_ref[...].astype(BF16), w_ref[...])
    y_ref[...] = _rms(x2, gf_ref[...]) if final_norm else x2


def _ffn_down_call(act, x1, gt3, mod_idx, w_down, g_final, tm, final_norm):
    m = x1.shape[0]
    mrows = gt3.shape[1]
    return pl.pallas_call(
        functools.partial(_ffn_down_kernel, final_norm=final_norm),
        grid=(m // tm,),
        in_specs=[pl.BlockSpec((tm, D_FF), lambda i: (i, 0)),
                  pl.BlockSpec((tm, D_MODEL), lambda i: (i, 0)),
                  pl.BlockSpec((None, mrows, D_MODEL), lambda i: (mod_idx(i), 0, 0)),
                  pl.BlockSpec((D_FF, D_MODEL), lambda i: (0, 0)),
                  pl.BlockSpec((1, D_MODEL), lambda i: (0, 0))],
        out_specs=pl.BlockSpec((tm, D_MODEL), lambda i: (i, 0)),
        out_shape=jax.ShapeDtypeStruct((m, D_MODEL), F32),
        compiler_params=_cparams("parallel"),
        name="ffn_down",
    )(act, x1, gt3, w_down, g_final)


def _relayout_w_in(w_in):
    q_n, kv_n, gate_n, q_r, k_r, v_r, g_r = jnp.split(w_in, [512, 1280, 1304, 1560, 1816, 2328], axis=1)
    gate = gate_n.reshape(D_MODEL, 3, N_KV, REP)
    gate = jnp.transpose(gate, (0, 2, 1, 3)).reshape(D_MODEL, N_KV, 3 * REP)
    gate = jnp.pad(gate, ((0, 0), (0, 0), (0, LANES - 3 * REP))).reshape(D_MODEL, N_KV * LANES)
    return jnp.concatenate([q_n, kv_n, q_r, k_r, v_r, g_r, gate], axis=1).astype(BF16)


def _relayout_cmp(w1, pos):
    w1r = w1.reshape(2, CMP_STRIDE, HD, HD)
    w = jnp.einsum('atdn,gh->tgdahn', w1r, jnp.eye(N_KV, dtype=w1.dtype)).reshape(CMP_STRIDE * N_KV * HD, 2 * N_KV * HD)
    posr = pos.reshape(2, CMP_STRIDE, 1, HD)
    prow = jnp.broadcast_to(posr, (2, CMP_STRIDE, N_KV, HD)).reshape(2, CMP_STRIDE * N_KV * HD)
    prow = jnp.pad(prow, ((0, 14), (0, 0)))
    return w.astype(BF16), prow.astype(BF16)


def _block_diag4(w2k, w2v):
    z = jnp.zeros((HD, HD), w2k.dtype)
    rows = [[w2k, z, z, z], [z, w2k, z, z], [z, z, w2v, z], [z, z, z, w2v]]
    return jnp.block(rows).astype(BF16)


def _rope_tables(pos):
    half = HD // 2
    inv = ROPE_THETA ** (-jnp.arange(half, dtype=F32) / half)
    ang = pos.astype(F32)[:, None] * inv[None, :]
    cos, sin = jnp.cos(ang), jnp.sin(ang)
    return jnp.tile(cos, (1, 4)), jnp.tile(jnp.concatenate([-sin, sin], axis=1), (1, 2))


def _importance_matrix(nc, nsp):
    n = np.arange(nc)[:, None]
    d = n - 4 * np.arange(nsp)[None, :]
    m = ((d >= 0) & (d <= 3)).astype(np.float32) + ((d >= -1) & (d <= 2)).astype(np.float32)
    return jnp.asarray(m, dtype=BF16)


def _block_onehot_t(past):
    blk = np.arange(LANES)[:, None]
    return jnp.asarray((blk == np.arange(past)[None, :] // SEL_BLOCK).astype(np.float32), dtype=BF16)


def _round_up(x, m):
    return (x + m - 1) // m * m


def kernel(x_prompt, x_sample, c_prompt, c_sample, cache_nsa_kv, cache_win_kv, state_ret, state_conv, page_table,
           w_ada, b_ada, g_norm1, w_in, cmp_pos_k, cmp_w1_k, cmp_w2_k, cmp_pos_v, cmp_w1_v, cmp_w2_v,
           g_nsa_out, g_ret_out, w_out, g_norm2, w_up, conv_w, conv_b, w_down, g_final):
    nb_p, seq, _ = x_prompt.shape
    nb_s, ts, _ = x_sample.shape
    depth = w_ada.shape[0]
    n_pages = page_table.shape[1]
    past = n_pages * PAGE
    wb = cache_win_kv.shape[2]
    m_p, m_s = nb_p * seq, nb_s * ts
    tm_p = 512

    xp = x_prompt.reshape(m_p, D_MODEL)
    xs = x_sample.reshape(m_s, D_MODEL)
    n_c = nb_p + nb_s
    c_all = jnp.pad(jnp.concatenate([c_prompt, c_sample], axis=0), ((0, _round_up(n_c, 8) - n_c), (0, 0)))

    cos_p, sin_p = _rope_tables(jnp.arange(seq, dtype=jnp.int32))
    cos_s, sin_s = _rope_tables(jnp.tile(past + jnp.arange(ts, dtype=jnp.int32), nb_s))
    tabs_p = _ret_tables(min(RET_CHUNK, seq))
    tabs_s = _ret_tables(ts)
    nc_p, nc_s = seq // CMP_STRIDE, past // CMP_STRIDE
    nsp_p = _round_up(seq // SEL_BLOCK, LANES)
    nsp_s = _round_up(past // SEL_BLOCK + 1, LANES)
    mmat_p = _importance_matrix(nc_p, nsp_p)
    mmat_s, oht_s = _importance_matrix(nc_s, nsp_s), _block_onehot_t(past)

    tiles_per_batch = seq // tm_p
    idx_p = lambda i: i // tiles_per_batch
    tab_p = lambda i: i % tiles_per_batch
    idx_s = lambda i: 0

    outs = {k: [] for k in ('kv_p', 'kv_s', 'win_p', 'win_s', 'ret_p', 'ret_s', 'conv_p', 'conv_s')}
    for l in range(depth):
        mod = _mod_call(c_all, w_ada[l], b_ada[l])
        mods_p = [a.reshape(nb_p, 1, D_MODEL) for a in jnp.split(mod[:nb_p], 6, axis=1)]
        mods_s = [jnp.repeat(a, ts, axis=0).reshape(1, m_s, D_MODEL) for a in jnp.split(mod[nb_p:n_c], 6, axis=1)]
        w_in_re = _relayout_w_in(w_in[l])
        wk, pos_k = _relayout_cmp(cmp_w1_k[l], cmp_pos_k[l])
        wv, pos_v = _relayout_cmp(cmp_w1_v[l], cmp_pos_v[l])
        pos_kv = jnp.stack([pos_k, pos_v])
        w2bd = _block_diag4(cmp_w2_k[l], cmp_w2_v[l])
        g1 = g_norm1[l].reshape(1, -1)
        g2 = g_norm2[l].reshape(1, -1)
        g_nsa = g_nsa_out[l].reshape(1, -1)
        g_ret = g_ret_out[l].reshape(1, -1)
        w_out_bf = w_out[l].astype(BF16)
        w_up_a = w_up[l][:, :D_FF].astype(BF16)
        w_up_b = w_up[l][:, D_FF:].astype(BF16)
        w_down_bf = w_down[l].astype(BF16)

        sh1, sc1, gt1, sh2, sc2, gt2 = mods_p
        qh_p, kvn_t, kvw_t, kaug_p, vaug_p, kwin_p, vwaug_p, ret_p, gat_p, kvc_p = _inproj_call(
            xp, sc1, sh1, idx_p, g1, w_in_re, cos_p, sin_p, tab_p, tm_p, HD ** -0.5 * LOG2_E, tiles_per_batch)
        sh1s, sc1s, gt1s, sh2s, sc2s, gt2s = mods_s
        qh_s, kvn_s, kvw_s, kvh_s, ret_s, gat_s = _inproj_call(
            xs, sc1s, sh1s, idx_s, g1, w_in_re, cos_s, sin_s, idx_s, m_s, HD ** -0.5)

        ab_p = _cmp_prompt_call(kvc_p, wk, wv, pos_kv).reshape(nb_p, nc_p, 512)
        cache_t = jnp.transpose(cache_nsa_kv[:, l], (0, 2, 3, 4, 1)).reshape(cache_nsa_kv.shape[0], 512, PAGE)
        win_t = jnp.transpose(cache_win_kv[:, l], (0, 2, 3, 4, 1)).reshape(nb_s, 256, wb)
        ab_s, ks_s = _cmp_sample_call(cache_t, page_table, wk, wv, pos_kv)
        cmp_p = _cmp_fin_call(ab_p, w2bd)
        cmp_s = _cmp_fin_call(ab_s, w2bd)

        o_nsa_p = _nsa_prompt_call(qh_p, gat_p, cmp_p, kaug_p, vaug_p, kwin_p, vwaug_p, mmat_p, nb_p, seq, 0)
        o_nsa_s = _nsa_sample_call(qh_s, gat_s, cmp_s, ks_s, kvh_s, win_t, mmat_s, oht_s, nb_s, ts, past, 0)

        s0_p = jnp.zeros((nb_p, N_RET, DK_RET, DV_RET), F32)
        o_ret_p, s_new_p = _ret_call(ret_p, s0_p, tabs_p, g_ret, nb_p, seq, min(4 * RET_CHUNK, seq))
        o_ret_s, s_new_s = _ret_call(ret_s, state_ret[:, l], tabs_s, g_ret, nb_s, ts, ts)

        x1_p, h2_p = _outproj_call(o_nsa_p, o_ret_p, xp, gt1, sc2, sh2, idx_p, g_nsa, g2, w_out_bf, tm_p, BF16)
        x1_s, h2_s = _outproj_call(o_nsa_s, o_ret_s, xs, gt1s, sc2s, sh2s, idx_s, g_nsa, g2, w_out_bf, m_s, F32)

        conv0_p = jnp.zeros((nb_p, 2, 2 * D_FF), F32)
        act_p, csa_p, csb_p = _ffn_up_call(h2_p, w_up_a, w_up_b, conv_w[l], conv_b[l].reshape(1, -1), conv0_p,
                                           nb_p, seq, tm_p, BF16)
        act_s, csa_s, csb_s = _ffn_up_call(h2_s, w_up_a, w_up_b, conv_w[l], conv_b[l].reshape(1, -1),
                                           state_conv[:, l], nb_s, ts, ts, F32)
        last = l == depth - 1
        gf = g_final.reshape(1, -1)
        xp = _ffn_down_call(act_p, x1_p, gt2, idx_p, w_down_bf, gf, tm_p, last)
        xs = _ffn_down_call(act_s, x1_s, gt2s, idx_s, w_down_bf, gf, m_s, last)

        outs['kv_p'].append(jnp.transpose(kvn_t.reshape(nb_p, 4, N_KV, HD, seq), (0, 4, 1, 2, 3)))
        outs['kv_s'].append(kvn_s.reshape(nb_s, ts, 4, N_KV, HD))
        keep_p = min(WINDOW, seq)
        win_t_p = kvw_t[:, :, seq - keep_p:].reshape(nb_p, 2, N_KV, HD, keep_p)
        outs['win_p'].append(jnp.transpose(win_t_p, (0, 4, 1, 2, 3)))
        win_all = jnp.concatenate([cache_win_kv[:, l], kvw_s.reshape(nb_s, ts, 2, N_KV, HD)], axis=1)
        outs['win_s'].append(win_all[:, win_all.shape[1] - min(WINDOW, past + ts):])
        outs['ret_p'].append(s_new_p)
        outs['ret_s'].append(s_new_s)
        outs['conv_p'].append(jnp.concatenate([csa_p, csb_p], axis=-1))
        outs['conv_s'].append(jnp.concatenate([csa_s, csb_s], axis=-1))

    st = lambda k: jnp.stack(outs[k], axis=1)
    return (xp.reshape(nb_p, seq, D_MODEL), xs.reshape(nb_s, ts, D_MODEL),
            st('kv_p'), st('kv_s'), st('win_p'), st('win_s'), st('ret_p'), st('ret_s'), st('conv_p'), st('conv_s'))
```

```python
import functools

import numpy as np
import jax
import jax.numpy as jnp
from jax import lax
from jax.experimental import pallas as pl
from jax.experimental.pallas import tpu as pltpu

F32 = jnp.float32
BF16 = jnp.bfloat16

D_MODEL = 1024
PAGE = 128
HD = 64
N_KV = 2
REP = 4
D_NSA = 512
CMP_STRIDE = 16
SEL_BLOCK = 64
N_SEL = 16
WINDOW = 512
Q_TILE = 512
KV_TILE = 1024
FORCE_BONUS = 100.0
N_RET = 4
DK_RET = 64
DV_RET = 128
D_RET = 512
RET_CHUNK = 128
D_FF = 2816
ROPE_THETA = 10000.0
EPS = 1e-6
NEG_INF = -1e30
LOG2_E = 1.4426950408889634
LANES = 128
W_IN_COLS = 3072
VMEM_LIMIT = 56 * 1024 * 1024


def _cparams(*sem):
    return pltpu.CompilerParams(dimension_semantics=sem, vmem_limit_bytes=VMEM_LIMIT)


def _dot(a, b):
    return jnp.dot(a, b, preferred_element_type=F32)


def _dot_nt(a, b):
    return lax.dot_general(a, b, (((1,), (1,)), ((), ())), preferred_element_type=F32)


def _dot_tn(a, b):
    return lax.dot_general(a, b, (((0,), (0,)), ((), ())), preferred_element_type=F32)


def _dot_split3(p, m_bf16):
    hi = p.astype(BF16)
    r1 = p - hi.astype(F32)
    mid = r1.astype(BF16)
    lo = (r1 - mid.astype(F32)).astype(BF16)
    return _dot(hi, m_bf16) + _dot(mid, m_bf16) + _dot(lo, m_bf16)


def _rms(x, g):
    return x * lax.rsqrt(jnp.mean(x * x, axis=-1, keepdims=True) + EPS) * g


def _masked_softmax(s, mask):
    s = jnp.where(mask, s, NEG_INF)
    m = jnp.max(s, axis=-1, keepdims=True)
    e = jnp.where(mask, jnp.exp(s - m), 0.0)
    return e / jnp.maximum(jnp.sum(e, axis=-1, keepdims=True), 1e-30)


def _topk_mask_t(score_t, n_sel):
    nb = score_t.shape[0]
    blk = lax.broadcasted_iota(jnp.int32, score_t.shape, 0)

    def body(_, work):
        m = jnp.max(work, axis=0, keepdims=True)
        idx = jnp.min(jnp.where(work == m, blk, nb), axis=0, keepdims=True)
        return jnp.where(blk == idx, -jnp.inf, work)

    work = lax.fori_loop(0, n_sel, body, score_t)
    return jnp.where(work == -jnp.inf, jnp.where(score_t == -jnp.inf, 0.0, 1.0), 0.0)


def _selection(psum, mmat, qpos, ns, n_sel):
    imp = _dot_split3(psum, mmat)
    blk = lax.broadcasted_iota(jnp.int32, (1, imp.shape[1]), 1)
    qblk = qpos >> 6
    valid = (blk * SEL_BLOCK <= qpos) & (blk < ns)
    forced = (blk == 0) | (blk == qblk) | (blk == qblk - 1)
    score = jnp.where(valid, imp + jnp.where(forced, FORCE_BONUS, 0.0), NEG_INF)
    score = jnp.where(blk < ns, score, -jnp.inf)
    sel = _topk_mask_t(score.T, n_sel).T
    return jnp.where(valid, sel, 0.0)


def _mod_kernel(c_ref, w_ref, b_ref, o_ref):
    c = c_ref[...]
    o_ref[...] = _dot(c * jax.nn.sigmoid(c), w_ref[...]) + b_ref[...]


def _mod_call(c_all, w_ada, b_ada):
    n = c_all.shape[0]
    tn = 1536
    return pl.pallas_call(
        _mod_kernel,
        grid=(w_ada.shape[1] // tn,),
        in_specs=[pl.BlockSpec((n, D_MODEL), lambda j: (0, 0)),
                  pl.BlockSpec((D_MODEL, tn), lambda j: (0, j)),
                  pl.BlockSpec((1, tn), lambda j: (0, j))],
        out_specs=pl.BlockSpec((n, tn), lambda j: (0, j)),
        out_shape=jax.ShapeDtypeStruct((n, w_ada.shape[1]), F32),
        compiler_params=_cparams("arbitrary"),
        name="adaln_mod",
    )(c_all, w_ada, b_ada.reshape(1, -1))


def _inproj_kernel(x_ref, sc_ref, sh_ref, g1_ref, w_ref, cos_ref, sin_ref, qh_ref, kvn_ref, kvw_ref, *rest,
                   q_scale, tiles_per_batch):
    prompt = tiles_per_batch is not None
    if prompt:
        kaug_ref, vaug_ref, kwin_ref, vwaug_ref, ret_ref, gat_ref, kvc_ref = rest
        tm = x_ref.shape[0]
        pos = ((pl.program_id(0) % tiles_per_batch) * tm + lax.broadcasted_iota(jnp.int32, (tm, 1), 0))
        blk_onehot = jnp.where(lax.broadcasted_iota(jnp.int32, (1, LANES), 1) == (pos >> (SEL_BLOCK.bit_length() - 1)),
                               1.0, 0.0).astype(BF16)
        ones_cols = jnp.where(lax.broadcasted_iota(jnp.int32, (tm, HD), 1) == 0, 1.0, 0.0).astype(BF16)
    else:
        kvh_ref, ret_ref, gat_ref = rest
    h = (_rms(x_ref[...], g1_ref[...]) * (1.0 + sc_ref[...]) + sh_ref[...]).astype(BF16)
    cos = cos_ref[...]
    sin = sin_ref[...]
    lane = lax.broadcasted_iota(jnp.int32, (1, LANES), 1)
    first_half = (lane & (HD - 1)) < HD // 2

    def seg(c0, n):
        return _dot(h, w_ref[:, c0:c0 + n])

    def rope(a):
        sw = jnp.where(first_half, pltpu.roll(a, LANES - HD // 2, 1), pltpu.roll(a, HD // 2, 1))
        return a * cos + sw * sin

    def slab_pair(c0):
        z = seg(c0, 2 * LANES)
        return z[:, :LANES], z[:, LANES:]

    for j, q in enumerate(slab_pair(0) + slab_pair(2 * LANES)):
        q = rope(q) * q_scale
        qh_ref[2 * j] = q[:, :HD].astype(qh_ref.dtype)
        qh_ref[2 * j + 1] = q[:, HD:].astype(qh_ref.dtype)
    kv_slabs = slab_pair(512) + slab_pair(768) + slab_pair(1024)
    for s, a in enumerate(kv_slabs):
        if s % 2 == 0:
            a = rope(a)
        out_ref, s_out = (kvn_ref, s) if s < 4 else (kvw_ref, s - 4)
        if not prompt:
            out_ref[:, LANES * s_out:LANES * (s_out + 1)] = a
            if s >= 2:
                kvh_ref[2 * (s - 2)] = a[:, :HD].astype(kvh_ref.dtype)
                kvh_ref[2 * (s - 2) + 1] = a[:, HD:].astype(kvh_ref.dtype)
            continue
        out_ref[LANES * s_out:LANES * (s_out + 1), :] = a.T
        if s < 2:
            kvc_ref[s] = a
            continue
        for g in range(N_KV):
            part = a[:, HD * g:HD * (g + 1)].astype(BF16)
            if s == 2:
                kaug_ref[g] = jnp.concatenate([blk_onehot, part], axis=1)
            elif s == 3:
                vaug_ref[g] = jnp.concatenate([part, ones_cols], axis=1)
            elif s == 4:
                kwin_ref[g] = part
            else:
                vwaug_ref[g] = jnp.concatenate([part, ones_cols], axis=1)
    for j, (qr, kr) in enumerate(zip(slab_pair(1280), slab_pair(1536))):
        ret_ref[:, LANES * j:LANES * (j + 1)] = rope(qr) * (DK_RET ** -0.5)
        ret_ref[:, 256 + LANES * j:256 + LANES * (j + 1)] = rope(kr)
    ret_ref[:, 512:1024] = seg(1792, 512)
    ret_ref[:, 1024:1536] = seg(2304, 512)
    gat_ref[...] = seg(2816, 256)


def _inproj_call(x2d, sc3, sh3, mod_idx, g1, w_in_re, cos, sin, tab_idx, tm, q_scale, tiles_per_batch=None):
    m = x2d.shape[0]
    mrows = sc3.shape[1]
    rows = lambda width: pl.BlockSpec((tm, width), lambda i: (i, 0))
    slabs = lambda n, width: pl.BlockSpec((n, tm, width), lambda i: (0, i, 0))
    tail_specs = [rows(1536), rows(256)]
    tail_shapes = [jax.ShapeDtypeStruct((m, 1536), F32), jax.ShapeDtypeStruct((m, 256), F32)]
    if tiles_per_batch is not None:
        nb, seq = m // (tiles_per_batch * tm), tiles_per_batch * tm
        t_idx = lambda i: (i // tiles_per_batch, 0, i % tiles_per_batch)
        out_specs = ([slabs(8, HD), pl.BlockSpec((None, 512, tm), t_idx), pl.BlockSpec((None, 256, tm), t_idx),
                      slabs(N_KV, LANES + HD), slabs(N_KV, LANES), slabs(N_KV, HD), slabs(N_KV, LANES)]
                     + tail_specs + [slabs(2, LANES)])
        out_shape = ([jax.ShapeDtypeStruct((8, m, HD), BF16),
                      jax.ShapeDtypeStruct((nb, 512, seq), F32), jax.ShapeDtypeStruct((nb, 256, seq), F32),
                      jax.ShapeDtypeStruct((N_KV, m, LANES + HD), BF16), jax.ShapeDtypeStruct((N_KV, m, LANES), BF16),
                      jax.ShapeDtypeStruct((N_KV, m, HD), BF16), jax.ShapeDtypeStruct((N_KV, m, LANES), BF16)]
                     + tail_shapes + [jax.ShapeDtypeStruct((2, m, LANES), F32)])
    else:
        out_specs = [slabs(8, HD), rows(512), rows(256), slabs(8, HD)] + tail_specs
        out_shape = [jax.ShapeDtypeStruct((8, m, HD), F32), jax.ShapeDtypeStruct((m, 512), F32),
                     jax.ShapeDtypeStruct((m, 256), F32), jax.ShapeDtypeStruct((8, m, HD), F32)] + tail_shapes
    return pl.pallas_call(
        functools.partial(_inproj_kernel, q_scale=q_scale, tiles_per_batch=tiles_per_batch),
        grid=(m // tm,),
        in_specs=[pl.BlockSpec((tm, D_MODEL), lambda i: (i, 0)),
                  pl.BlockSpec((None, mrows, D_MODEL), lambda i: (mod_idx(i), 0, 0)),
                  pl.BlockSpec((None, mrows, D_MODEL), lambda i: (mod_idx(i), 0, 0)),
                  pl.BlockSpec((1, D_MODEL), lambda i: (0, 0)),
                  pl.BlockSpec((D_MODEL, W_IN_COLS), lambda i: (0, 0)),
                  pl.BlockSpec((tm, LANES), lambda i: (tab_idx(i), 0)),
                  pl.BlockSpec((tm, LANES), lambda i: (tab_idx(i), 0))],
        out_specs=out_specs,
        out_shape=out_shape,
        compiler_params=_cparams("parallel"),
        name="in_proj",
    )(x2d, sc3, sh3, g1, w_in_re, cos, sin)


def _cmp_partial(slab, wk_ref, wv_ref, pos_ref, ab_ref):
    for kind, w_ref in ((0, wk_ref), (1, wv_ref)):
        xcat = jnp.concatenate([slab(t, kind) for t in range(CMP_STRIDE)], axis=1).astype(BF16)
        r = _dot(xcat, w_ref[...])
        bias = _dot(pos_ref[kind], w_ref[...])
        ab_ref[:, LANES * kind:LANES * (kind + 1)] = r[:, :LANES] + bias[0:1, :LANES]
        ab_ref[:, 256 + LANES * kind:256 + LANES * (kind + 1)] = r[:, LANES:] + bias[1:2, LANES:]


def _cmp_prompt_kernel(x_ref, wk_ref, wv_ref, pos_ref, ab_ref):
    rows = ab_ref.shape[0]
    _cmp_partial(lambda t, kind: x_ref[kind, pl.ds(t, rows, stride=CMP_STRIDE), :],
                 wk_ref, wv_ref, pos_ref, ab_ref)


def _cmp_prompt_call(kvc, wk, wv, pos):
    rows = kvc.shape[1] // CMP_STRIDE
    tr = min(128, rows)
    return pl.pallas_call(
        _cmp_prompt_kernel,
        grid=(rows // tr,),
        in_specs=[pl.BlockSpec((2, tr * CMP_STRIDE, LANES), lambda i: (0, i, 0)),
                  pl.BlockSpec(wk.shape, lambda i: (0, 0)),
                  pl.BlockSpec(wv.shape, lambda i: (0, 0)),
                  pl.BlockSpec(pos.shape, lambda i: (0, 0, 0))],
        out_specs=pl.BlockSpec((tr, 512), lambda i: (i, 0)),
        out_shape=jax.ShapeDtypeStruct((rows, 512), F32),
        compiler_params=_cparams("parallel"),
        name="cmp_partial_prompt",
    )(kvc, wk, wv, pos)


PAGES_PER_STEP = 16


def _cmp_sample_kernel(pt_ref, *refs):
    pages = refs[:PAGES_PER_STEP]
    perm_ref, wk_ref, wv_ref, pos_ref, ab_ref, ks_ref, x_sc = refs[PAGES_PER_STEP:]
    chunks = PAGE // CMP_STRIDE
    for p, page in enumerate(pages):
        x_sc[p] = _dot_nt(perm_ref[...], page[0:256, :].astype(BF16))
        ks_ref[:, PAGE * p:PAGE * (p + 1)] = page[256:512, :].astype(BF16)

    def slab(t, kind):
        return jnp.concatenate([x_sc[p, chunks * t:chunks * (t + 1), LANES * kind:LANES * (kind + 1)]
                                for p in range(PAGES_PER_STEP)], axis=0)

    _cmp_partial(slab, wk_ref, wv_ref, pos_ref, ab_ref)


def _cmp_sample_call(cache_t, page_table, wk, wv, pos):
    nb, n_pages = page_table.shape
    chunks = PAGE // CMP_STRIDE
    steps = n_pages // PAGES_PER_STEP

    def page_spec(k):
        return pl.BlockSpec((None, 512, PAGE), lambda b, j, pt: (pt[b, j * PAGES_PER_STEP + k], 0, 0))

    grid_spec = pltpu.PrefetchScalarGridSpec(
        num_scalar_prefetch=1,
        grid=(nb, steps),
        in_specs=[page_spec(k) for k in range(PAGES_PER_STEP)] + [
            pl.BlockSpec((PAGE, PAGE), lambda b, j, pt: (0, 0)),
            pl.BlockSpec(wk.shape, lambda b, j, pt: (0, 0)),
            pl.BlockSpec(wv.shape, lambda b, j, pt: (0, 0)),
            pl.BlockSpec(pos.shape, lambda b, j, pt: (0, 0, 0))],
        out_specs=[pl.BlockSpec((None, PAGES_PER_STEP * chunks, 512), lambda b, j, pt: (b, j, 0)),
                   pl.BlockSpec((None, 256, PAGES_PER_STEP * PAGE), lambda b, j, pt: (b, 0, j))],
        scratch_shapes=[pltpu.VMEM((PAGES_PER_STEP, PAGE, 256), F32)],
    )
    tok = np.arange(PAGE)
    perm = np.zeros((PAGE, PAGE), np.float32)
    perm[(tok % CMP_STRIDE) * chunks + tok // CMP_STRIDE, tok] = 1.0
    return pl.pallas_call(
        _cmp_sample_kernel,
        grid_spec=grid_spec,
        out_shape=[jax.ShapeDtypeStruct((nb, n_pages * chunks, 512), F32),
                   jax.ShapeDtypeStruct((nb, 256, n_pages * PAGE), BF16)],
        compiler_params=_cparams("parallel", "arbitrary"),
        name="cmp_partial_sample",
    )(page_table, *([cache_t] * PAGES_PER_STEP), jnp.asarray(perm, dtype=BF16), wk, wv, pos)


def _cmp_fin_kernel(ab_ref, w2_ref, o_ref):
    nc = ab_ref.shape[0]
    a = ab_ref[:, :256]
    b_next = pltpu.roll(ab_ref[:, 256:], nc - 1, 0)
    hid = jax.nn.gelu(a + b_next)
    out = _dot(hid.astype(BF16), w2_ref[...])
    row = lax.broadcasted_iota(jnp.int32, (nc, 1), 0)
    out = jnp.where(row < nc - 1, out, 0.0)
    for s in range(4):
        o_ref[s] = out[:, HD * s:HD * (s + 1)]


def _cmp_fin_call(ab, w2bd):
    nb, nc, _ = ab.shape
    return pl.pallas_call(
        _cmp_fin_kernel,
        grid=(nb,),
        in_specs=[pl.BlockSpec((None, nc, 512), lambda b: (b, 0, 0)),
                  pl.BlockSpec((256, 256), lambda b: (0, 0))],
        out_specs=pl.BlockSpec((None, 4, nc, HD), lambda b: (b, 0, 0, 0)),
        out_shape=jax.ShapeDtypeStruct((nb, 4, nc, HD), F32),
        compiler_params=_cparams("parallel"),
        name="cmp_finish",
    )(ab, w2bd)


def _nsa_prompt_kernel(q_ref, gat_ref, kc_ref, vc_ref, ka_ref, va_ref, kw_ref, vw_ref, mmat_ref,
                       o_ref, qa_sc, s_sc, mcur_sc, m_sc, acc_sc, *, ns, n_sel):
    i = pl.program_id(2)
    start = i * Q_TILE
    qpos = start + lax.broadcasted_iota(jnp.int32, (Q_TILE, 1), 0)
    nc = kc_ref.shape[0]
    head_rows =[slice(r * Q_TILE, (r + 1) * Q_TILE) for r in range(REP)]

    kc = kc_ref[...].astype(BF16)
    vc = vc_ref[...].astype(BF16)
    c_end = lax.broadcasted_iota(jnp.int32, (1, nc), 1) * CMP_STRIDE + (2 * CMP_STRIDE - 1)
    cbias = jnp.where(c_end <= qpos, 0.0, NEG_INF)
    any_valid = qpos >= 2 * CMP_STRIDE - 1
    psum = jnp.zeros((Q_TILE, nc), F32)
    o_cmp = []
    for r in range(REP):
        s = _dot_nt(q_ref[r], kc) + cbias
        e = jnp.exp2(s - jnp.max(s, axis=-1, keepdims=True))
        norm = jnp.where(any_valid, 1.0 / jnp.maximum(jnp.sum(e, axis=-1, keepdims=True), 1e-30), 0.0)
        p = e * norm
        psum = psum + p
        o_cmp.append(_dot(p.astype(BF16), vc))

    sel = _selection(psum, mmat_ref[...], qpos, ns, n_sel)
    selneg = jnp.where(sel > 0.5, 0.0, NEG_INF).astype(BF16)
    for r in range(REP):
        qa_sc[head_rows[r], :] = jnp.concatenate([selneg, q_ref[r]], axis=1)

    base = pl.multiple_of(jnp.maximum(start - WINDOW, 0), Q_TILE)
    kw = kw_ref[pl.ds(base, WINDOW + Q_TILE), :]
    vw = vw_ref[pl.ds(base, WINDOW + Q_TILE), :]
    dpos = qpos - (base + lax.broadcasted_iota(jnp.int32, (1, WINDOW + Q_TILE), 1))
    wbias = jnp.where((dpos >= 0) & (dpos < WINDOW), 0.0, NEG_INF)
    o_win = []
    for r in range(REP):
        s = _dot_nt(q_ref[r], kw) + wbias
        e = jnp.exp2((s - jnp.max(s, axis=-1, keepdims=True)).astype(BF16))
        ow = _dot(e, vw)
        o_win.append(ow[:, :HD] / jnp.maximum(ow[:, HD:HD + 1], 1e-30))

    n_full = start // KV_TILE

    def scores(j):
        k0 = pl.multiple_of(j * KV_TILE, KV_TILE)
        return _dot_nt(qa_sc[...], ka_ref[pl.ds(k0, KV_TILE), :])

    def consume(slot, t):
        v0 = pl.multiple_of(t * KV_TILE, KV_TILE)
        v_t = va_ref[pl.ds(v0, KV_TILE), :]
        for r in range(REP):
            rows = head_rows[r]
            m_old = m_sc[rows]
            m_new = jnp.maximum(m_old, mcur_sc[slot, rows])
            p = jnp.exp2((s_sc[slot, rows] - m_new).astype(BF16))
            acc_sc[rows] = jnp.exp2(m_old - m_new) * acc_sc[rows] + _dot(p, v_t)
            m_sc[rows] = m_new

    kpos = n_full * KV_TILE + lax.broadcasted_iota(jnp.int32, (1, KV_TILE), 1)
    causal_bias = jnp.where(kpos <= qpos, 0.0, NEG_INF)
    s_diag = scores(n_full)
    for r in range(REP):
        s_r = s_diag[head_rows[r]] + causal_bias
        s_sc[0, head_rows[r]] = s_r
        mcur_sc[0, head_rows[r]] = jnp.max(s_r, axis=-1, keepdims=True)
    m_sc[...] = jnp.full(m_sc.shape, NEG_INF, F32)
    acc_sc[...] = jnp.zeros(acc_sc.shape, F32)

    def body(j, carry):
        consume(j & 1, jnp.where(j == 0, n_full, j - 1))
        s_new = scores(j)
        slot_new = (j + 1) & 1
        s_sc[slot_new] = s_new
        mcur_sc[slot_new] = jnp.max(s_new, axis=-1, keepdims=True)
        return carry

    lax.fori_loop(0, n_full, body, 0)
    consume(n_full & 1, jnp.maximum(n_full - 1, 0))

    gate = jax.nn.sigmoid(gat_ref[...])
    for r in range(REP):
        acc = acc_sc[head_rows[r]]
        o_slc = acc[:, :HD] / jnp.maximum(acc[:, HD:HD + 1], 1e-30)
        o_ref[:, HD * r:HD * (r + 1)] = (gate[:, r:r + 1] * o_cmp[r] + gate[:, REP + r:REP + r + 1] * o_slc
                                        + gate[:, 2 * REP + r:2 * REP + r + 1] * o_win[r])


def _nsa_prompt_call(qh, gat, cmp, kaug, vaug, kwin, vwaug, mmat, nb, seq, cmp_off):
    nq = seq // Q_TILE
    nc = cmp.shape[2]
    ns = seq // SEL_BLOCK
    assert ns <= LANES and mmat.shape[1] == LANES
    kern = functools.partial(_nsa_prompt_kernel, ns=ns, n_sel=min(N_SEL, ns))

    def kv_spec(width):
        return pl.BlockSpec((None, seq, width), lambda b, g, i: (g, b, 0))

    return pl.pallas_call(
        kern,
        grid=(nb, N_KV, nq),
        in_specs=[pl.BlockSpec((REP, Q_TILE, HD), lambda b, g, i: (g, b * nq + i, 0)),
                  pl.BlockSpec((Q_TILE, LANES), lambda b, g, i: (b * nq + i, g)),
                  pl.BlockSpec((None, None, nc, HD), lambda b, g, i: (cmp_off + b, g, 0, 0)),
                  pl.BlockSpec((None, None, nc, HD), lambda b, g, i: (cmp_off + b, 2 + g, 0, 0)),
                  kv_spec(LANES + HD), kv_spec(LANES), kv_spec(HD), kv_spec(LANES),
                  pl.BlockSpec(mmat.shape, lambda b, g, i: (0, 0))],
        out_specs=pl.BlockSpec((Q_TILE, REP * HD), lambda b, g, i: (b * nq + i, g)),
        out_shape=jax.ShapeDtypeStruct((nb * seq, D_NSA), F32),
        scratch_shapes=[pltpu.VMEM((REP * Q_TILE, LANES + HD), BF16),
                        pltpu.VMEM((2, REP * Q_TILE, KV_TILE), F32),
                        pltpu.VMEM((2, REP * Q_TILE, 1), F32),
                        pltpu.VMEM((REP * Q_TILE, 1), F32),
                        pltpu.VMEM((REP * Q_TILE, LANES), F32)],
        compiler_params=_cparams("parallel", "parallel", "arbitrary"),
        name="nsa_prompt",
    )(qh, gat, cmp, cmp, kaug, vaug, kwin, vwaug, mmat)


def _nsa_sample_kernel(q_ref, gat_ref, cmp_ref, ks_ref, new_ref, win_ref, mmat_ref, oht_ref, o_ref,
                       *, past, ts, ns, n_sel):
    nc = cmp_ref.shape[1]
    wb = win_ref.shape[1]
    t_idx = lax.broadcasted_iota(jnp.int32, (ts, 1), 0)
    qpos = jnp.concatenate([past + t_idx] * REP, axis=0)
    gate = jax.nn.sigmoid(gat_ref[...])
    pad_rows = jnp.zeros((LANES - ts, HD), F32)

    def padded(slab):
        return jnp.concatenate([new_ref[slab], pad_rows], axis=0).astype(BF16)

    c_end = lax.broadcasted_iota(jnp.int32, (1, nc), 1) * CMP_STRIDE + (2 * CMP_STRIDE - 1)
    qs, o_cmps, psums = [], [], []
    for g in range(N_KV):
        q = jnp.concatenate([q_ref[REP * g + r] for r in range(REP)], axis=0).astype(BF16)
        p = _masked_softmax(_dot_nt(q, cmp_ref[g].astype(BF16)), c_end <= qpos)
        o_cmps.append(_dot(p.astype(BF16), cmp_ref[2 + g].astype(BF16)))
        psum = p[0:ts]
        for r in range(1, REP):
            psum = psum + p[r * ts:(r + 1) * ts]
        qs.append(q)
        psums.append(psum)
    psum_all = jnp.concatenate(psums + [jnp.zeros((LANES - N_KV * ts, nc), F32)], axis=0)
    row = lax.broadcasted_iota(jnp.int32, (LANES, 1), 0)
    sel_all = _selection(psum_all, mmat_ref[...], past + lax.rem(row, ts), ns, n_sel)

    kidx = lax.broadcasted_iota(jnp.int32, (1, LANES), 1)
    kpos_w = past - wb + lax.broadcasted_iota(jnp.int32, (1, wb + LANES), 1)
    dpos_w = qpos - kpos_w
    wmask = (dpos_w >= 0) & (dpos_w < WINDOW) & (kpos_w >= 0)
    for g in range(N_KV):
        q = qs[g]
        sel = jnp.concatenate([sel_all[g * ts:(g + 1) * ts]] * REP, axis=0)
        selneg = jnp.where(sel[:, :LANES] > 0.5, 0.0, NEG_INF).astype(BF16)
        s_old = _dot(q, ks_ref[HD * g:HD * (g + 1), :]) + _dot(selneg, oht_ref[...])
        new_bias = jnp.where((sel[:, ns - 1:ns] > 0.5) & (past + kidx <= qpos), 0.0, NEG_INF)
        s_new = _dot_nt(q, padded(g)) + new_bias
        m = jnp.maximum(jnp.max(s_old, axis=-1, keepdims=True), jnp.max(s_new, axis=-1, keepdims=True))
        e_old = jnp.exp(s_old - m)
        e_new = jnp.exp(s_new - m)
        denom = jnp.sum(e_old, axis=-1, keepdims=True) + jnp.sum(e_new, axis=-1, keepdims=True)
        o_slc = (_dot_nt(e_old.astype(BF16), ks_ref[128 + HD * g:128 + HD * (g + 1), :])
                 + _dot(e_new.astype(BF16), padded(2 + g))) / jnp.maximum(denom, 1e-30)

        s_w = jnp.concatenate([_dot(q, win_ref[HD * g:HD * (g + 1), :].astype(BF16)),
                               _dot_nt(q, padded(4 + g))], axis=1)
        pw = _masked_softmax(s_w, wmask)
        o_win = (_dot_nt(pw[:, :wb].astype(BF16), win_ref[128 + HD * g:128 + HD * (g + 1), :].astype(BF16))
                 + _dot(pw[:, wb:].astype(BF16), padded(6 + g)))

        for r in range(REP):
            rs = slice(r * ts, (r + 1) * ts)
            c = LANES * g + r
            o_ref[:, HD * (REP * g + r):HD * (REP * g + r + 1)] = (
                gate[:, c:c + 1] * o_cmps[g][rs] + gate[:, c + REP:c + REP + 1] * o_slc[rs]
                + gate[:, c + 2 * REP:c + 2 * REP + 1] * o_win[rs])


def _nsa_sample_call(qh, gat, cmp, ks_t, kvh_new, win_t, mmat, oht, nb, ts, past, cmp_off):
    nc = cmp.shape[2]
    ns = past // SEL_BLOCK + 1
    assert ns - 1 <= LANES and N_KV * ts <= LANES
    kern = functools.partial(_nsa_sample_kernel, past=past, ts=ts, ns=ns, n_sel=min(N_SEL, ns))
    return pl.pallas_call(
        kern,
        grid=(nb,),
        in_specs=[pl.BlockSpec((8, ts, HD), lambda b: (0, b, 0)),
                  pl.BlockSpec((ts, 256), lambda b: (b, 0)),
                  pl.BlockSpec((None, 4, nc, HD), lambda b: (cmp_off + b, 0, 0, 0)),
                  pl.BlockSpec((None, 256, past), lambda b: (b, 0, 0)),
                  pl.BlockSpec((8, ts, HD), lambda b: (0, b, 0)),
                  pl.BlockSpec((None, 256, win_t.shape[2]), lambda b: (b, 0, 0)),
                  pl.BlockSpec(mmat.shape, lambda b: (0, 0)),
                  pl.BlockSpec(oht.shape, lambda b: (0, 0))],
        out_specs=pl.BlockSpec((ts, D_NSA), lambda b: (b, 0)),
        out_shape=jax.ShapeDtypeStruct((nb * ts, D_NSA), F32),
        compiler_params=_cparams("parallel"),
        name="nsa_sample",
    )(qh, gat, cmp, ks_t, kvh_new, win_t, mmat, oht)


def _ret_kernel(x_ref, s0_ref, dmat_ref, xi_ref, zeta_ref, gc_ref, gro_ref, o_ref, s_out_ref, s_sc, pad_sc,
                *, rows):
    @pl.when(pl.program_id(1) == 0)
    def _():
        s_sc[...] = s0_ref[...]

    if rows < RET_CHUNK:
        pad_sc[...] = jnp.zeros(pad_sc.shape, F32)
        pad_sc[0:rows, :] = x_ref[...]
        x = pad_sc
    else:
        x = x_ref
    for c in range(max(rows // RET_CHUNK, 1)):
        r0 = RET_CHUNK * c
        for h in range(N_RET):
            q = x[r0:r0 + RET_CHUNK, DK_RET * h:DK_RET * (h + 1)].astype(BF16)
            k = x[r0:r0 + RET_CHUNK, 256 + DK_RET * h:256 + DK_RET * (h + 1)]
            v = x[r0:r0 + RET_CHUNK, 512 + DV_RET * h:512 + DV_RET * (h + 1)].astype(BF16)
            gr = x[r0:r0 + RET_CHUNK, 1024 + DV_RET * h:1024 + DV_RET * (h + 1)]
            att = _dot_nt(q, k.astype(BF16)) * dmat_ref[h]
            s_old = s_sc[h]
            o = _dot(att.astype(BF16), v) + _dot(q, s_old.astype(BF16)) * xi_ref[h]
            s_sc[h] = gc_ref[h] * s_old + _dot_tn((k * zeta_ref[h]).astype(BF16), v)
            mu = jnp.mean(o, axis=-1, keepdims=True)
            var = jnp.mean(jnp.square(o - mu), axis=-1, keepdims=True)
            y = (o - mu) * lax.rsqrt(var + EPS) * gro_ref[:, DV_RET * h:DV_RET * (h + 1)]
            res = gr * jax.nn.sigmoid(gr) * y
            n_out = min(rows, RET_CHUNK)
            o_ref[r0:r0 + n_out, DV_RET * h:DV_RET * (h + 1)] = res[0:n_out]
    s_out_ref[...] = s_sc[...]


def _ret_call(ret, s0, tabs, g_ret_out, nb, seq, rows):
    nchunks = seq // rows
    dmat, xi, zeta, gc = tabs
    kern = functools.partial(_ret_kernel, rows=rows)
    full3 = lambda b, c: (0, 0, 0)
    return pl.pallas_call(
        kern,
        grid=(nb, nchunks),
        in_specs=[pl.BlockSpec((rows, 1536), lambda b, c: (b * nchunks + c, 0)),
                  pl.BlockSpec((None, N_RET, DK_RET, DV_RET), lambda b, c: (b, 0, 0, 0)),
                  pl.BlockSpec(dmat.shape, full3), pl.BlockSpec(xi.shape, full3),
                  pl.BlockSpec(zeta.shape, full3), pl.BlockSpec(gc.shape, full3),
                  pl.BlockSpec((1, D_RET), lambda b, c: (0, 0))],
        out_specs=[pl.BlockSpec((rows, D_RET), lambda b, c: (b * nchunks + c, 0)),
                   pl.BlockSpec((None, N_RET, DK_RET, DV_RET), lambda b, c: (b, 0, 0, 0))],
        out_shape=[jax.ShapeDtypeStruct((nb * seq, D_RET), F32),
                   jax.ShapeDtypeStruct((nb, N_RET, DK_RET, DV_RET), F32)],
        scratch_shapes=[pltpu.VMEM((N_RET, DK_RET, DV_RET), F32),
                        pltpu.VMEM((RET_CHUNK, 1536), F32)],
        compiler_params=_cparams("parallel", "arbitrary"),
        name="retention",
    )(ret, s0, dmat, xi, zeta, gc, g_ret_out)


def _ret_tables(chunk):
    c = RET_CHUNK
    log_g = jnp.log1p(-jnp.power(2.0, -5.0 - jnp.arange(N_RET, dtype=F32)))
    i = jnp.arange(c, dtype=F32)
    diff = i[:, None] - i[None, :]
    dmat = jnp.where(diff >= 0, jnp.exp(jnp.maximum(diff, 0.0)[None] * log_g[:, None, None]), 0.0)
    xi = jnp.exp((i[None, :] + 1.0) * log_g[:, None])
    zeta = jnp.where(i[None, :] < chunk, jnp.exp((chunk - 1.0 - i)[None, :] * log_g[:, None]), 0.0)
    g_c = jnp.exp(chunk * log_g)
    return (dmat,
            jnp.broadcast_to(xi[:, :, None], (N_RET, c, DV_RET)),
            jnp.broadcast_to(zeta[:, :, None], (N_RET, c, DK_RET)),
            jnp.broadcast_to(g_c[:, None, None], (N_RET, DK_RET, DV_RET)))


def _outproj_kernel(on_ref, or_ref, x_ref, gt_ref, sc_ref, sh_ref, gn_ref, g2_ref, w_ref, x1_ref, h2_ref):
    a = _rms(on_ref[...], gn_ref[...])
    mix = _dot(a.astype(BF16), w_ref[0:D_NSA, :]) + _dot(or_ref[...].astype(BF16), w_ref[D_NSA:, :])
    x1 = x_ref[...] + gt_ref[...] * mix
    x1_ref[...] = x1
    h2_ref[...] = (_rms(x1, g2_ref[...]) * (1.0 + sc_ref[...]) + sh_ref[...]).astype(h2_ref.dtype)


def _outproj_call(o_nsa, o_ret, x2d, gt3, sc3, sh3, mod_idx, g_nsa, g2, w_out, tm, h_dtype):
    m = x2d.shape[0]
    mrows = gt3.shape[1]
    mod_spec = pl.BlockSpec((None, mrows, D_MODEL), lambda i: (mod_idx(i), 0, 0))
    return pl.pallas_call(
        _outproj_kernel,
        grid=(m // tm,),
        in_specs=[pl.BlockSpec((tm, D_NSA), lambda i: (i, 0)),
                  pl.BlockSpec((tm, D_RET), lambda i: (i, 0)),
                  pl.BlockSpec((tm, D_MODEL), lambda i: (i, 0)),
                  mod_spec, mod_spec, mod_spec,
                  pl.BlockSpec((1, D_NSA), lambda i: (0, 0)),
                  pl.BlockSpec((1, D_MODEL), lambda i: (0, 0)),
                  pl.BlockSpec((D_NSA + D_RET, D_MODEL), lambda i: (0, 0))],
        out_specs=[pl.BlockSpec((tm, D_MODEL), lambda i: (i, 0)),
                   pl.BlockSpec((tm, D_MODEL), lambda i: (i, 0))],
        out_shape=[jax.ShapeDtypeStruct((m, D_MODEL), F32),
                   jax.ShapeDtypeStruct((m, D_MODEL), h_dtype)],
        compiler_params=_cparams("parallel"),
        name="out_proj",
    )(o_nsa, o_ret, x2d, gt3, sc3, sh3, g_nsa, g2, w_out)


FF_TILE = D_FF // 2


def _ffn_up_kernel(h_ref, wa_ref, wb_ref, cwa_ref, cwb_ref, cba_ref, cbb_ref, sta_ref, stb_ref,
                   act_ref, csa_ref, csb_ref, prev_a, prev_b):
    @pl.when(pl.program_id(2) == 0)
    def _():
        prev_a[...] = sta_ref[...]
        prev_b[...] = stb_ref[...]

    h = h_ref[...].astype(BF16)
    tm = h.shape[0]
    row = lax.broadcasted_iota(jnp.int32, (tm, 1), 0)

    def half(w_ref, cw_ref, cb_ref, prev, cs_ref):
        u = _dot(h, w_ref[...])
        p = prev[...]
        back1 = pltpu.roll(u, 1, 0)
        back2 = pltpu.roll(u, 2, 0)
        u1 = jnp.where(row >= 1, back1, p[1:2])
        u2 = jnp.where(row >= 2, back2, jnp.where(row == 1, p[1:2], p[0:1]))
        y = cb_ref[...] + cw_ref[0:1] * u2 + cw_ref[1:2] * u1 + cw_ref[2:3] * u
        tail = back2[0:2]
        prev[...] = tail
        cs_ref[...] = tail
        return y

    a = half(wa_ref, cwa_ref, cba_ref, prev_a, csa_ref)
    b = half(wb_ref, cwb_ref, cbb_ref, prev_b, csb_ref)
    act_ref[...] = (a * jax.nn.sigmoid(a) * b).astype(act_ref.dtype)


def _ffn_up_call(h2, w_up_a, w_up_b, conv_w, conv_b, conv_state, nb, seq, tm, act_dtype):
    nrt = seq // tm
    nt = D_FF // FF_TILE
    return pl.pallas_call(
        _ffn_up_kernel,
        grid=(nt, nb, nrt),
        in_specs=[pl.BlockSpec((tm, D_MODEL), lambda j, b, i: (b * nrt + i, 0)),
                  pl.BlockSpec((D_MODEL, FF_TILE), lambda j, b, i: (0, j)),
                  pl.BlockSpec((D_MODEL, FF_TILE), lambda j, b, i: (0, j)),
                  pl.BlockSpec((3, FF_TILE), lambda j, b, i: (0, j)),
                  pl.BlockSpec((3, FF_TILE), lambda j, b, i: (0, nt + j)),
                  pl.BlockSpec((1, FF_TILE), lambda j, b, i: (0, j)),
                  pl.BlockSpec((1, FF_TILE), lambda j, b, i: (0, nt + j)),
                  pl.BlockSpec((None, 2, FF_TILE), lambda j, b, i: (b, 0, j)),
                  pl.BlockSpec((None, 2, FF_TILE), lambda j, b, i: (b, 0, nt + j))],
        out_specs=[pl.BlockSpec((tm, FF_TILE), lambda j, b, i: (b * nrt + i, j)),
                   pl.BlockSpec((None, 2, FF_TILE), lambda j, b, i: (b, 0, j)),
                   pl.BlockSpec((None, 2, FF_TILE), lambda j, b, i: (b, 0, j))],
        out_shape=[jax.ShapeDtypeStruct((nb * seq, D_FF), act_dtype),
                   jax.ShapeDtypeStruct((nb, 2, D_FF), F32),
                   jax.ShapeDtypeStruct((nb, 2, D_FF), F32)],
        scratch_shapes=[pltpu.VMEM((2, FF_TILE), F32), pltpu.VMEM((2, FF_TILE), F32)],
        compiler_params=_cparams("parallel", "parallel", "arbitrary"),
        name="ffn_up_conv",
    )(h2, w_up_a, w_up_b, conv_w, conv_w, conv_b, conv_b, conv_state, conv_state)


def _ffn_up_decode_kernel(h_ref, wa_ref, wb_ref, cwa_ref, cwb_ref, cba_ref, cbb_ref, sta_ref, stb_ref,
                          act_ref, csa_ref, csb_ref, *, nb, ts):
    h = h_ref[...].astype(BF16)
    t = lax.broadcasted_iota(jnp.int32, (1, ts, 1), 1)

    def half(w_ref, cw_ref, cb_ref, st_ref, cs_ref):
        u = _dot(h, w_ref[...]).reshape(nb, ts, FF_TILE)
        p = st_ref[...]
        back1 = pltpu.roll(u, 1, 1)
        back2 = pltpu.roll(u, 2, 1)
        u1 = jnp.where(t >= 1, back1, p[:, 1:2])
        u2 = jnp.where(t >= 2, back2, jnp.where(t == 1, p[:, 1:2], p[:, 0:1]))
        y = cb_ref[...] + cw_ref[0:1] * u2 + cw_ref[1:2] * u1 + cw_ref[2:3] * u
        cs_ref[...] = back2[:, 0:2]
        return y

    a = half(wa_ref, cwa_ref, cba_ref, sta_ref, csa_ref)
    b = half(wb_ref, cwb_ref, cbb_ref, stb_ref, csb_ref)
    act_ref[...] = (a * jax.nn.sigmoid(a) * b).reshape(nb * ts, FF_TILE)


def _ffn_up_decode_call(h2, w_up_a, w_up_b, conv_w, conv_b, conv_state, nb, ts):
    nt = D_FF // FF_TILE
    m = nb * ts
    return pl.pallas_call(
        functools.partial(_ffn_up_decode_kernel, nb=nb, ts=ts),
        grid=(nt,),
        in_specs=[pl.BlockSpec((m, D_MODEL), lambda j: (0, 0)),
                  pl.BlockSpec((D_MODEL, FF_TILE), lambda j: (0, j)),
                  pl.BlockSpec((D_MODEL, FF_TILE), lambda j: (0, j)),
                  pl.BlockSpec((3, FF_TILE), lambda j: (0, j)),
                  pl.BlockSpec((3, FF_TILE), lambda j: (0, nt + j)),
                  pl.BlockSpec((1, FF_TILE), lambda j: (0, j)),
                  pl.BlockSpec((1, FF_TILE), lambda j: (0, nt + j)),
                  pl.BlockSpec((nb, 2, FF_TILE), lambda j: (0, 0, j)),
                  pl.BlockSpec((nb, 2, FF_TILE), lambda j: (0, 0, nt + j))],
        out_specs=[pl.BlockSpec((m, FF_TILE), lambda j: (0, j)),
                   pl.BlockSpec((nb, 2, FF_TILE), lambda j: (0, 0, j)),
                   pl.BlockSpec((nb, 2, FF_TILE), lambda j: (0, 0, j))],
        out_shape=[jax.ShapeDtypeStruct((m, D_FF), F32),
                   jax.ShapeDtypeStruct((nb, 2, D_FF), F32),
                   jax.ShapeDtypeStruct((nb, 2, D_FF), F32)],
        compiler_params=_cparams("parallel"),
        name="ffn_up_conv_decode",
    )(h2, w_up_a, w_up_b, conv_w, conv_w, conv_b, conv_b, conv_state, conv_state)


def _ffn_down_kernel(a_ref, x1_ref, gt_ref, w_ref, gf_ref, y_ref, *, final_norm):
    x2 = x1_ref[...] + gt_ref[...] * _dot(a_ref[...].astype(BF16), w_ref[...])
    y_ref[...] = _rms(x2, gf_ref[...]) if final_norm else x2


def _ffn_down_call(act, x1, gt3, mod_idx, w_down, g_final, tm, final_norm):
    m = x1.shape[0]
    mrows = gt3.shape[1]
    return pl.pallas_call(
        functools.partial(_ffn_down_kernel, final_norm=final_norm),
        grid=(m // tm,),
        in_specs=[pl.BlockSpec((tm, D_FF), lambda i: (i, 0)),
                  pl.BlockSpec((tm, D_MODEL), lambda i: (i, 0)),
                  pl.BlockSpec((None, mrows, D_MODEL), lambda i: (mod_idx(i), 0, 0)),
                  pl.BlockSpec((D_FF, D_MODEL), lambda i: (0, 0)),
                  pl.BlockSpec((1, D_MODEL), lambda i: (0, 0))],
        out_specs=pl.BlockSpec((tm, D_MODEL), lambda i: (i, 0)),
        out_shape=jax.ShapeDtypeStruct((m, D_MODEL), F32),
        compiler_params=_cparams("parallel"),
        name="ffn_down",
    )(act, x1, gt3, w_down, g_final)


def _relayout_w_in(w_in):
    q_n, kv_n, gate_n, q_r, k_r, v_r, g_r = jnp.split(w_in, [512, 1280, 1304, 1560, 1816, 2328], axis=1)
    gate = gate_n.reshape(D_MODEL, 3, N_KV, REP)
    gate = jnp.transpose(gate, (0, 2, 1, 3)).reshape(D_MODEL, N_KV, 3 * REP)
    gate = jnp.pad(gate, ((0, 0), (0, 0), (0, LANES - 3 * REP))).reshape(D_MODEL, N_KV * LANES)
    return jnp.concatenate([q_n, kv_n, q_r, k_r, v_r, g_r, gate], axis=1).astype(BF16)


def _relayout_cmp(w1, pos):
    w1r = w1.reshape(2, CMP_STRIDE, HD, HD)
    w = jnp.einsum('atdn,gh->tgdahn', w1r, jnp.eye(N_KV, dtype=w1.dtype)).reshape(CMP_STRIDE * N_KV * HD, 2 * N_KV * HD)
    posr = pos.reshape(2, CMP_STRIDE, 1, HD)
    prow = jnp.broadcast_to(posr, (2, CMP_STRIDE, N_KV, HD)).reshape(2, CMP_STRIDE * N_KV * HD)
    prow = jnp.pad(prow, ((0, 14), (0, 0)))
    return w.astype(BF16), prow.astype(BF16)


def _block_diag4(w2k, w2v):
    z = jnp.zeros((HD, HD), w2k.dtype)
    rows = [[w2k, z, z, z], [z, w2k, z, z], [z, z, w2v, z], [z, z, z, w2v]]
    return jnp.block(rows).astype(BF16)


def _rope_tables(pos):
    half = HD // 2
    inv = ROPE_THETA ** (-jnp.arange(half, dtype=F32) / half)
    ang = pos.astype(F32)[:, None] * inv[None, :]
    cos, sin = jnp.cos(ang), jnp.sin(ang)
    return jnp.tile(cos, (1, 4)), jnp.tile(jnp.concatenate([-sin, sin], axis=1), (1, 2))


def _importance_matrix(nc, nsp):
    n = np.arange(nc)[:, None]
    d = n - 4 * np.arange(nsp)[None, :]
    m = ((d >= 0) & (d <= 3)).astype(np.float32) + ((d >= -1) & (d <= 2)).astype(np.float32)
    return jnp.asarray(m, dtype=BF16)


def _block_onehot_t(past):
    blk = np.arange(LANES)[:, None]
    return jnp.asarray((blk == np.arange(past)[None, :] // SEL_BLOCK).astype(np.float32), dtype=BF16)


def _round_up(x, m):
    return (x + m - 1) // m * m


def kernel(x_prompt, x_sample, c_prompt, c_sample, cache_nsa_kv, cache_win_kv, state_ret, state_conv, page_table,
           w_ada, b_ada, g_norm1, w_in, cmp_pos_k, cmp_w1_k, cmp_w2_k, cmp_pos_v, cmp_w1_v, cmp_w2_v,
           g_nsa_out, g_ret_out, w_out, g_norm2, w_up, conv_w, conv_b, w_down, g_final):
    nb_p, seq, _ = x_prompt.shape
    nb_s, ts, _ = x_sample.shape
    depth = w_ada.shape[0]
    n_pages = page_table.shape[1]
    past = n_pages * PAGE
    wb = cache_win_kv.shape[2]
    m_p, m_s = nb_p * seq, nb_s * ts
    tm_p = 512

    xp = x_prompt.reshape(m_p, D_MODEL)
    xs = x_sample.reshape(m_s, D_MODEL)
    n_c = nb_p + nb_s
    c_all = jnp.pad(jnp.concatenate([c_prompt, c_sample], axis=0), ((0, _round_up(n_c, 8) - n_c), (0, 0)))

    cos_p, sin_p = _rope_tables(jnp.arange(seq, dtype=jnp.int32))
    cos_s, sin_s = _rope_tables(jnp.tile(past + jnp.arange(ts, dtype=jnp.int32), nb_s))
    tabs_p = _ret_tables(min(RET_CHUNK, seq))
    tabs_s = _ret_tables(ts)
    nc_p, nc_s = seq // CMP_STRIDE, past // CMP_STRIDE
    nsp_p = _round_up(seq // SEL_BLOCK, LANES)
    nsp_s = _round_up(past // SEL_BLOCK + 1, LANES)
    mmat_p = _importance_matrix(nc_p, nsp_p)
    mmat_s, oht_s = _importance_matrix(nc_s, nsp_s), _block_onehot_t(past)

    tiles_per_batch = seq // tm_p
    idx_p = lambda i: i // tiles_per_batch
    tab_p = lambda i: i % tiles_per_batch
    idx_s = lambda i: 0

    outs = {k: [] for k in ('kv_p', 'kv_s', 'win_p', 'win_s', 'ret_p', 'ret_s', 'conv_p', 'conv_s')}
    for l in range(depth):
        mod = _mod_call(c_all, w_ada[l], b_ada[l])
        mods_p = [a.reshape(nb_p, 1, D_MODEL) for a in jnp.split(mod[:nb_p], 6, axis=1)]
        mods_s = [jnp.repeat(a, ts, axis=0).reshape(1, m_s, D_MODEL) for a in jnp.split(mod[nb_p:n_c], 6, axis=1)]
        w_in_re = _relayout_w_in(w_in[l])
        wk, pos_k = _relayout_cmp(cmp_w1_k[l], cmp_pos_k[l])
        wv, pos_v = _relayout_cmp(cmp_w1_v[l], cmp_pos_v[l])
        pos_kv = jnp.stack([pos_k, pos_v])
        w2bd = _block_diag4(cmp_w2_k[l], cmp_w2_v[l])
        g1 = g_norm1[l].reshape(1, -1)
        g2 = g_norm2[l].reshape(1, -1)
        g_nsa = g_nsa_out[l].reshape(1, -1)
        g_ret = g_ret_out[l].reshape(1, -1)
        w_out_bf = w_out[l].astype(BF16)
        w_up_a = w_up[l][:, :D_FF].astype(BF16)
        w_up_b = w_up[l][:, D_FF:].astype(BF16)
        w_down_bf = w_down[l].astype(BF16)

        sh1, sc1, gt1, sh2, sc2, gt2 = mods_p
        qh_p, kvn_t, kvw_t, kaug_p, vaug_p, kwin_p, vwaug_p, ret_p, gat_p, kvc_p = _inproj_call(
            xp, sc1, sh1, idx_p, g1, w_in_re, cos_p, sin_p, tab_p, tm_p, HD ** -0.5 * LOG2_E, tiles_per_batch)
        sh1s, sc1s, gt1s, sh2s, sc2s, gt2s = mods_s
        qh_s, kvn_s, kvw_s, kvh_s, ret_s, gat_s = _inproj_call(
            xs, sc1s, sh1s, idx_s, g1, w_in_re, cos_s, sin_s, idx_s, m_s, HD ** -0.5)

        ab_p = _cmp_prompt_call(kvc_p, wk, wv, pos_kv).reshape(nb_p, nc_p, 512)
        cache_t = jnp.transpose(cache_nsa_kv[:, l], (0, 2, 3, 4, 1)).reshape(cache_nsa_kv.shape[0], 512, PAGE)
        win_t = jnp.transpose(cache_win_kv[:, l], (0, 2, 3, 4, 1)).reshape(nb_s, 256, wb)
        ab_s, ks_s = _cmp_sample_call(cache_t, page_table, wk, wv, pos_kv)
        cmp_p = _cmp_fin_call(ab_p, w2bd)
        cmp_s = _cmp_fin_call(ab_s, w2bd)

        o_nsa_p = _nsa_prompt_call(qh_p, gat_p, cmp_p, kaug_p, vaug_p, kwin_p, vwaug_p, mmat_p, nb_p, seq, 0)
        o_nsa_s = _nsa_sample_call(qh_s, gat_s, cmp_s, ks_s, kvh_s, win_t, mmat_s, oht_s, nb_s, ts, past, 0)

        s0_p = jnp.zeros((nb_p, N_RET, DK_RET, DV_RET), F32)
        o_ret_p, s_new_p = _ret_call(ret_p, s0_p, tabs_p, g_ret, nb_p, seq, min(4 * RET_CHUNK, seq))
        o_ret_s, s_new_s = _ret_call(ret_s, state_ret[:, l], tabs_s, g_ret, nb_s, ts, ts)

        x1_p, h2_p = _outproj_call(o_nsa_p, o_ret_p, xp, gt1, sc2, sh2, idx_p, g_nsa, g2, w_out_bf, tm_p, BF16)
        x1_s, h2_s = _outproj_call(o_nsa_s, o_ret_s, xs, gt1s, sc2s, sh2s, idx_s, g_nsa, g2, w_out_bf, m_s, F32)

        conv0_p = jnp.zeros((nb_p, 2, 2 * D_FF), F32)
        act_p, csa_p, csb_p = _ffn_up_call(h2_p, w_up_a, w_up_b, conv_w[l], conv_b[l].reshape(1, -1), conv0_p,
                                           nb_p, seq, tm_p, BF16)
        act_s, csa_s, csb_s = _ffn_up_decode_call(h2_s, w_up_a, w_up_b, conv_w[l], conv_b[l].reshape(1, -1),
                                                  state_conv[:, l], nb_s, ts)
        last = l == depth - 1
        gf = g_final.reshape(1, -1)
        xp = _ffn_down_call(act_p, x1_p, gt2, idx_p, w_down_bf, gf, tm_p, last)
        xs = _ffn_down_call(act_s, x1_s, gt2s, idx_s, w_down_bf, gf, m_s, last)

        outs['kv_p'].append(jnp.transpose(kvn_t.reshape(nb_p, 4, N_KV, HD, seq), (0, 4, 1, 2, 3)))
        outs['kv_s'].append(kvn_s.reshape(nb_s, ts, 4, N_KV, HD))
        keep_p = min(WINDOW, seq)
        win_t_p = kvw_t[:, :, seq - keep_p:].reshape(nb_p, 2, N_KV, HD, keep_p)
        outs['win_p'].append(jnp.transpose(win_t_p, (0, 4, 1, 2, 3)))
        win_all = jnp.concatenate([cache_win_kv[:, l], kvw_s.reshape(nb_s, ts, 2, N_KV, HD)], axis=1)
        outs['win_s'].append(win_all[:, win_all.shape[1] - min(WINDOW, past + ts):])
        outs['ret_p'].append(s_new_p)
        outs['ret_s'].append(s_new_s)
        outs['conv_p'].append(jnp.concatenate([csa_p, csb_p], axis=-1))
        outs['conv_s'].append(jnp.concatenate([csa_s, csb_s], axis=-1))

    st = lambda k: jnp.stack(outs[k], axis=1)
    return (xp.reshape(nb_p, seq, D_MODEL), xs.reshape(nb_s, ts, D_MODEL),
            st('kv_p'), st('kv_s'), st('win_p'), st('win_s'), st('ret_p'), st('ret_s'), st('conv_p'), st('conv_s'))
```

```python
import functools

import numpy as np
import jax
import jax.numpy as jnp
from jax import lax
from jax.experimental import pallas as pl
from jax.experimental.pallas import tpu as pltpu

F32 = jnp.float32
BF16 = jnp.bfloat16

D_MODEL = 1024
PAGE = 128
HD = 64
N_KV = 2
REP = 4
D_NSA = 512
CMP_STRIDE = 16
SEL_BLOCK = 64
N_SEL = 16
WINDOW = 512
Q_TILE = 512
KV_TILE = 1024
FORCE_BONUS = 100.0
N_RET = 4
DK_RET = 64
DV_RET = 128
D_RET = 512
RET_CHUNK = 128
D_FF = 2816
ROPE_THETA = 10000.0
EPS = 1e-6
NEG_INF = -1e30
LOG2_E = 1.4426950408889634
LANES = 128
W_IN_COLS = 3072
VMEM_LIMIT = 56 * 1024 * 1024


def _cparams(*sem):
    return pltpu.CompilerParams(dimension_semantics=sem, vmem_limit_bytes=VMEM_LIMIT)


def _dot(a, b):
    return jnp.dot(a, b, preferred_element_type=F32)


def _dot_nt(a, b):
    return lax.dot_general(a, b, (((1,), (1,)), ((), ())), preferred_element_type=F32)


def _dot_tn(a, b):
    return lax.dot_general(a, b, (((0,), (0,)), ((), ())), preferred_element_type=F32)


def _dot_split3(p, m_bf16):
    hi = p.astype(BF16)
    r1 = p - hi.astype(F32)
    mid = r1.astype(BF16)
    lo = (r1 - mid.astype(F32)).astype(BF16)
    return _dot(hi, m_bf16) + _dot(mid, m_bf16) + _dot(lo, m_bf16)


def _rms(x, g):
    return x * lax.rsqrt(jnp.mean(x * x, axis=-1, keepdims=True) + EPS) * g


def _masked_softmax(s, mask):
    s = jnp.where(mask, s, NEG_INF)
    m = jnp.max(s, axis=-1, keepdims=True)
    e = jnp.where(mask, jnp.exp(s - m), 0.0)
    return e / jnp.maximum(jnp.sum(e, axis=-1, keepdims=True), 1e-30)


def _topk_mask_t(score_t, n_sel):
    nb = score_t.shape[0]
    blk = lax.broadcasted_iota(jnp.int32, score_t.shape, 0)

    def body(_, work):
        m = jnp.max(work, axis=0, keepdims=True)
        idx = jnp.min(jnp.where(work == m, blk, nb), axis=0, keepdims=True)
        return jnp.where(blk == idx, -jnp.inf, work)

    work = lax.fori_loop(0, n_sel, body, score_t)
    return jnp.where(work == -jnp.inf, jnp.where(score_t == -jnp.inf, 0.0, 1.0), 0.0)


def _selection(psum, mmat, qpos, ns, n_sel):
    imp = _dot_split3(psum, mmat)
    blk = lax.broadcasted_iota(jnp.int32, (1, imp.shape[1]), 1)
    qblk = qpos >> 6
    valid = (blk * SEL_BLOCK <= qpos) & (blk < ns)
    forced = (blk == 0) | (blk == qblk) | (blk == qblk - 1)
    score = jnp.where(valid, imp + jnp.where(forced, FORCE_BONUS, 0.0), NEG_INF)
    score = jnp.where(blk < ns, score, -jnp.inf)
    sel = _topk_mask_t(score.T, n_sel).T
    return jnp.where(valid, sel, 0.0)


def _mod_kernel(c_ref, w_ref, b_ref, o_ref):
    c = c_ref[...]
    o_ref[...] = _dot(c * jax.nn.sigmoid(c), w_ref[...]) + b_ref[...]


def _mod_call(c_all, w_ada, b_ada):
    n = c_all.shape[0]
    tn = 1536
    return pl.pallas_call(
        _mod_kernel,
        grid=(w_ada.shape[1] // tn,),
        in_specs=[pl.BlockSpec((n, D_MODEL), lambda j: (0, 0)),
                  pl.BlockSpec((D_MODEL, tn), lambda j: (0, j)),
                  pl.BlockSpec((1, tn), lambda j: (0, j))],
        out_specs=pl.BlockSpec((n, tn), lambda j: (0, j)),
        out_shape=jax.ShapeDtypeStruct((n, w_ada.shape[1]), F32),
        compiler_params=_cparams("arbitrary"),
        name="adaln_mod",
    )(c_all, w_ada, b_ada.reshape(1, -1))


def _inproj_kernel(x_ref, sc_ref, sh_ref, g1_ref, w_ref, cos_ref, sin_ref, qh_ref, kvn_ref, kvw_ref, *rest,
                   q_scale, tiles_per_batch):
    prompt = tiles_per_batch is not None
    if prompt:
        kaug_ref, vaug_ref, kwin_ref, vwaug_ref, ret_ref, gat_ref, kvc_ref = rest
        tm = x_ref.shape[0]
        pos = ((pl.program_id(0) % tiles_per_batch) * tm + lax.broadcasted_iota(jnp.int32, (tm, 1), 0))
        blk_onehot = jnp.where(lax.broadcasted_iota(jnp.int32, (1, LANES), 1) == (pos >> (SEL_BLOCK.bit_length() - 1)),
                               1.0, 0.0).astype(BF16)
        ones_cols = jnp.where(lax.broadcasted_iota(jnp.int32, (tm, HD), 1) == 0, 1.0, 0.0).astype(BF16)
    else:
        kvh_ref, ret_ref, gat_ref = rest
    h = (_rms(x_ref[...], g1_ref[...]) * (1.0 + sc_ref[...]) + sh_ref[...]).astype(BF16)
    cos = cos_ref[...]
    sin = sin_ref[...]
    lane = lax.broadcasted_iota(jnp.int32, (1, LANES), 1)
    first_half = (lane & (HD - 1)) < HD // 2

    def seg(c0, n):
        return _dot(h, w_ref[:, c0:c0 + n])

    def rope(a):
        sw = jnp.where(first_half, pltpu.roll(a, LANES - HD // 2, 1), pltpu.roll(a, HD // 2, 1))
        return a * cos + sw * sin

    def slab_pair(c0):
        z = seg(c0, 2 * LANES)
        return z[:, :LANES], z[:, LANES:]

    for j, q in enumerate(slab_pair(0) + slab_pair(2 * LANES)):
        q = rope(q) * q_scale
        qh_ref[2 * j] = q[:, :HD].astype(qh_ref.dtype)
        qh_ref[2 * j + 1] = q[:, HD:].astype(qh_ref.dtype)
    kv_slabs = slab_pair(512) + slab_pair(768) + slab_pair(1024)
    for s, a in enumerate(kv_slabs):
        if s % 2 == 0:
            a = rope(a)
        out_ref, s_out = (kvn_ref, s) if s < 4 else (kvw_ref, s - 4)
        if not prompt:
            out_ref[:, LANES * s_out:LANES * (s_out + 1)] = a
            if s >= 2:
                kvh_ref[2 * (s - 2)] = a[:, :HD].astype(kvh_ref.dtype)
                kvh_ref[2 * (s - 2) + 1] = a[:, HD:].astype(kvh_ref.dtype)
            continue
        out_ref[LANES * s_out:LANES * (s_out + 1), :] = a.T
        if s < 2:
            kvc_ref[s] = a
            continue
        for g in range(N_KV):
            part = a[:, HD * g:HD * (g + 1)].astype(BF16)
            if s == 2:
                kaug_ref[g] = jnp.concatenate([blk_onehot, part], axis=1)
            elif s == 3:
                vaug_ref[g] = jnp.concatenate([part, ones_cols], axis=1)
            elif s == 4:
                kwin_ref[g] = part
            else:
                vwaug_ref[g] = jnp.concatenate([part, ones_cols], axis=1)
    for j, (qr, kr) in enumerate(zip(slab_pair(1280), slab_pair(1536))):
        ret_ref[:, LANES * j:LANES * (j + 1)] = rope(qr) * (DK_RET ** -0.5)
        ret_ref[:, 256 + LANES * j:256 + LANES * (j + 1)] = rope(kr)
    ret_ref[:, 512:1024] = seg(1792, 512)
    ret_ref[:, 1024:1536] = seg(2304, 512)
    gat_ref[...] = seg(2816, 256)


def _inproj_call(x2d, sc3, sh3, mod_idx, g1, w_in_re, cos, sin, tab_idx, tm, q_scale, tiles_per_batch=None):
    m = x2d.shape[0]
    mrows = sc3.shape[1]
    rows = lambda width: pl.BlockSpec((tm, width), lambda i: (i, 0))
    slabs = lambda n, width: pl.BlockSpec((n, tm, width), lambda i: (0, i, 0))
    tail_specs = [rows(1536), rows(256)]
    tail_shapes = [jax.ShapeDtypeStruct((m, 1536), F32), jax.ShapeDtypeStruct((m, 256), F32)]
    if tiles_per_batch is not None:
        nb, seq = m // (tiles_per_batch * tm), tiles_per_batch * tm
        t_idx = lambda i: (i // tiles_per_batch, 0, i % tiles_per_batch)
        out_specs = ([slabs(8, HD), pl.BlockSpec((None, 512, tm), t_idx), pl.BlockSpec((None, 256, tm), t_idx),
                      slabs(N_KV, LANES + HD), slabs(N_KV, LANES), slabs(N_KV, HD), slabs(N_KV, LANES)]
                     + tail_specs + [slabs(2, LANES)])
        out_shape = ([jax.ShapeDtypeStruct((8, m, HD), BF16),
                      jax.ShapeDtypeStruct((nb, 512, seq), F32), jax.ShapeDtypeStruct((nb, 256, seq), F32),
                      jax.ShapeDtypeStruct((N_KV, m, LANES + HD), BF16), jax.ShapeDtypeStruct((N_KV, m, LANES), BF16),
                      jax.ShapeDtypeStruct((N_KV, m, HD), BF16), jax.ShapeDtypeStruct((N_KV, m, LANES), BF16)]
                     + tail_shapes + [jax.ShapeDtypeStruct((2, m, LANES), F32)])
    else:
        out_specs = [slabs(8, HD), rows(512), rows(256), slabs(8, HD)] + tail_specs
        out_shape = [jax.ShapeDtypeStruct((8, m, HD), F32), jax.ShapeDtypeStruct((m, 512), F32),
                     jax.ShapeDtypeStruct((m, 256), F32), jax.ShapeDtypeStruct((8, m, HD), F32)] + tail_shapes
    return pl.pallas_call(
        functools.partial(_inproj_kernel, q_scale=q_scale, tiles_per_batch=tiles_per_batch),
        grid=(m // tm,),
        in_specs=[pl.BlockSpec((tm, D_MODEL), lambda i: (i, 0)),
                  pl.BlockSpec((None, mrows, D_MODEL), lambda i: (mod_idx(i), 0, 0)),
                  pl.BlockSpec((None, mrows, D_MODEL), lambda i: (mod_idx(i), 0, 0)),
                  pl.BlockSpec((1, D_MODEL), lambda i: (0, 0)),
                  pl.BlockSpec((D_MODEL, W_IN_COLS), lambda i: (0, 0)),
                  pl.BlockSpec((tm, LANES), lambda i: (tab_idx(i), 0)),
                  pl.BlockSpec((tm, LANES), lambda i: (tab_idx(i), 0))],
        out_specs=out_specs,
        out_shape=out_shape,
        compiler_params=_cparams("parallel"),
        name="in_proj",
    )(x2d, sc3, sh3, g1, w_in_re, cos, sin)


def _cmp_partial(slab, wk_ref, wv_ref, pos_ref, ab_ref):
    for kind, w_ref in ((0, wk_ref), (1, wv_ref)):
        xcat = jnp.concatenate([slab(t, kind) for t in range(CMP_STRIDE)], axis=1).astype(BF16)
        r = _dot(xcat, w_ref[...])
        bias = _dot(pos_ref[kind], w_ref[...])
        ab_ref[:, LANES * kind:LANES * (kind + 1)] = r[:, :LANES] + bias[0:1, :LANES]
        ab_ref[:, 256 + LANES * kind:256 + LANES * (kind + 1)] = r[:, LANES:] + bias[1:2, LANES:]


def _cmp_prompt_kernel(x_ref, wk_ref, wv_ref, pos_ref, ab_ref):
    rows = ab_ref.shape[0]
    _cmp_partial(lambda t, kind: x_ref[kind, pl.ds(t, rows, stride=CMP_STRIDE), :],
                 wk_ref, wv_ref, pos_ref, ab_ref)


def _cmp_prompt_call(kvc, wk, wv, pos):
    rows = kvc.shape[1] // CMP_STRIDE
    tr = min(128, rows)
    return pl.pallas_call(
        _cmp_prompt_kernel,
        grid=(rows // tr,),
        in_specs=[pl.BlockSpec((2, tr * CMP_STRIDE, LANES), lambda i: (0, i, 0)),
                  pl.BlockSpec(wk.shape, lambda i: (0, 0)),
                  pl.BlockSpec(wv.shape, lambda i: (0, 0)),
                  pl.BlockSpec(pos.shape, lambda i: (0, 0, 0))],
        out_specs=pl.BlockSpec((tr, 512), lambda i: (i, 0)),
        out_shape=jax.ShapeDtypeStruct((rows, 512), F32),
        compiler_params=_cparams("parallel"),
        name="cmp_partial_prompt",
    )(kvc, wk, wv, pos)


PAGES_PER_STEP = 16


def _cmp_sample_kernel(pt_ref, *refs):
    pages = refs[:PAGES_PER_STEP]
    perm_ref, wk_ref, wv_ref, pos_ref, ab_ref, ks_ref, x_sc = refs[PAGES_PER_STEP:]
    chunks = PAGE // CMP_STRIDE
    for p, page in enumerate(pages):
        x_sc[p] = _dot_nt(perm_ref[...], page[0:256, :].astype(BF16))
        ks_ref[:, PAGE * p:PAGE * (p + 1)] = page[256:512, :].astype(BF16)

    def slab(t, kind):
        return jnp.concatenate([x_sc[p, chunks * t:chunks * (t + 1), LANES * kind:LANES * (kind + 1)]
                                for p in range(PAGES_PER_STEP)], axis=0)

    _cmp_partial(slab, wk_ref, wv_ref, pos_ref, ab_ref)


def _cmp_sample_call(cache_t, page_table, wk, wv, pos):
    nb, n_pages = page_table.shape
    chunks = PAGE // CMP_STRIDE
    steps = n_pages // PAGES_PER_STEP

    def page_spec(k):
        return pl.BlockSpec((None, 512, PAGE), lambda b, j, pt: (pt[b, j * PAGES_PER_STEP + k], 0, 0))

    grid_spec = pltpu.PrefetchScalarGridSpec(
        num_scalar_prefetch=1,
        grid=(nb, steps),
        in_specs=[page_spec(k) for k in range(PAGES_PER_STEP)] + [
            pl.BlockSpec((PAGE, PAGE), lambda b, j, pt: (0, 0)),
            pl.BlockSpec(wk.shape, lambda b, j, pt: (0, 0)),
            pl.BlockSpec(wv.shape, lambda b, j, pt: (0, 0)),
            pl.BlockSpec(pos.shape, lambda b, j, pt: (0, 0, 0))],
        out_specs=[pl.BlockSpec((None, PAGES_PER_STEP * chunks, 512), lambda b, j, pt: (b, j, 0)),
                   pl.BlockSpec((None, 256, PAGES_PER_STEP * PAGE), lambda b, j, pt: (b, 0, j))],
        scratch_shapes=[pltpu.VMEM((PAGES_PER_STEP, PAGE, 256), F32)],
    )
    tok = np.arange(PAGE)
    perm = np.zeros((PAGE, PAGE), np.float32)
    perm[(tok % CMP_STRIDE) * chunks + tok // CMP_STRIDE, tok] = 1.0
    return pl.pallas_call(
        _cmp_sample_kernel,
        grid_spec=grid_spec,
        out_shape=[jax.ShapeDtypeStruct((nb, n_pages * chunks, 512), F32),
                   jax.ShapeDtypeStruct((nb, 256, n_pages * PAGE), BF16)],
        compiler_params=_cparams("parallel", "arbitrary"),
        name="cmp_partial_sample",
    )(page_table, *([cache_t] * PAGES_PER_STEP), jnp.asarray(perm, dtype=BF16), wk, wv, pos)


def _cmp_fin_kernel(ab_ref, w2_ref, o_ref):
    nc = ab_ref.shape[0]
    a = ab_ref[:, :256]
    b_next = pltpu.roll(ab_ref[:, 256:], nc - 1, 0)
    hid = jax.nn.gelu(a + b_next)
    out = _dot(hid.astype(BF16), w2_ref[...])
    row = lax.broadcasted_iota(jnp.int32, (nc, 1), 0)
    out = jnp.where(row < nc - 1, out, 0.0)
    for s in range(4):
        o_ref[s] = out[:, HD * s:HD * (s + 1)]


def _cmp_fin_call(ab, w2bd):
    nb, nc, _ = ab.shape
    return pl.pallas_call(
        _cmp_fin_kernel,
        grid=(nb,),
        in_specs=[pl.BlockSpec((None, nc, 512), lambda b: (b, 0, 0)),
                  pl.BlockSpec((256, 256), lambda b: (0, 0))],
        out_specs=pl.BlockSpec((None, 4, nc, HD), lambda b: (b, 0, 0, 0)),
        out_shape=jax.ShapeDtypeStruct((nb, 4, nc, HD), F32),
        compiler_params=_cparams("parallel"),
        name="cmp_finish",
    )(ab, w2bd)


def _nsa_prompt_kernel(q_ref, gat_ref, kc_ref, vc_ref, ka_ref, va_ref, kw_ref, vw_ref, mmat_ref,
                       o_ref, qa_sc, s_sc, mcur_sc, m_sc, acc_sc, *, ns, n_sel):
    i = pl.program_id(2)
    start = i * Q_TILE
    qpos = start + lax.broadcasted_iota(jnp.int32, (Q_TILE, 1), 0)
    nc = kc_ref.shape[0]
    head_rows =[slice(r * Q_TILE, (r + 1) * Q_TILE) for r in range(REP)]

    kc = kc_ref[...].astype(BF16)
    vc = vc_ref[...].astype(BF16)
    c_end = lax.broadcasted_iota(jnp.int32, (1, nc), 1) * CMP_STRIDE + (2 * CMP_STRIDE - 1)
    cbias = jnp.where(c_end <= qpos, 0.0, NEG_INF)
    any_valid = qpos >= 2 * CMP_STRIDE - 1
    psum = jnp.zeros((Q_TILE, nc), F32)
    o_cmp = []
    for r in range(REP):
        s = _dot_nt(q_ref[r], kc) + cbias
        e = jnp.exp2(s - jnp.max(s, axis=-1, keepdims=True))
        norm = jnp.where(any_valid, 1.0 / jnp.maximum(jnp.sum(e, axis=-1, keepdims=True), 1e-30), 0.0)
        p = e * norm
        psum = psum + p
        o_cmp.append(_dot(p.astype(BF16), vc))

    sel = _selection(psum, mmat_ref[...], qpos, ns, n_sel)
    selneg = jnp.where(sel > 0.5, 0.0, NEG_INF).astype(BF16)
    for r in range(REP):
        qa_sc[head_rows[r], :] = jnp.concatenate([selneg, q_ref[r]], axis=1)

    base = pl.multiple_of(jnp.maximum(start - WINDOW, 0), Q_TILE)
    kw = kw_ref[pl.ds(base, WINDOW + Q_TILE), :]
    vw = vw_ref[pl.ds(base, WINDOW + Q_TILE), :]
    dpos = qpos - (base + lax.broadcasted_iota(jnp.int32, (1, WINDOW + Q_TILE), 1))
    wbias = jnp.where((dpos >= 0) & (dpos < WINDOW), 0.0, NEG_INF)
    o_win = []
    for r in range(REP):
        s = _dot_nt(q_ref[r], kw) + wbias
        e = jnp.exp2((s - jnp.max(s, axis=-1, keepdims=True)).astype(BF16))
        ow = _dot(e, vw)
        o_win.append(ow[:, :HD] / jnp.maximum(ow[:, HD:HD + 1], 1e-30))

    n_full = start // KV_TILE

    def scores(j):
        k0 = pl.multiple_of(j * KV_TILE, KV_TILE)
        return _dot_nt(qa_sc[...], ka_ref[pl.ds(k0, KV_TILE), :])

    def consume(slot, t):
        v0 = pl.multiple_of(t * KV_TILE, KV_TILE)
        v_t = va_ref[pl.ds(v0, KV_TILE), :]
        for r in range(REP):
            rows = head_rows[r]
            m_old = m_sc[rows]
            m_new = jnp.maximum(m_old, mcur_sc[slot, rows])
            p = jnp.exp2((s_sc[slot, rows] - m_new).astype(BF16))
            acc_sc[rows] = jnp.exp2(m_old - m_new) * acc_sc[rows] + _dot(p, v_t)
            m_sc[rows] = m_new

    kpos = n_full * KV_TILE + lax.broadcasted_iota(jnp.int32, (1, KV_TILE), 1)
    causal_bias = jnp.where(kpos <= qpos, 0.0, NEG_INF)
    s_diag = scores(n_full)
    for r in range(REP):
        s_r = s_diag[head_rows[r]] + causal_bias
        s_sc[0, head_rows[r]] = s_r
        mcur_sc[0, head_rows[r]] = jnp.max(s_r, axis=-1, keepdims=True)
    m_sc[...] = jnp.full(m_sc.shape, NEG_INF, F32)
    acc_sc[...] = jnp.zeros(acc_sc.shape, F32)

    def body(j, carry):
        consume(j & 1, jnp.where(j == 0, n_full, j - 1))
        s_new = scores(j)
        slot_new = (j + 1) & 1
        s_sc[slot_new] = s_new
        mcur_sc[slot_new] = jnp.max(s_new, axis=-1, keepdims=True)
        return carry

    lax.fori_loop(0, n_full, body, 0)
    consume(n_full & 1, jnp.maximum(n_full - 1, 0))

    gate = jax.nn.sigmoid(gat_ref[...])
    for r in range(REP):
        acc = acc_sc[head_rows[r]]
        o_slc = acc[:, :HD] / jnp.maximum(acc[:, HD:HD + 1], 1e-30)
        o_ref[:, HD * r:HD * (r + 1)] = (gate[:, r:r + 1] * o_cmp[r] + gate[:, REP + r:REP + r + 1] * o_slc
                                        + gate[:, 2 * REP + r:2 * REP + r + 1] * o_win[r])


def _nsa_prompt_call(qh, gat, cmp, kaug, vaug, kwin, vwaug, mmat, nb, seq, cmp_off):
    nq = seq // Q_TILE
    nc = cmp.shape[2]
    ns = seq // SEL_BLOCK
    assert ns <= LANES and mmat.shape[1] == LANES
    kern = functools.partial(_nsa_prompt_kernel, ns=ns, n_sel=min(N_SEL, ns))

    def kv_spec(width):
        return pl.BlockSpec((None, seq, width), lambda b, g, i: (g, b, 0))

    return pl.pallas_call(
        kern,
        grid=(nb, N_KV, nq),
        in_specs=[pl.BlockSpec((REP, Q_TILE, HD), lambda b, g, i: (g, b * nq + i, 0)),
                  pl.BlockSpec((Q_TILE, LANES), lambda b, g, i: (b * nq + i, g)),
                  pl.BlockSpec((None, None, nc, HD), lambda b, g, i: (cmp_off + b, g, 0, 0)),
                  pl.BlockSpec((None, None, nc, HD), lambda b, g, i: (cmp_off + b, 2 + g, 0, 0)),
                  kv_spec(LANES + HD), kv_spec(LANES), kv_spec(HD), kv_spec(LANES),
                  pl.BlockSpec(mmat.shape, lambda b, g, i: (0, 0))],
        out_specs=pl.BlockSpec((Q_TILE, REP * HD), lambda b, g, i: (b * nq + i, g)),
        out_shape=jax.ShapeDtypeStruct((nb * seq, D_NSA), F32),
        scratch_shapes=[pltpu.VMEM((REP * Q_TILE, LANES + HD), BF16),
                        pltpu.VMEM((2, REP * Q_TILE, KV_TILE), F32),
                        pltpu.VMEM((2, REP * Q_TILE, 1), F32),
                        pltpu.VMEM((REP * Q_TILE, 1), F32),
                        pltpu.VMEM((REP * Q_TILE, LANES), F32)],
        compiler_params=_cparams("parallel", "parallel", "arbitrary"),
        name="nsa_prompt",
    )(qh, gat, cmp, cmp, kaug, vaug, kwin, vwaug, mmat)


SEQS_PER_STEP = 2


def _nsa_sample_kernel(q_ref, gat_ref, cmp_ref, ks_ref, new_ref, win_ref, mmat_ref, oht_ref, o_ref,
                       *, past, ts, ns, n_sel, nseq):
    nc = cmp_ref.shape[2]
    wb = win_ref.shape[2]
    t_idx = lax.broadcasted_iota(jnp.int32, (ts, 1), 0)
    qpos = jnp.concatenate([past + t_idx] * REP, axis=0)
    gate = jax.nn.sigmoid(gat_ref[...])
    pad_rows = jnp.zeros((LANES - ts, HD), F32)
    units = [(b, g) for b in range(nseq) for g in range(N_KV)]

    def padded(slab, b):
        return jnp.concatenate([new_ref[slab, ts * b:ts * (b + 1), :], pad_rows], axis=0).astype(BF16)

    c_end = lax.broadcasted_iota(jnp.int32, (1, nc), 1) * CMP_STRIDE + (2 * CMP_STRIDE - 1)
    qs, o_cmps, psums = [], [], []
    for b, g in units:
        q = jnp.concatenate([q_ref[REP * g + r, ts * b:ts * (b + 1), :] for r in range(REP)], axis=0).astype(BF16)
        p = _masked_softmax(_dot_nt(q, cmp_ref[b, g].astype(BF16)), c_end <= qpos)
        o_cmps.append(_dot(p.astype(BF16), cmp_ref[b, 2 + g].astype(BF16)))
        psum = p[0:ts]
        for r in range(1, REP):
            psum = psum + p[r * ts:(r + 1) * ts]
        qs.append(q)
        psums.append(psum)
    psum_all = jnp.concatenate(psums + [jnp.zeros((LANES - len(units) * ts, nc), F32)], axis=0)
    row = lax.broadcasted_iota(jnp.int32, (LANES, 1), 0)
    sel_all = _selection(psum_all, mmat_ref[...], past + lax.rem(row, ts), ns, n_sel)

    kidx = lax.broadcasted_iota(jnp.int32, (1, LANES), 1)
    kpos_w = past - wb + lax.broadcasted_iota(jnp.int32, (1, wb + LANES), 1)
    dpos_w = qpos - kpos_w
    wmask = (dpos_w >= 0) & (dpos_w < WINDOW) & (kpos_w >= 0)
    for u, (b, g) in enumerate(units):
        q = qs[u]
        sel = jnp.concatenate([sel_all[u * ts:(u + 1) * ts]] * REP, axis=0)
        selneg = jnp.where(sel[:, :LANES] > 0.5, 0.0, NEG_INF).astype(BF16)
        s_old = _dot(q, ks_ref[b, HD * g:HD * (g + 1), :]) + _dot(selneg, oht_ref[...])
        new_bias = jnp.where((sel[:, ns - 1:ns] > 0.5) & (past + kidx <= qpos), 0.0, NEG_INF)
        s_new = _dot_nt(q, padded(g, b)) + new_bias
        m = jnp.maximum(jnp.max(s_old, axis=-1, keepdims=True), jnp.max(s_new, axis=-1, keepdims=True))
        e_old = jnp.exp(s_old - m)
        e_new = jnp.exp(s_new - m)
        denom = jnp.sum(e_old, axis=-1, keepdims=True) + jnp.sum(e_new, axis=-1, keepdims=True)
        o_slc = (_dot_nt(e_old.astype(BF16), ks_ref[b, 128 + HD * g:128 + HD * (g + 1), :])
                 + _dot(e_new.astype(BF16), padded(2 + g, b))) / jnp.maximum(denom, 1e-30)

        s_w = jnp.concatenate([_dot(q, win_ref[b, HD * g:HD * (g + 1), :].astype(BF16)),
                               _dot_nt(q, padded(4 + g, b))], axis=1)
        pw = _masked_softmax(s_w, wmask)
        o_win = (_dot_nt(pw[:, :wb].astype(BF16), win_ref[b, 128 + HD * g:128 + HD * (g + 1), :].astype(BF16))
                 + _dot(pw[:, wb:].astype(BF16), padded(6 + g, b)))

        g_b = gate[ts * b:ts * (b + 1)]
        for r in range(REP):
            rs = slice(r * ts, (r + 1) * ts)
            c = LANES * g + r
            o_ref[ts * b:ts * (b + 1), HD * (REP * g + r):HD * (REP * g + r + 1)] = (
                g_b[:, c:c + 1] * o_cmps[u][rs] + g_b[:, c + REP:c + REP + 1] * o_slc[rs]
                + g_b[:, c + 2 * REP:c + 2 * REP + 1] * o_win[rs])


def _nsa_sample_call(qh, gat, cmp, ks_t, kvh_new, win_t, mmat, oht, nb, ts, past, cmp_off):
    nc = cmp.shape[2]
    ns = past // SEL_BLOCK + 1
    nseq = SEQS_PER_STEP if nb % SEQS_PER_STEP == 0 else 1
    assert ns - 1 <= LANES and nseq * N_KV * ts <= LANES and cmp_off % nseq == 0
    kern = functools.partial(_nsa_sample_kernel, past=past, ts=ts, ns=ns, n_sel=min(N_SEL, ns), nseq=nseq)
    rows = nseq * ts
    return pl.pallas_call(
        kern,
        grid=(nb // nseq,),
        in_specs=[pl.BlockSpec((8, rows, HD), lambda b: (0, b, 0)),
                  pl.BlockSpec((rows, 256), lambda b: (b, 0)),
                  pl.BlockSpec((nseq, 4, nc, HD), lambda b: (cmp_off // nseq + b, 0, 0, 0)),
                  pl.BlockSpec((nseq, 256, past), lambda b: (b, 0, 0)),
                  pl.BlockSpec((8, rows, HD), lambda b: (0, b, 0)),
                  pl.BlockSpec((nseq, 256, win_t.shape[2]), lambda b: (b, 0, 0)),
                  pl.BlockSpec(mmat.shape, lambda b: (0, 0)),
                  pl.BlockSpec(oht.shape, lambda b: (0, 0))],
        out_specs=pl.BlockSpec((rows, D_NSA), lambda b: (b, 0)),
        out_shape=jax.ShapeDtypeStruct((nb * ts, D_NSA), F32),
        compiler_params=_cparams("parallel"),
        name="nsa_sample",
    )(qh, gat, cmp, ks_t, kvh_new, win_t, mmat, oht)


def _ret_kernel(x_ref, s0_ref, dmat_ref, xi_ref, zeta_ref, gc_ref, gro_ref, o_ref, s_out_ref, s_sc, pad_sc,
                *, rows):
    @pl.when(pl.program_id(1) == 0)
    def _():
        s_sc[...] = s0_ref[...]

    if rows < RET_CHUNK:
        pad_sc[...] = jnp.zeros(pad_sc.shape, F32)
        pad_sc[0:rows, :] = x_ref[...]
        x = pad_sc
    else:
        x = x_ref
    for c in range(max(rows // RET_CHUNK, 1)):
        r0 = RET_CHUNK * c
        for h in range(N_RET):
            q = x[r0:r0 + RET_CHUNK, DK_RET * h:DK_RET * (h + 1)].astype(BF16)
            k = x[r0:r0 + RET_CHUNK, 256 + DK_RET * h:256 + DK_RET * (h + 1)]
            v = x[r0:r0 + RET_CHUNK, 512 + DV_RET * h:512 + DV_RET * (h + 1)].astype(BF16)
            gr = x[r0:r0 + RET_CHUNK, 1024 + DV_RET * h:1024 + DV_RET * (h + 1)]
            att = _dot_nt(q, k.astype(BF16)) * dmat_ref[h]
            s_old = s_sc[h]
            o = _dot(att.astype(BF16), v) + _dot(q, s_old.astype(BF16)) * xi_ref[h]
            s_sc[h] = gc_ref[h] * s_old + _dot_tn((k * zeta_ref[h]).astype(BF16), v)
            mu = jnp.mean(o, axis=-1, keepdims=True)
            var = jnp.mean(jnp.square(o - mu), axis=-1, keepdims=True)
            y = (o - mu) * lax.rsqrt(var + EPS) * gro_ref[:, DV_RET * h:DV_RET * (h + 1)]
            res = gr * jax.nn.sigmoid(gr) * y
            n_out = min(rows, RET_CHUNK)
            o_ref[r0:r0 + n_out, DV_RET * h:DV_RET * (h + 1)] = res[0:n_out]
    s_out_ref[...] = s_sc[...]


def _ret_call(ret, s0, tabs, g_ret_out, nb, seq, rows):
    nchunks = seq // rows
    dmat, xi, zeta, gc = tabs
    kern = functools.partial(_ret_kernel, rows=rows)
    full3 = lambda b, c: (0, 0, 0)
    return pl.pallas_call(
        kern,
        grid=(nb, nchunks),
        in_specs=[pl.BlockSpec((rows, 1536), lambda b, c: (b * nchunks + c, 0)),
                  pl.BlockSpec((None, N_RET, DK_RET, DV_RET), lambda b, c: (b, 0, 0, 0)),
                  pl.BlockSpec(dmat.shape, full3), pl.BlockSpec(xi.shape, full3),
                  pl.BlockSpec(zeta.shape, full3), pl.BlockSpec(gc.shape, full3),
                  pl.BlockSpec((1, D_RET), lambda b, c: (0, 0))],
        out_specs=[pl.BlockSpec((rows, D_RET), lambda b, c: (b * nchunks + c, 0)),
                   pl.BlockSpec((None, N_RET, DK_RET, DV_RET), lambda b, c: (b, 0, 0, 0))],
        out_shape=[jax.ShapeDtypeStruct((nb * seq, D_RET), F32),
                   jax.ShapeDtypeStruct((nb, N_RET, DK_RET, DV_RET), F32)],
        scratch_shapes=[pltpu.VMEM((N_RET, DK_RET, DV_RET), F32),
                        pltpu.VMEM((RET_CHUNK, 1536), F32)],
        compiler_params=_cparams("parallel", "arbitrary"),
        name="retention",
    )(ret, s0, dmat, xi, zeta, gc, g_ret_out)


def _ret_tables(chunk):
    c = RET_CHUNK
    log_g = jnp.log1p(-jnp.power(2.0, -5.0 - jnp.arange(N_RET, dtype=F32)))
    i = jnp.arange(c, dtype=F32)
    diff = i[:, None] - i[None, :]
    dmat = jnp.where(diff >= 0, jnp.exp(jnp.maximum(diff, 0.0)[None] * log_g[:, None, None]), 0.0)
    xi = jnp.exp((i[None, :] + 1.0) * log_g[:, None])
    zeta = jnp.where(i[None, :] < chunk, jnp.exp((chunk - 1.0 - i)[None, :] * log_g[:, None]), 0.0)
    g_c = jnp.exp(chunk * log_g)
    return (dmat,
            jnp.broadcast_to(xi[:, :, None], (N_RET, c, DV_RET)),
            jnp.broadcast_to(zeta[:, :, None], (N_RET, c, DK_RET)),
            jnp.broadcast_to(g_c[:, None, None], (N_RET, DK_RET, DV_RET)))


def _outproj_kernel(on_ref, or_ref, x_ref, gt_ref, sc_ref, sh_ref, gn_ref, g2_ref, w_ref, x1_ref, h2_ref):
    a = _rms(on_ref[...], gn_ref[...])
    mix = _dot(a.astype(BF16), w_ref[0:D_NSA, :]) + _dot(or_ref[...].astype(BF16), w_ref[D_NSA:, :])
    x1 = x_ref[...] + gt_ref[...] * mix
    x1_ref[...] = x1
    h2_ref[...] = (_rms(x1, g2_ref[...]) * (1.0 + sc_ref[...]) + sh_ref[...]).astype(h2_ref.dtype)


def _outproj_call(o_nsa, o_ret, x2d, gt3, sc3, sh3, mod_idx, g_nsa, g2, w_out, tm, h_dtype):
    m = x2d.shape[0]
    mrows = gt3.shape[1]
    mod_spec = pl.BlockSpec((None, mrows, D_MODEL), lambda i: (mod_idx(i), 0, 0))
    return pl.pallas_call(
        _outproj_kernel,
        grid=(m // tm,),
        in_specs=[pl.BlockSpec((tm, D_NSA), lambda i: (i, 0)),
                  pl.BlockSpec((tm, D_RET), lambda i: (i, 0)),
                  pl.BlockSpec((tm, D_MODEL), lambda i: (i, 0)),
                  mod_spec, mod_spec, mod_spec,
                  pl.BlockSpec((1, D_NSA), lambda i: (0, 0)),
                  pl.BlockSpec((1, D_MODEL), lambda i: (0, 0)),
                  pl.BlockSpec((D_NSA + D_RET, D_MODEL), lambda i: (0, 0))],
        out_specs=[pl.BlockSpec((tm, D_MODEL), lambda i: (i, 0)),
                   pl.BlockSpec((tm, D_MODEL), lambda i: (i, 0))],
        out_shape=[jax.ShapeDtypeStruct((m, D_MODEL), F32),
                   jax.ShapeDtypeStruct((m, D_MODEL), h_dtype)],
        compiler_params=_cparams("parallel"),
        name="out_proj",
    )(o_nsa, o_ret, x2d, gt3, sc3, sh3, g_nsa, g2, w_out)


FF_TILE = D_FF // 2


def _ffn_up_kernel(h_ref, wa_ref, wb_ref, cwa_ref, cwb_ref, cba_ref, cbb_ref, sta_ref, stb_ref,
                   act_ref, csa_ref, csb_ref, prev_a, prev_b):
    @pl.when(pl.program_id(2) == 0)
    def _():
        prev_a[...] = sta_ref[...]
        prev_b[...] = stb_ref[...]

    h = h_ref[...].astype(BF16)
    tm = h.shape[0]
    row = lax.broadcasted_iota(jnp.int32, (tm, 1), 0)

    def half(w_ref, cw_ref, cb_ref, prev, cs_ref):
        u = _dot(h, w_ref[...])
        p = prev[...]
        back1 = pltpu.roll(u, 1, 0)
        back2 = pltpu.roll(u, 2, 0)
        u1 = jnp.where(row >= 1, back1, p[1:2])
        u2 = jnp.where(row >= 2, back2, jnp.where(row == 1, p[1:2], p[0:1]))
        y = cb_ref[...] + cw_ref[0:1] * u2 + cw_ref[1:2] * u1 + cw_ref[2:3] * u
        tail = back2[0:2]
        prev[...] = tail
        cs_ref[...] = tail
        return y

    a = half(wa_ref, cwa_ref, cba_ref, prev_a, csa_ref)
    b = half(wb_ref, cwb_ref, cbb_ref, prev_b, csb_ref)
    act_ref[...] = (a * jax.nn.sigmoid(a) * b).astype(act_ref.dtype)


def _ffn_up_call(h2, w_up_a, w_up_b, conv_w, conv_b, conv_state, nb, seq, tm, act_dtype):
    nrt = seq // tm
    nt = D_FF // FF_TILE
    return pl.pallas_call(
        _ffn_up_kernel,
        grid=(nt, nb, nrt),
        in_specs=[pl.BlockSpec((tm, D_MODEL), lambda j, b, i: (b * nrt + i, 0)),
                  pl.BlockSpec((D_MODEL, FF_TILE), lambda j, b, i: (0, j)),
                  pl.BlockSpec((D_MODEL, FF_TILE), lambda j, b, i: (0, j)),
                  pl.BlockSpec((3, FF_TILE), lambda j, b, i: (0, j)),
                  pl.BlockSpec((3, FF_TILE), lambda j, b, i: (0, nt + j)),
                  pl.BlockSpec((1, FF_TILE), lambda j, b, i: (0, j)),
                  pl.BlockSpec((1, FF_TILE), lambda j, b, i: (0, nt + j)),
                  pl.BlockSpec((None, 2, FF_TILE), lambda j, b, i: (b, 0, j)),
                  pl.BlockSpec((None, 2, FF_TILE), lambda j, b, i: (b, 0, nt + j))],
        out_specs=[pl.BlockSpec((tm, FF_TILE), lambda j, b, i: (b * nrt + i, j)),
                   pl.BlockSpec((None, 2, FF_TILE), lambda j, b, i: (b, 0, j)),
                   pl.BlockSpec((None, 2, FF_TILE), lambda j, b, i: (b, 0, j))],
        out_shape=[jax.ShapeDtypeStruct((nb * seq, D_FF), act_dtype),
                   jax.ShapeDtypeStruct((nb, 2, D_FF), F32),
                   jax.ShapeDtypeStruct((nb, 2, D_FF), F32)],
        scratch_shapes=[pltpu.VMEM((2, FF_TILE), F32), pltpu.VMEM((2, FF_TILE), F32)],
        compiler_params=_cparams("parallel", "parallel", "arbitrary"),
        name="ffn_up_conv",
    )(h2, w_up_a, w_up_b, conv_w, conv_w, conv_b, conv_b, conv_state, conv_state)


def _ffn_up_decode_kernel(h_ref, wa_ref, wb_ref, cwa_ref, cwb_ref, cba_ref, cbb_ref, sta_ref, stb_ref,
                          act_ref, csa_ref, csb_ref, *, nb, ts):
    h = h_ref[...].astype(BF16)
    t = lax.broadcasted_iota(jnp.int32, (1, ts, 1), 1)

    def half(w_ref, cw_ref, cb_ref, st_ref, cs_ref):
        u = _dot(h, w_ref[...]).reshape(nb, ts, FF_TILE)
        p = st_ref[...]
        back1 = pltpu.roll(u, 1, 1)
        back2 = pltpu.roll(u, 2, 1)
        u1 = jnp.where(t >= 1, back1, p[:, 1:2])
        u2 = jnp.where(t >= 2, back2, jnp.where(t == 1, p[:, 1:2], p[:, 0:1]))
        y = cb_ref[...] + cw_ref[0:1] * u2 + cw_ref[1:2] * u1 + cw_ref[2:3] * u
        cs_ref[...] = back2[:, 0:2]
        return y

    a = half(wa_ref, cwa_ref, cba_ref, sta_ref, csa_ref)
    b = half(wb_ref, cwb_ref, cbb_ref, stb_ref, csb_ref)
    act_ref[...] = (a * jax.nn.sigmoid(a) * b).reshape(nb * ts, FF_TILE)


def _ffn_up_decode_call(h2, w_up_a, w_up_b, conv_w, conv_b, conv_state, nb, ts):
    nt = D_FF // FF_TILE
    m = nb * ts
    return pl.pallas_call(
        functools.partial(_ffn_up_decode_kernel, nb=nb, ts=ts),
        grid=(nt,),
        in_specs=[pl.BlockSpec((m, D_MODEL), lambda j: (0, 0)),
                  pl.BlockSpec((D_MODEL, FF_TILE), lambda j: (0, j)),
                  pl.BlockSpec((D_MODEL, FF_TILE), lambda j: (0, j)),
                  pl.BlockSpec((3, FF_TILE), lambda j: (0, j)),
                  pl.BlockSpec((3, FF_TILE), lambda j: (0, nt + j)),
                  pl.BlockSpec((1, FF_TILE), lambda j: (0, j)),
                  pl.BlockSpec((1, FF_TILE), lambda j: (0, nt + j)),
                  pl.BlockSpec((nb, 2, FF_TILE), lambda j: (0, 0, j)),
                  pl.BlockSpec((nb, 2, FF_TILE), lambda j: (0, 0, nt + j))],
        out_specs=[pl.BlockSpec((m, FF_TILE), lambda j: (0, j)),
                   pl.BlockSpec((nb, 2, FF_TILE), lambda j: (0, 0, j)),
                   pl.BlockSpec((nb, 2, FF_TILE), lambda j: (0, 0, j))],
        out_shape=[jax.ShapeDtypeStruct((m, D_FF), F32),
                   jax.ShapeDtypeStruct((nb, 2, D_FF), F32),
                   jax.ShapeDtypeStruct((nb, 2, D_FF), F32)],
        compiler_params=_cparams("parallel"),
        name="ffn_up_conv_decode",
    )(h2, w_up_a, w_up_b, conv_w, conv_w, conv_b, conv_b, conv_state, conv_state)


def _ffn_down_kernel(a_ref, x1_ref, gt_ref, w_ref, gf_ref, y_ref, *, final_norm):
    x2 = x1_ref[...] + gt_ref[...] * _dot(a_ref[...].astype(BF16), w_ref[...])
    y_ref[...] = _rms(x2, gf_ref[...]) if final_norm else x2


def _ffn_down_call(act, x1, gt3, mod_idx, w_down, g_final, tm, final_norm):
    m = x1.shape[0]
    mrows = gt3.shape[1]
    return pl.pallas_call(
        functools.partial(_ffn_down_kernel, final_norm=final_norm),
        grid=(m // tm,),
        in_specs=[pl.BlockSpec((tm, D_FF), lambda i: (i, 0)),
                  pl.BlockSpec((tm, D_MODEL), lambda i: (i, 0)),
                  pl.BlockSpec((None, mrows, D_MODEL), lambda i: (mod_idx(i), 0, 0)),
                  pl.BlockSpec((D_FF, D_MODEL), lambda i: (0, 0)),
                  pl.BlockSpec((1, D_MODEL), lambda i: (0, 0))],
        out_specs=pl.BlockSpec((tm, D_MODEL), lambda i: (i, 0)),
        out_shape=jax.ShapeDtypeStruct((m, D_MODEL), F32),
        compiler_params=_cparams("parallel"),
        name="ffn_down",
    )(act, x1, gt3, w_down, g_final)


def _relayout_w_in(w_in):
    q_n, kv_n, gate_n, q_r, k_r, v_r, g_r = jnp.split(w_in, [512, 1280, 1304, 1560, 1816, 2328], axis=1)
    gate = gate_n.reshape(D_MODEL, 3, N_KV, REP)
    gate = jnp.transpose(gate, (0, 2, 1, 3)).reshape(D_MODEL, N_KV, 3 * REP)
    gate = jnp.pad(gate, ((0, 0), (0, 0), (0, LANES - 3 * REP))).reshape(D_MODEL, N_KV * LANES)
    return jnp.concatenate([q_n, kv_n, q_r, k_r, v_r, g_r, gate], axis=1).astype(BF16)


def _relayout_cmp(w1, pos):
    w1r = w1.reshape(2, CMP_STRIDE, HD, HD)
    w = jnp.einsum('atdn,gh->tgdahn', w1r, jnp.eye(N_KV, dtype=w1.dtype)).reshape(CMP_STRIDE * N_KV * HD, 2 * N_KV * HD)
    posr = pos.reshape(2, CMP_STRIDE, 1, HD)
    prow = jnp.broadcast_to(posr, (2, CMP_STRIDE, N_KV, HD)).reshape(2, CMP_STRIDE * N_KV * HD)
    prow = jnp.pad(prow, ((0, 14), (0, 0)))
    return w.astype(BF16), prow.astype(BF16)


def _block_diag4(w2k, w2v):
    z = jnp.zeros((HD, HD), w2k.dtype)
    rows = [[w2k, z, z, z], [z, w2k, z, z], [z, z, w2v, z], [z, z, z, w2v]]
    return jnp.block(rows).astype(BF16)


def _rope_tables(pos):
    half = HD // 2
    inv = ROPE_THETA ** (-jnp.arange(half, dtype=F32) / half)
    ang = pos.astype(F32)[:, None] * inv[None, :]
    cos, sin = jnp.cos(ang), jnp.sin(ang)
    return jnp.tile(cos, (1, 4)), jnp.tile(jnp.concatenate([-sin, sin], axis=1), (1, 2))


def _importance_matrix(nc, nsp):
    n = np.arange(nc)[:, None]
    d = n - 4 * np.arange(nsp)[None, :]
    m = ((d >= 0) & (d <= 3)).astype(np.float32) + ((d >= -1) & (d <= 2)).astype(np.float32)
    return jnp.asarray(m, dtype=BF16)


def _block_onehot_t(past):
    blk = np.arange(LANES)[:, None]
    return jnp.asarray((blk == np.arange(past)[None, :] // SEL_BLOCK).astype(np.float32), dtype=BF16)


def _round_up(x, m):
    return (x + m - 1) // m * m


def kernel(x_prompt, x_sample, c_prompt, c_sample, cache_nsa_kv, cache_win_kv, state_ret, state_conv, page_table,
           w_ada, b_ada, g_norm1, w_in, cmp_pos_k, cmp_w1_k, cmp_w2_k, cmp_pos_v, cmp_w1_v, cmp_w2_v,
           g_nsa_out, g_ret_out, w_out, g_norm2, w_up, conv_w, conv_b, w_down, g_final):
    nb_p, seq, _ = x_prompt.shape
    nb_s, ts, _ = x_sample.shape
    depth = w_ada.shape[0]
    n_pages = page_table.shape[1]
    past = n_pages * PAGE
    wb = cache_win_kv.shape[2]
    m_p, m_s = nb_p * seq, nb_s * ts
    tm_p = 512

    xp = x_prompt.reshape(m_p, D_MODEL)
    xs = x_sample.reshape(m_s, D_MODEL)
    n_c = nb_p + nb_s
    c_all = jnp.pad(jnp.concatenate([c_prompt, c_sample], axis=0), ((0, _round_up(n_c, 8) - n_c), (0, 0)))

    cos_p, sin_p = _rope_tables(jnp.arange(seq, dtype=jnp.int32))
    cos_s, sin_s = _rope_tables(jnp.tile(past + jnp.arange(ts, dtype=jnp.int32), nb_s))
    tabs_p = _ret_tables(min(RET_CHUNK, seq))
    tabs_s = _ret_tables(ts)
    nc_p, nc_s = seq // CMP_STRIDE, past // CMP_STRIDE
    nsp_p = _round_up(seq // SEL_BLOCK, LANES)
    nsp_s = _round_up(past // SEL_BLOCK + 1, LANES)
    mmat_p = _importance_matrix(nc_p, nsp_p)
    mmat_s, oht_s = _importance_matrix(nc_s, nsp_s), _block_onehot_t(past)

    tiles_per_batch = seq // tm_p
    idx_p = lambda i: i // tiles_per_batch
    tab_p = lambda i: i % tiles_per_batch
    idx_s = lambda i: 0

    outs = {k: [] for k in ('kv_p', 'kv_s', 'win_p', 'win_s', 'ret_p', 'ret_s', 'conv_p', 'conv_s')}
    for l in range(depth):
        mod = _mod_call(c_all, w_ada[l], b_ada[l])
        mods_p = [a.reshape(nb_p, 1, D_MODEL) for a in jnp.split(mod[:nb_p], 6, axis=1)]
        mods_s = [jnp.repeat(a, ts, axis=0).reshape(1, m_s, D_MODEL) for a in jnp.split(mod[nb_p:n_c], 6, axis=1)]
        w_in_re = _relayout_w_in(w_in[l])
        wk, pos_k = _relayout_cmp(cmp_w1_k[l], cmp_pos_k[l])
        wv, pos_v = _relayout_cmp(cmp_w1_v[l], cmp_pos_v[l])
        pos_kv = jnp.stack([pos_k, pos_v])
        w2bd = _block_diag4(cmp_w2_k[l], cmp_w2_v[l])
        g1 = g_norm1[l].reshape(1, -1)
        g2 = g_norm2[l].reshape(1, -1)
        g_nsa = g_nsa_out[l].reshape(1, -1)
        g_ret = g_ret_out[l].reshape(1, -1)
        w_out_bf = w_out[l].astype(BF16)
        w_up_a = w_up[l][:, :D_FF].astype(BF16)
        w_up_b = w_up[l][:, D_FF:].astype(BF16)
        w_down_bf = w_down[l].astype(BF16)

        sh1, sc1, gt1, sh2, sc2, gt2 = mods_p
        qh_p, kvn_t, kvw_t, kaug_p, vaug_p, kwin_p, vwaug_p, ret_p, gat_p, kvc_p = _inproj_call(
            xp, sc1, sh1, idx_p, g1, w_in_re, cos_p, sin_p, tab_p, tm_p, HD ** -0.5 * LOG2_E, tiles_per_batch)
        sh1s, sc1s, gt1s, sh2s, sc2s, gt2s = mods_s
        qh_s, kvn_s, kvw_s, kvh_s, ret_s, gat_s = _inproj_call(
            xs, sc1s, sh1s, idx_s, g1, w_in_re, cos_s, sin_s, idx_s, m_s, HD ** -0.5)

        ab_p = _cmp_prompt_call(kvc_p, wk, wv, pos_kv).reshape(nb_p, nc_p, 512)
        cache_t = jnp.transpose(cache_nsa_kv[:, l], (0, 2, 3, 4, 1)).reshape(cache_nsa_kv.shape[0], 512, PAGE)
        win_t = jnp.transpose(cache_win_kv[:, l], (0, 2, 3, 4, 1)).reshape(nb_s, 256, wb)
        ab_s, ks_s = _cmp_sample_call(cache_t, page_table, wk, wv, pos_kv)
        cmp_p = _cmp_fin_call(ab_p, w2bd)
        cmp_s = _cmp_fin_call(ab_s, w2bd)

        o_nsa_p = _nsa_prompt_call(qh_p, gat_p, cmp_p, kaug_p, vaug_p, kwin_p, vwaug_p, mmat_p, nb_p, seq, 0)
        o_nsa_s = _nsa_sample_call(qh_s, gat_s, cmp_s, ks_s, kvh_s, win_t, mmat_s, oht_s, nb_s, ts, past, 0)

        s0_p = jnp.zeros((nb_p, N_RET, DK_RET, DV_RET), F32)
        o_ret_p, s_new_p = _ret_call(ret_p, s0_p, tabs_p, g_ret, nb_p, seq, min(4 * RET_CHUNK, seq))
        o_ret_s, s_new_s = _ret_call(ret_s, state_ret[:, l], tabs_s, g_ret, nb_s, ts, ts)

        x1_p, h2_p = _outproj_call(o_nsa_p, o_ret_p, xp, gt1, sc2, sh2, idx_p, g_nsa, g2, w_out_bf, tm_p, BF16)
        x1_s, h2_s = _outproj_call(o_nsa_s, o_ret_s, xs, gt1s, sc2s, sh2s, idx_s, g_nsa, g2, w_out_bf, m_s, F32)

        conv0_p = jnp.zeros((nb_p, 2, 2 * D_FF), F32)
        act_p, csa_p, csb_p = _ffn_up_call(h2_p, w_up_a, w_up_b, conv_w[l], conv_b[l].reshape(1, -1), conv0_p,
                                           nb_p, seq, tm_p, BF16)
        act_s, csa_s, csb_s = _ffn_up_decode_call(h2_s, w_up_a, w_up_b, conv_w[l], conv_b[l].reshape(1, -1),
                                                  state_conv[:, l], nb_s, ts)
        last = l == depth - 1
        gf = g_final.reshape(1, -1)
        xp = _ffn_down_call(act_p, x1_p, gt2, idx_p, w_down_bf, gf, tm_p, last)
        xs = _ffn_down_call(act_s, x1_s, gt2s, idx_s, w_down_bf, gf, m_s, last)

        outs['kv_p'].append(jnp.transpose(kvn_t.reshape(nb_p, 4, N_KV, HD, seq), (0, 4, 1, 2, 3)))
        outs['kv_s'].append(kvn_s.reshape(nb_s, ts, 4, N_KV, HD))
        keep_p = min(WINDOW, seq)
        win_t_p = kvw_t[:, :, seq - keep_p:].reshape(nb_p, 2, N_KV, HD, keep_p)
        outs['win_p'].append(jnp.transpose(win_t_p, (0, 4, 1, 2, 3)))
        win_all = jnp.concatenate([cache_win_kv[:, l], kvw_s.reshape(nb_s, ts, 2, N_KV, HD)], axis=1)
        outs['win_s'].append(win_all[:, win_all.shape[1] - min(WINDOW, past + ts):])
        outs['ret_p'].append(s_new_p)
        outs['ret_s'].append(s_new_s)
        outs['conv_p'].append(jnp.concatenate([csa_p, csb_p], axis=-1))
        outs['conv_s'].append(jnp.concatenate([csa_s, csb_s], axis=-1))

    st = lambda k: jnp.stack(outs[k], axis=1)
    return (xp.reshape(nb_p, seq, D_MODEL), xs.reshape(nb_s, ts, D_MODEL),
            st('kv_p'), st('kv_s'), st('win_p'), st('win_s'), st('ret_p'), st('ret_s'), st('conv_p'), st('conv_s'))
```

```python
import functools

import numpy as np
import jax
import jax.numpy as jnp
from jax import lax
from jax.experimental import pallas as pl
from jax.experimental.pallas import tpu as pltpu

F32 = jnp.float32
BF16 = jnp.bfloat16

D_MODEL = 1024
PAGE = 128
HD = 64
N_KV = 2
REP = 4
D_NSA = 512
CMP_STRIDE = 16
SEL_BLOCK = 64
N_SEL = 16
WINDOW = 512
Q_TILE = 512
KV_TILE = 1024
FORCE_BONUS = 100.0
N_RET = 4
DK_RET = 64
DV_RET = 128
D_RET = 512
RET_CHUNK = 128
D_FF = 2816
ROPE_THETA = 10000.0
EPS = 1e-6
NEG_INF = -1e30
LOG2_E = 1.4426950408889634
LANES = 128
W_IN_COLS = 3072
VMEM_LIMIT = 56 * 1024 * 1024


def _cparams(*sem):
    return pltpu.CompilerParams(dimension_semantics=sem, vmem_limit_bytes=VMEM_LIMIT)


def _dot(a, b):
    return jnp.dot(a, b, preferred_element_type=F32)


def _dot_nt(a, b):
    return lax.dot_general(a, b, (((1,), (1,)), ((), ())), preferred_element_type=F32)


def _dot_tn(a, b):
    return lax.dot_general(a, b, (((0,), (0,)), ((), ())), preferred_element_type=F32)


def _dot_split3(p, m_bf16):
    hi = p.astype(BF16)
    r1 = p - hi.astype(F32)
    mid = r1.astype(BF16)
    lo = (r1 - mid.astype(F32)).astype(BF16)
    return _dot(hi, m_bf16) + _dot(mid, m_bf16) + _dot(lo, m_bf16)


def _rms(x, g):
    return x * lax.rsqrt(jnp.mean(x * x, axis=-1, keepdims=True) + EPS) * g


def _masked_softmax(s, mask):
    s = jnp.where(mask, s, NEG_INF)
    m = jnp.max(s, axis=-1, keepdims=True)
    e = jnp.where(mask, jnp.exp(s - m), 0.0)
    return e / jnp.maximum(jnp.sum(e, axis=-1, keepdims=True), 1e-30)


def _topk_mask_t(score_t, n_sel):
    nb = score_t.shape[0]
    blk = lax.broadcasted_iota(jnp.int32, score_t.shape, 0)

    def body(_, work):
        m = jnp.max(work, axis=0, keepdims=True)
        idx = jnp.min(jnp.where(work == m, blk, nb), axis=0, keepdims=True)
        return jnp.where(blk == idx, -jnp.inf, work)

    work = lax.fori_loop(0, n_sel, body, score_t)
    return jnp.where(work == -jnp.inf, jnp.where(score_t == -jnp.inf, 0.0, 1.0), 0.0)


def _selection(psum, mmat, qpos, ns, n_sel):
    imp = _dot_split3(psum, mmat)
    blk = lax.broadcasted_iota(jnp.int32, (1, imp.shape[1]), 1)
    qblk = qpos >> 6
    valid = (blk * SEL_BLOCK <= qpos) & (blk < ns)
    forced = (blk == 0) | (blk == qblk) | (blk == qblk - 1)
    score = jnp.where(valid, imp + jnp.where(forced, FORCE_BONUS, 0.0), NEG_INF)
    score = jnp.where(blk < ns, score, -jnp.inf)
    sel = _topk_mask_t(score.T, n_sel).T
    return jnp.where(valid, sel, 0.0)


def _mod_kernel(c_ref, w_ref, b_ref, o_ref):
    c = c_ref[...]
    o_ref[...] = _dot(c * jax.nn.sigmoid(c), w_ref[...]) + b_ref[...]


def _mod_call(c_all, w_ada, b_ada):
    n = c_all.shape[0]
    tn = 1536
    return pl.pallas_call(
        _mod_kernel,
        grid=(w_ada.shape[1] // tn,),
        in_specs=[pl.BlockSpec((n, D_MODEL), lambda j: (0, 0)),
                  pl.BlockSpec((D_MODEL, tn), lambda j: (0, j)),
                  pl.BlockSpec((1, tn), lambda j: (0, j))],
        out_specs=pl.BlockSpec((n, tn), lambda j: (0, j)),
        out_shape=jax.ShapeDtypeStruct((n, w_ada.shape[1]), F32),
        compiler_params=_cparams("arbitrary"),
        name="adaln_mod",
    )(c_all, w_ada, b_ada.reshape(1, -1))


def _inproj_kernel(x_ref, sc_ref, sh_ref, g1_ref, w_ref, cos_ref, sin_ref, qh_ref, kvn_ref, kvw_ref, *rest,
                   q_scale, tiles_per_batch):
    prompt = tiles_per_batch is not None
    if prompt:
        kaug_ref, vaug_ref, kwin_ref, vwaug_ref, ret_ref, gat_ref, kvc_ref = rest
        tm = x_ref.shape[0]
        pos = ((pl.program_id(0) % tiles_per_batch) * tm + lax.broadcasted_iota(jnp.int32, (tm, 1), 0))
        blk_onehot = jnp.where(lax.broadcasted_iota(jnp.int32, (1, LANES), 1) == (pos >> (SEL_BLOCK.bit_length() - 1)),
                               1.0, 0.0).astype(BF16)
        ones_cols = jnp.where(lax.broadcasted_iota(jnp.int32, (tm, HD), 1) == 0, 1.0, 0.0).astype(BF16)
    else:
        kvh_ref, ret_ref, gat_ref = rest
    h = (_rms(x_ref[...], g1_ref[...]) * (1.0 + sc_ref[...]) + sh_ref[...]).astype(BF16)
    cos = cos_ref[...]
    sin = sin_ref[...]
    lane = lax.broadcasted_iota(jnp.int32, (1, LANES), 1)
    first_half = (lane & (HD - 1)) < HD // 2

    def seg(c0, n):
        return _dot(h, w_ref[:, c0:c0 + n])

    def rope(a):
        sw = jnp.where(first_half, pltpu.roll(a, LANES - HD // 2, 1), pltpu.roll(a, HD // 2, 1))
        return a * cos + sw * sin

    def slab_pair(c0):
        z = seg(c0, 2 * LANES)
        return z[:, :LANES], z[:, LANES:]

    for j, q in enumerate(slab_pair(0) + slab_pair(2 * LANES)):
        q = rope(q) * q_scale
        qh_ref[2 * j] = q[:, :HD].astype(qh_ref.dtype)
        qh_ref[2 * j + 1] = q[:, HD:].astype(qh_ref.dtype)
    kv_slabs = slab_pair(512) + slab_pair(768) + slab_pair(1024)
    for s, a in enumerate(kv_slabs):
        if s % 2 == 0:
            a = rope(a)
        out_ref, s_out = (kvn_ref, s) if s < 4 else (kvw_ref, s - 4)
        if not prompt:
            out_ref[:, LANES * s_out:LANES * (s_out + 1)] = a
            if s >= 2:
                kvh_ref[2 * (s - 2)] = a[:, :HD].astype(kvh_ref.dtype)
                kvh_ref[2 * (s - 2) + 1] = a[:, HD:].astype(kvh_ref.dtype)
            continue
        out_ref[LANES * s_out:LANES * (s_out + 1), :] = a.T
        if s < 2:
            kvc_ref[s] = a
            continue
        for g in range(N_KV):
            part = a[:, HD * g:HD * (g + 1)].astype(BF16)
            if s == 2:
                kaug_ref[g] = jnp.concatenate([blk_onehot, part], axis=1)
            elif s == 3:
                vaug_ref[g] = jnp.concatenate([part, ones_cols], axis=1)
            elif s == 4:
                kwin_ref[g] = part
            else:
                vwaug_ref[g] = jnp.concatenate([part, ones_cols], axis=1)
    for j, (qr, kr) in enumerate(zip(slab_pair(1280), slab_pair(1536))):
        ret_ref[:, LANES * j:LANES * (j + 1)] = rope(qr) * (DK_RET ** -0.5)
        ret_ref[:, 256 + LANES * j:256 + LANES * (j + 1)] = rope(kr)
    ret_ref[:, 512:1024] = seg(1792, 512)
    ret_ref[:, 1024:1536] = seg(2304, 512)
    gat_ref[...] = seg(2816, 256)


def _inproj_call(x2d, sc3, sh3, mod_idx, g1, w_in_re, cos, sin, tab_idx, tm, q_scale, tiles_per_batch=None):
    m = x2d.shape[0]
    mrows = sc3.shape[1]
    rows = lambda width: pl.BlockSpec((tm, width), lambda i: (i, 0))
    slabs = lambda n, width: pl.BlockSpec((n, tm, width), lambda i: (0, i, 0))
    tail_specs = [rows(1536), rows(256)]
    tail_shapes = [jax.ShapeDtypeStruct((m, 1536), F32), jax.ShapeDtypeStruct((m, 256), F32)]
    if tiles_per_batch is not None:
        nb, seq = m // (tiles_per_batch * tm), tiles_per_batch * tm
        t_idx = lambda i: (i // tiles_per_batch, 0, i % tiles_per_batch)
        out_specs = ([slabs(8, HD), pl.BlockSpec((None, 512, tm), t_idx), pl.BlockSpec((None, 256, tm), t_idx),
                      slabs(N_KV, LANES + HD), slabs(N_KV, LANES), slabs(N_KV, HD), slabs(N_KV, LANES)]
                     + tail_specs + [slabs(2, LANES)])
        out_shape = ([jax.ShapeDtypeStruct((8, m, HD), BF16),
                      jax.ShapeDtypeStruct((nb, 512, seq), F32), jax.ShapeDtypeStruct((nb, 256, seq), F32),
                      jax.ShapeDtypeStruct((N_KV, m, LANES + HD), BF16), jax.ShapeDtypeStruct((N_KV, m, LANES), BF16),
                      jax.ShapeDtypeStruct((N_KV, m, HD), BF16), jax.ShapeDtypeStruct((N_KV, m, LANES), BF16)]
                     + tail_shapes + [jax.ShapeDtypeStruct((2, m, LANES), F32)])
    else:
        out_specs = [slabs(8, HD), rows(512), rows(256), slabs(8, HD)] + tail_specs
        out_shape = [jax.ShapeDtypeStruct((8, m, HD), F32), jax.ShapeDtypeStruct((m, 512), F32),
                     jax.ShapeDtypeStruct((m, 256), F32), jax.ShapeDtypeStruct((8, m, HD), F32)] + tail_shapes
    return pl.pallas_call(
        functools.partial(_inproj_kernel, q_scale=q_scale, tiles_per_batch=tiles_per_batch),
        grid=(m // tm,),
        in_specs=[pl.BlockSpec((tm, D_MODEL), lambda i: (i, 0)),
                  pl.BlockSpec((None, mrows, D_MODEL), lambda i: (mod_idx(i), 0, 0)),
                  pl.BlockSpec((None, mrows, D_MODEL), lambda i: (mod_idx(i), 0, 0)),
                  pl.BlockSpec((1, D_MODEL), lambda i: (0, 0)),
                  pl.BlockSpec((D_MODEL, W_IN_COLS), lambda i: (0, 0)),
                  pl.BlockSpec((tm, LANES), lambda i: (tab_idx(i), 0)),
                  pl.BlockSpec((tm, LANES), lambda i: (tab_idx(i), 0))],
        out_specs=out_specs,
        out_shape=out_shape,
        compiler_params=_cparams("parallel"),
        name="in_proj",
    )(x2d, sc3, sh3, g1, w_in_re, cos, sin)


def _cmp_partial(slab, wk_ref, wv_ref, pos_ref, ab_ref):
    for kind, w_ref in ((0, wk_ref), (1, wv_ref)):
        xcat = jnp.concatenate([slab(t, kind) for t in range(CMP_STRIDE)], axis=1).astype(BF16)
        r = _dot(xcat, w_ref[...])
        bias = _dot(pos_ref[kind], w_ref[...])
        ab_ref[:, LANES * kind:LANES * (kind + 1)] = r[:, :LANES] + bias[0:1, :LANES]
        ab_ref[:, 256 + LANES * kind:256 + LANES * (kind + 1)] = r[:, LANES:] + bias[1:2, LANES:]


def _cmp_prompt_kernel(x_ref, wk_ref, wv_ref, pos_ref, ab_ref):
    rows = ab_ref.shape[0]
    _cmp_partial(lambda t, kind: x_ref[kind, pl.ds(t, rows, stride=CMP_STRIDE), :],
                 wk_ref, wv_ref, pos_ref, ab_ref)


def _cmp_prompt_call(kvc, wk, wv, pos):
    rows = kvc.shape[1] // CMP_STRIDE
    tr = min(128, rows)
    return pl.pallas_call(
        _cmp_prompt_kernel,
        grid=(rows // tr,),
        in_specs=[pl.BlockSpec((2, tr * CMP_STRIDE, LANES), lambda i: (0, i, 0)),
                  pl.BlockSpec(wk.shape, lambda i: (0, 0)),
                  pl.BlockSpec(wv.shape, lambda i: (0, 0)),
                  pl.BlockSpec(pos.shape, lambda i: (0, 0, 0))],
        out_specs=pl.BlockSpec((tr, 512), lambda i: (i, 0)),
        out_shape=jax.ShapeDtypeStruct((rows, 512), F32),
        compiler_params=_cparams("parallel"),
        name="cmp_partial_prompt",
    )(kvc, wk, wv, pos)


PAGES_PER_STEP = 16


def _cmp_sample_kernel(pt_ref, *refs):
    pages = refs[:PAGES_PER_STEP]
    perm_ref, wk_ref, wv_ref, pos_ref, ab_ref, ks_ref, x_sc = refs[PAGES_PER_STEP:]
    chunks = PAGE // CMP_STRIDE
    for p, page in enumerate(pages):
        x_sc[p] = _dot_nt(perm_ref[...], page[0:256, :].astype(BF16))
        ks_ref[:, PAGE * p:PAGE * (p + 1)] = page[256:512, :].astype(BF16)

    def slab(t, kind):
        return jnp.concatenate([x_sc[p, chunks * t:chunks * (t + 1), LANES * kind:LANES * (kind + 1)]
                                for p in range(PAGES_PER_STEP)], axis=0)

    _cmp_partial(slab, wk_ref, wv_ref, pos_ref, ab_ref)


def _cmp_sample_call(cache_t, page_table, wk, wv, pos):
    nb, n_pages = page_table.shape
    chunks = PAGE // CMP_STRIDE
    steps = n_pages // PAGES_PER_STEP

    def page_spec(k):
        return pl.BlockSpec((None, 512, PAGE), lambda b, j, pt: (pt[b, j * PAGES_PER_STEP + k], 0, 0))

    grid_spec = pltpu.PrefetchScalarGridSpec(
        num_scalar_prefetch=1,
        grid=(nb, steps),
        in_specs=[page_spec(k) for k in range(PAGES_PER_STEP)] + [
            pl.BlockSpec((PAGE, PAGE), lambda b, j, pt: (0, 0)),
            pl.BlockSpec(wk.shape, lambda b, j, pt: (0, 0)),
            pl.BlockSpec(wv.shape, lambda b, j, pt: (0, 0)),
            pl.BlockSpec(pos.shape, lambda b, j, pt: (0, 0, 0))],
        out_specs=[pl.BlockSpec((None, PAGES_PER_STEP * chunks, 512), lambda b, j, pt: (b, j, 0)),
                   pl.BlockSpec((None, 256, PAGES_PER_STEP * PAGE), lambda b, j, pt: (b, 0, j))],
        scratch_shapes=[pltpu.VMEM((PAGES_PER_STEP, PAGE, 256), F32)],
    )
    tok = np.arange(PAGE)
    perm = np.zeros((PAGE, PAGE), np.float32)
    perm[(tok % CMP_STRIDE) * chunks + tok // CMP_STRIDE, tok] = 1.0
    return pl.pallas_call(
        _cmp_sample_kernel,
        grid_spec=grid_spec,
        out_shape=[jax.ShapeDtypeStruct((nb, n_pages * chunks, 512), F32),
                   jax.ShapeDtypeStruct((nb, 256, n_pages * PAGE), BF16)],
        compiler_params=_cparams("parallel", "arbitrary"),
        name="cmp_partial_sample",
    )(page_table, *([cache_t] * PAGES_PER_STEP), jnp.asarray(perm, dtype=BF16), wk, wv, pos)


def _cmp_fin_kernel(ab_ref, w2_ref, o_ref):
    nc = ab_ref.shape[0]
    a = ab_ref[:, :256]
    b_next = pltpu.roll(ab_ref[:, 256:], nc - 1, 0)
    hid = jax.nn.gelu(a + b_next)
    out = _dot(hid.astype(BF16), w2_ref[...])
    row = lax.broadcasted_iota(jnp.int32, (nc, 1), 0)
    out = jnp.where(row < nc - 1, out, 0.0)
    for s in range(4):
        o_ref[s] = out[:, HD * s:HD * (s + 1)]


def _cmp_fin_call(ab, w2bd):
    nb, nc, _ = ab.shape
    return pl.pallas_call(
        _cmp_fin_kernel,
        grid=(nb,),
        in_specs=[pl.BlockSpec((None, nc, 512), lambda b: (b, 0, 0)),
                  pl.BlockSpec((256, 256), lambda b: (0, 0))],
        out_specs=pl.BlockSpec((None, 4, nc, HD), lambda b: (b, 0, 0, 0)),
        out_shape=jax.ShapeDtypeStruct((nb, 4, nc, HD), F32),
        compiler_params=_cparams("parallel"),
        name="cmp_finish",
    )(ab, w2bd)


def _nsa_prompt_kernel(q_ref, gat_ref, kc_ref, vc_ref, ka_ref, va_ref, kw_ref, vw_ref, mmat_ref,
                       o_ref, qa_sc, s_sc, mcur_sc, m_sc, acc_sc, *, ns, n_sel):
    i = pl.program_id(2)
    start = i * Q_TILE
    qpos = start + lax.broadcasted_iota(jnp.int32, (Q_TILE, 1), 0)
    nc = kc_ref.shape[0]
    head_rows =[slice(r * Q_TILE, (r + 1) * Q_TILE) for r in range(REP)]

    kc = kc_ref[...].astype(BF16)
    vc = vc_ref[...].astype(BF16)
    c_end = lax.broadcasted_iota(jnp.int32, (1, nc), 1) * CMP_STRIDE + (2 * CMP_STRIDE - 1)
    cbias = jnp.where(c_end <= qpos, 0.0, NEG_INF)
    any_valid = qpos >= 2 * CMP_STRIDE - 1
    psum = jnp.zeros((Q_TILE, nc), F32)
    o_cmp = []
    for r in range(REP):
        s = _dot_nt(q_ref[r], kc) + cbias
        e = jnp.exp2(s - jnp.max(s, axis=-1, keepdims=True))
        norm = jnp.where(any_valid, 1.0 / jnp.maximum(jnp.sum(e, axis=-1, keepdims=True), 1e-30), 0.0)
        p = e * norm
        psum = psum + p
        o_cmp.append(_dot(p.astype(BF16), vc))

    sel = _selection(psum, mmat_ref[...], qpos, ns, n_sel)
    selneg = jnp.where(sel > 0.5, 0.0, NEG_INF).astype(BF16)
    for r in range(REP):
        qa_sc[head_rows[r], :] = jnp.concatenate([selneg, q_ref[r]], axis=1)

    base = pl.multiple_of(jnp.maximum(start - WINDOW, 0), Q_TILE)
    kw = kw_ref[pl.ds(base, WINDOW + Q_TILE), :]
    vw = vw_ref[pl.ds(base, WINDOW + Q_TILE), :]
    dpos = qpos - (base + lax.broadcasted_iota(jnp.int32, (1, WINDOW + Q_TILE), 1))
    wbias = jnp.where((dpos >= 0) & (dpos < WINDOW), 0.0, NEG_INF)
    o_win = []
    for r in range(REP):
        s = _dot_nt(q_ref[r], kw) + wbias
        e = jnp.exp2((s - jnp.max(s, axis=-1, keepdims=True)).astype(BF16))
        ow = _dot(e, vw)
        o_win.append(ow[:, :HD] / jnp.maximum(ow[:, HD:HD + 1], 1e-30))

    n_full = start // KV_TILE

    def scores(j):
        k0 = pl.multiple_of(j * KV_TILE, KV_TILE)
        return _dot_nt(qa_sc[...], ka_ref[pl.ds(k0, KV_TILE), :])

    def consume(slot, t):
        v0 = pl.multiple_of(t * KV_TILE, KV_TILE)
        v_t = va_ref[pl.ds(v0, KV_TILE), :]
        for r in range(REP):
            rows = head_rows[r]
            m_old = m_sc[rows]
            m_new = jnp.maximum(m_old, mcur_sc[slot, rows])
            p = jnp.exp2((s_sc[slot, rows] - m_new).astype(BF16))
            acc_sc[rows] = jnp.exp2(m_old - m_new) * acc_sc[rows] + _dot(p, v_t)
            m_sc[rows] = m_new

    kpos = n_full * KV_TILE + lax.broadcasted_iota(jnp.int32, (1, KV_TILE), 1)
    causal_bias = jnp.where(kpos <= qpos, 0.0, NEG_INF)
    s_diag = scores(n_full)
    for r in range(REP):
        s_r = s_diag[head_rows[r]] + causal_bias
        s_sc[0, head_rows[r]] = s_r
        mcur_sc[0, head_rows[r]] = jnp.max(s_r, axis=-1, keepdims=True)
    m_sc[...] = jnp.full(m_sc.shape, NEG_INF, F32)
    acc_sc[...] = jnp.zeros(acc_sc.shape, F32)

    def body(j, carry):
        consume(j & 1, jnp.where(j == 0, n_full, j - 1))
        s_new = scores(j)
        slot_new = (j + 1) & 1
        s_sc[slot_new] = s_new
        mcur_sc[slot_new] = jnp.max(s_new, axis=-1, keepdims=True)
        return carry

    lax.fori_loop(0, n_full, body, 0)
    consume(n_full & 1, jnp.maximum(n_full - 1, 0))

    gate = jax.nn.sigmoid(gat_ref[...])
    for r in range(REP):
        acc = acc_sc[head_rows[r]]
        o_slc = acc[:, :HD] / jnp.maximum(acc[:, HD:HD + 1], 1e-30)
        o_ref[:, HD * r:HD * (r + 1)] = (gate[:, r:r + 1] * o_cmp[r] + gate[:, REP + r:REP + r + 1] * o_slc
                                        + gate[:, 2 * REP + r:2 * REP + r + 1] * o_win[r])


def _nsa_prompt_call(qh, gat, cmp, kaug, vaug, kwin, vwaug, mmat, nb, seq, cmp_off):
    nq = seq // Q_TILE
    nc = cmp.shape[2]
    ns = seq // SEL_BLOCK
    assert ns <= LANES and mmat.shape[1] == LANES
    kern = functools.partial(_nsa_prompt_kernel, ns=ns, n_sel=min(N_SEL, ns))

    def kv_spec(width):
        return pl.BlockSpec((None, seq, width), lambda b, g, i: (g, b, 0))

    return pl.pallas_call(
        kern,
        grid=(nb, N_KV, nq),
        in_specs=[pl.BlockSpec((REP, Q_TILE, HD), lambda b, g, i: (g, b * nq + i, 0)),
                  pl.BlockSpec((Q_TILE, LANES), lambda b, g, i: (b * nq + i, g)),
                  pl.BlockSpec((None, None, nc, HD), lambda b, g, i: (cmp_off + b, g, 0, 0)),
                  pl.BlockSpec((None, None, nc, HD), lambda b, g, i: (cmp_off + b, 2 + g, 0, 0)),
                  kv_spec(LANES + HD), kv_spec(LANES), kv_spec(HD), kv_spec(LANES),
                  pl.BlockSpec(mmat.shape, lambda b, g, i: (0, 0))],
        out_specs=pl.BlockSpec((Q_TILE, REP * HD), lambda b, g, i: (b * nq + i, g)),
        out_shape=jax.ShapeDtypeStruct((nb * seq, D_NSA), F32),
        scratch_shapes=[pltpu.VMEM((REP * Q_TILE, LANES + HD), BF16),
                        pltpu.VMEM((2, REP * Q_TILE, KV_TILE), F32),
                        pltpu.VMEM((2, REP * Q_TILE, 1), F32),
                        pltpu.VMEM((REP * Q_TILE, 1), F32),
                        pltpu.VMEM((REP * Q_TILE, LANES), F32)],
        compiler_params=_cparams("parallel", "parallel", "arbitrary"),
        name="nsa_prompt",
    )(qh, gat, cmp, cmp, kaug, vaug, kwin, vwaug, mmat)


SEQS_PER_STEP = 2


def _nsa_sample_kernel(q_ref, gat_ref, cmp_ref, ks_ref, new_ref, win_ref, mmat_ref, oht_ref, o_ref,
                       *, past, ts, ns, n_sel, nseq):
    nc = cmp_ref.shape[2]
    wb = win_ref.shape[2]
    t_idx = lax.broadcasted_iota(jnp.int32, (ts, 1), 0)
    qpos = jnp.concatenate([past + t_idx] * REP, axis=0)
    gate = jax.nn.sigmoid(gat_ref[...])
    pad_rows = jnp.zeros((LANES - ts, HD), F32)
    units = [(b, g) for b in range(nseq) for g in range(N_KV)]

    def padded(slab, b):
        return jnp.concatenate([new_ref[slab, ts * b:ts * (b + 1), :], pad_rows], axis=0).astype(BF16)

    c_end = lax.broadcasted_iota(jnp.int32, (1, nc), 1) * CMP_STRIDE + (2 * CMP_STRIDE - 1)
    qs, o_cmps, psums = [], [], []
    for b, g in units:
        q = jnp.concatenate([q_ref[REP * g + r, ts * b:ts * (b + 1), :] for r in range(REP)], axis=0).astype(BF16)
        p = _masked_softmax(_dot_nt(q, cmp_ref[b, g].astype(BF16)), c_end <= qpos)
        o_cmps.append(_dot(p.astype(BF16), cmp_ref[b, 2 + g].astype(BF16)))
        psum = p[0:ts]
        for r in range(1, REP):
            psum = psum + p[r * ts:(r + 1) * ts]
        qs.append(q)
        psums.append(psum)
    psum_all = jnp.concatenate(psums + [jnp.zeros((LANES - len(units) * ts, nc), F32)], axis=0)
    row = lax.broadcasted_iota(jnp.int32, (LANES, 1), 0)
    sel_all = _selection(psum_all, mmat_ref[...], past + lax.rem(row, ts), ns, n_sel)

    kidx = lax.broadcasted_iota(jnp.int32, (1, LANES), 1)
    kpos_w = past - wb + lax.broadcasted_iota(jnp.int32, (1, wb + LANES), 1)
    dpos_w = qpos - kpos_w
    wmask = (dpos_w >= 0) & (dpos_w < WINDOW) & (kpos_w >= 0)
    for u, (b, g) in enumerate(units):
        q = qs[u]
        sel = jnp.concatenate([sel_all[u * ts:(u + 1) * ts]] * REP, axis=0)
        selneg = jnp.where(sel[:, :LANES] > 0.5, 0.0, NEG_INF).astype(BF16)
        s_old = _dot(q, ks_ref[b, HD * g:HD * (g + 1), :]) + _dot(selneg, oht_ref[...])
        new_bias = jnp.where((sel[:, ns - 1:ns] > 0.5) & (past + kidx <= qpos), 0.0, NEG_INF)
        s_new = _dot_nt(q, padded(g, b)) + new_bias
        m = jnp.maximum(jnp.max(s_old, axis=-1, keepdims=True), jnp.max(s_new, axis=-1, keepdims=True))
        e_old = jnp.exp(s_old - m)
        e_new = jnp.exp(s_new - m)
        denom = jnp.sum(e_old, axis=-1, keepdims=True) + jnp.sum(e_new, axis=-1, keepdims=True)
        o_slc = (_dot_nt(e_old.astype(BF16), ks_ref[b, 128 + HD * g:128 + HD * (g + 1), :])
                 + _dot(e_new.astype(BF16), padded(2 + g, b))) / jnp.maximum(denom, 1e-30)

        s_w = jnp.concatenate([_dot(q, win_ref[b, HD * g:HD * (g + 1), :].astype(BF16)),
                               _dot_nt(q, padded(4 + g, b))], axis=1)
        pw = _masked_softmax(s_w, wmask)
        o_win = (_dot_nt(pw[:, :wb].astype(BF16), win_ref[b, 128 + HD * g:128 + HD * (g + 1), :].astype(BF16))
                 + _dot(pw[:, wb:].astype(BF16), padded(6 + g, b)))

        g_b = gate[ts * b:ts * (b + 1)]
        for r in range(REP):
            rs = slice(r * ts, (r + 1) * ts)
            c = LANES * g + r
            o_ref[ts * b:ts * (b + 1), HD * (REP * g + r):HD * (REP * g + r + 1)] = (
                g_b[:, c:c + 1] * o_cmps[u][rs] + g_b[:, c + REP:c + REP + 1] * o_slc[rs]
                + g_b[:, c + 2 * REP:c + 2 * REP + 1] * o_win[rs])


def _nsa_sample_call(qh, gat, cmp, ks_t, kvh_new, win_t, mmat, oht, nb, ts, past, cmp_off):
    nc = cmp.shape[2]
    ns = past // SEL_BLOCK + 1
    nseq = SEQS_PER_STEP if nb % SEQS_PER_STEP == 0 else 1
    assert ns - 1 <= LANES and nseq * N_KV * ts <= LANES and cmp_off % nseq == 0
    kern = functools.partial(_nsa_sample_kernel, past=past, ts=ts, ns=ns, n_sel=min(N_SEL, ns), nseq=nseq)
    rows = nseq * ts
    return pl.pallas_call(
        kern,
        grid=(nb // nseq,),
        in_specs=[pl.BlockSpec((8, rows, HD), lambda b: (0, b, 0)),
                  pl.BlockSpec((rows, 256), lambda b: (b, 0)),
                  pl.BlockSpec((nseq, 4, nc, HD), lambda b: (cmp_off // nseq + b, 0, 0, 0)),
                  pl.BlockSpec((nseq, 256, past), lambda b: (b, 0, 0)),
                  pl.BlockSpec((8, rows, HD), lambda b: (0, b, 0)),
                  pl.BlockSpec((nseq, 256, win_t.shape[2]), lambda b: (b, 0, 0)),
                  pl.BlockSpec(mmat.shape, lambda b: (0, 0)),
                  pl.BlockSpec(oht.shape, lambda b: (0, 0))],
        out_specs=pl.BlockSpec((rows, D_NSA), lambda b: (b, 0)),
        out_shape=jax.ShapeDtypeStruct((nb * ts, D_NSA), F32),
        compiler_params=_cparams("parallel"),
        name="nsa_sample",
    )(qh, gat, cmp, ks_t, kvh_new, win_t, mmat, oht)


def _ret_kernel(x_ref, s0_ref, dmat_ref, xi_ref, zeta_ref, gc_ref, gro_ref, o_ref, s_out_ref, s_sc, pad_sc,
                *, rows):
    @pl.when(pl.program_id(1) == 0)
    def _():
        s_sc[...] = s0_ref[...]

    if rows < RET_CHUNK:
        pad_sc[...] = jnp.zeros(pad_sc.shape, F32)
        pad_sc[0:rows, :] = x_ref[...]
        x = pad_sc
    else:
        x = x_ref
    for c in range(max(rows // RET_CHUNK, 1)):
        r0 = RET_CHUNK * c
        for h in range(N_RET):
            q = x[r0:r0 + RET_CHUNK, DK_RET * h:DK_RET * (h + 1)].astype(BF16)
            k = x[r0:r0 + RET_CHUNK, 256 + DK_RET * h:256 + DK_RET * (h + 1)]
            v = x[r0:r0 + RET_CHUNK, 512 + DV_RET * h:512 + DV_RET * (h + 1)].astype(BF16)
            gr = x[r0:r0 + RET_CHUNK, 1024 + DV_RET * h:1024 + DV_RET * (h + 1)]
            att = _dot_nt(q, k.astype(BF16)) * dmat_ref[h]
            s_old = s_sc[h]
            o = _dot(att.astype(BF16), v) + _dot(q, s_old.astype(BF16)) * xi_ref[h]
            s_sc[h] = gc_ref[h] * s_old + _dot_tn((k * zeta_ref[h]).astype(BF16), v)
            mu = jnp.mean(o, axis=-1, keepdims=True)
            var = jnp.mean(jnp.square(o - mu), axis=-1, keepdims=True)
            y = (o - mu) * lax.rsqrt(var + EPS) * gro_ref[:, DV_RET * h:DV_RET * (h + 1)]
            res = gr * jax.nn.sigmoid(gr) * y
            n_out = min(rows, RET_CHUNK)
            o_ref[r0:r0 + n_out, DV_RET * h:DV_RET * (h + 1)] = res[0:n_out]
    s_out_ref[...] = s_sc[...]


def _ret_call(ret, s0, tabs, g_ret_out, nb, seq, rows):
    nchunks = seq // rows
    dmat, xi, zeta, gc = tabs
    kern = functools.partial(_ret_kernel, rows=rows)
    full3 = lambda b, c: (0, 0, 0)
    return pl.pallas_call(
        kern,
        grid=(nb, nchunks),
        in_specs=[pl.BlockSpec((rows, 1536), lambda b, c: (b * nchunks + c, 0)),
                  pl.BlockSpec((None, N_RET, DK_RET, DV_RET), lambda b, c: (b, 0, 0, 0)),
                  pl.BlockSpec(dmat.shape, full3), pl.BlockSpec(xi.shape, full3),
                  pl.BlockSpec(zeta.shape, full3), pl.BlockSpec(gc.shape, full3),
                  pl.BlockSpec((1, D_RET), lambda b, c: (0, 0))],
        out_specs=[pl.BlockSpec((rows, D_RET), lambda b, c: (b * nchunks + c, 0)),
                   pl.BlockSpec((None, N_RET, DK_RET, DV_RET), lambda b, c: (b, 0, 0, 0))],
        out_shape=[jax.ShapeDtypeStruct((nb * seq, D_RET), F32),
                   jax.ShapeDtypeStruct((nb, N_RET, DK_RET, DV_RET), F32)],
        scratch_shapes=[pltpu.VMEM((N_RET, DK_RET, DV_RET), F32),
                        pltpu.VMEM((RET_CHUNK, 1536), F32)],
        compiler_params=_cparams("parallel", "arbitrary"),
        name="retention",
    )(ret, s0, dmat, xi, zeta, gc, g_ret_out)


def _ret_tables(chunk):
    c = RET_CHUNK
    log_g = jnp.log1p(-jnp.power(2.0, -5.0 - jnp.arange(N_RET, dtype=F32)))
    i = jnp.arange(c, dtype=F32)
    diff = i[:, None] - i[None, :]
    dmat = jnp.where(diff >= 0, jnp.exp(jnp.maximum(diff, 0.0)[None] * log_g[:, None, None]), 0.0)
    xi = jnp.exp((i[None, :] + 1.0) * log_g[:, None])
    zeta = jnp.where(i[None, :] < chunk, jnp.exp((chunk - 1.0 - i)[None, :] * log_g[:, None]), 0.0)
    g_c = jnp.exp(chunk * log_g)
    return (dmat,
            jnp.broadcast_to(xi[:, :, None], (N_RET, c, DV_RET)),
            jnp.broadcast_to(zeta[:, :, None], (N_RET, c, DK_RET)),
            jnp.broadcast_to(g_c[:, None, None], (N_RET, DK_RET, DV_RET)))


def _outproj_kernel(on_ref, or_ref, x_ref, gt_ref, sc_ref, sh_ref, gn_ref, g2_ref, w_ref, x1_ref, h2_ref):
    a = _rms(on_ref[...], gn_ref[...])
    mix = _dot(a.astype(BF16), w_ref[0:D_NSA, :]) + _dot(or_ref[...].astype(BF16), w_ref[D_NSA:, :])
    x1 = x_ref[...] + gt_ref[...] * mix
    x1_ref[...] = x1
    h2_ref[...] = (_rms(x1, g2_ref[...]) * (1.0 + sc_ref[...]) + sh_ref[...]).astype(h2_ref.dtype)


def _outproj_call(o_nsa, o_ret, x2d, gt3, sc3, sh3, mod_idx, g_nsa, g2, w_out, tm, h_dtype):
    m = x2d.shape[0]
    mrows = gt3.shape[1]
    mod_spec = pl.BlockSpec((None, mrows, D_MODEL), lambda i: (mod_idx(i), 0, 0))
    return pl.pallas_call(
        _outproj_kernel,
        grid=(m // tm,),
        in_specs=[pl.BlockSpec((tm, D_NSA), lambda i: (i, 0)),
                  pl.BlockSpec((tm, D_RET), lambda i: (i, 0)),
                  pl.BlockSpec((tm, D_MODEL), lambda i: (i, 0)),
                  mod_spec, mod_spec, mod_spec,
                  pl.BlockSpec((1, D_NSA), lambda i: (0, 0)),
                  pl.BlockSpec((1, D_MODEL), lambda i: (0, 0)),
                  pl.BlockSpec((D_NSA + D_RET, D_MODEL), lambda i: (0, 0))],
        out_specs=[pl.BlockSpec((tm, D_MODEL), lambda i: (i, 0)),
                   pl.BlockSpec((tm, D_MODEL), lambda i: (i, 0))],
        out_shape=[jax.ShapeDtypeStruct((m, D_MODEL), F32),
                   jax.ShapeDtypeStruct((m, D_MODEL), h_dtype)],
        compiler_params=_cparams("parallel"),
        name="out_proj",
    )(o_nsa, o_ret, x2d, gt3, sc3, sh3, g_nsa, g2, w_out)


FF_TILE = D_FF
FF_ROWS = 256


def _ffn_up_kernel(h_ref, wa_ref, wb_ref, cwa_ref, cwb_ref, cba_ref, cbb_ref, sta_ref, stb_ref,
                   act_ref, csa_ref, csb_ref, prev_a, prev_b):
    @pl.when(pl.program_id(2) == 0)
    def _():
        prev_a[...] = sta_ref[...]
        prev_b[...] = stb_ref[...]

    h = h_ref[...].astype(BF16)
    tm = h.shape[0]
    row = lax.broadcasted_iota(jnp.int32, (tm, 1), 0)

    def half(w_ref, cw_ref, cb_ref, prev, cs_ref):
        u = _dot(h, w_ref[...])
        p = prev[...]
        back1 = pltpu.roll(u, 1, 0)
        back2 = pltpu.roll(u, 2, 0)
        u1 = jnp.where(row >= 1, back1, p[1:2])
        u2 = jnp.where(row >= 2, back2, jnp.where(row == 1, p[1:2], p[0:1]))
        y = cb_ref[...] + cw_ref[0:1] * u2 + cw_ref[1:2] * u1 + cw_ref[2:3] * u
        tail = back2[0:2]
        prev[...] = tail
        cs_ref[...] = tail
        return y

    a = half(wa_ref, cwa_ref, cba_ref, prev_a, csa_ref)
    b = half(wb_ref, cwb_ref, cbb_ref, prev_b, csb_ref)
    act_ref[...] = (a * jax.nn.sigmoid(a) * b).astype(act_ref.dtype)


def _ffn_up_call(h2, w_up_a, w_up_b, conv_w, conv_b, conv_state, nb, seq, tm, act_dtype):
    nrt = seq // tm
    nt = D_FF // FF_TILE
    return pl.pallas_call(
        _ffn_up_kernel,
        grid=(nt, nb, nrt),
        in_specs=[pl.BlockSpec((tm, D_MODEL), lambda j, b, i: (b * nrt + i, 0)),
                  pl.BlockSpec((D_MODEL, FF_TILE), lambda j, b, i: (0, j)),
                  pl.BlockSpec((D_MODEL, FF_TILE), lambda j, b, i: (0, j)),
                  pl.BlockSpec((3, FF_TILE), lambda j, b, i: (0, j)),
                  pl.BlockSpec((3, FF_TILE), lambda j, b, i: (0, nt + j)),
                  pl.BlockSpec((1, FF_TILE), lambda j, b, i: (0, j)),
                  pl.BlockSpec((1, FF_TILE), lambda j, b, i: (0, nt + j)),
                  pl.BlockSpec((None, 2, FF_TILE), lambda j, b, i: (b, 0, j)),
                  pl.BlockSpec((None, 2, FF_TILE), lambda j, b, i: (b, 0, nt + j))],
        out_specs=[pl.BlockSpec((tm, FF_TILE), lambda j, b, i: (b * nrt + i, j)),
                   pl.BlockSpec((None, 2, FF_TILE), lambda j, b, i: (b, 0, j)),
                   pl.BlockSpec((None, 2, FF_TILE), lambda j, b, i: (b, 0, j))],
        out_shape=[jax.ShapeDtypeStruct((nb * seq, D_FF), act_dtype),
                   jax.ShapeDtypeStruct((nb, 2, D_FF), F32),
                   jax.ShapeDtypeStruct((nb, 2, D_FF), F32)],
        scratch_shapes=[pltpu.VMEM((2, FF_TILE), F32), pltpu.VMEM((2, FF_TILE), F32)],
        compiler_params=_cparams("parallel", "parallel", "arbitrary"),
        name="ffn_up_conv",
    )(h2, w_up_a, w_up_b, conv_w, conv_w, conv_b, conv_b, conv_state, conv_state)


def _ffn_up_decode_kernel(h_ref, wa_ref, wb_ref, cwa_ref, cwb_ref, cba_ref, cbb_ref, sta_ref, stb_ref,
                          act_ref, csa_ref, csb_ref, *, nb, ts):
    h = h_ref[...].astype(BF16)
    t = lax.broadcasted_iota(jnp.int32, (1, ts, 1), 1)

    def half(w_ref, cw_ref, cb_ref, st_ref, cs_ref):
        u = _dot(h, w_ref[...]).reshape(nb, ts, FF_TILE)
        p = st_ref[...]
        back1 = pltpu.roll(u, 1, 1)
        back2 = pltpu.roll(u, 2, 1)
        u1 = jnp.where(t >= 1, back1, p[:, 1:2])
        u2 = jnp.where(t >= 2, back2, jnp.where(t == 1, p[:, 1:2], p[:, 0:1]))
        y = cb_ref[...] + cw_ref[0:1] * u2 + cw_ref[1:2] * u1 + cw_ref[2:3] * u
        cs_ref[...] = back2[:, 0:2]
        return y

    a = half(wa_ref, cwa_ref, cba_ref, sta_ref, csa_ref)
    b = half(wb_ref, cwb_ref, cbb_ref, stb_ref, csb_ref)
    act_ref[...] = (a * jax.nn.sigmoid(a) * b).reshape(nb * ts, FF_TILE)


def _ffn_up_decode_call(h2, w_up_a, w_up_b, conv_w, conv_b, conv_state, nb, ts):
    nt = D_FF // FF_TILE
    m = nb * ts
    return pl.pallas_call(
        functools.partial(_ffn_up_decode_kernel, nb=nb, ts=ts),
        grid=(nt,),
        in_specs=[pl.BlockSpec((m, D_MODEL), lambda j: (0, 0)),
                  pl.BlockSpec((D_MODEL, FF_TILE), lambda j: (0, j)),
                  pl.BlockSpec((D_MODEL, FF_TILE), lambda j: (0, j)),
                  pl.BlockSpec((3, FF_TILE), lambda j: (0, j)),
                  pl.BlockSpec((3, FF_TILE), lambda j: (0, nt + j)),
                  pl.BlockSpec((1, FF_TILE), lambda j: (0, j)),
                  pl.BlockSpec((1, FF_TILE), lambda j: (0, nt + j)),
                  pl.BlockSpec((nb, 2, FF_TILE), lambda j: (0, 0, j)),
                  pl.BlockSpec((nb, 2, FF_TILE), lambda j: (0, 0, nt + j))],
        out_specs=[pl.BlockSpec((m, FF_TILE), lambda j: (0, j)),
                   pl.BlockSpec((nb, 2, FF_TILE), lambda j: (0, 0, j)),
                   pl.BlockSpec((nb, 2, FF_TILE), lambda j: (0, 0, j))],
        out_shape=[jax.ShapeDtypeStruct((m, D_FF), F32),
                   jax.ShapeDtypeStruct((nb, 2, D_FF), F32),
                   jax.ShapeDtypeStruct((nb, 2, D_FF), F32)],
        compiler_params=_cparams("parallel"),
        name="ffn_up_conv_decode",
    )(h2, w_up_a, w_up_b, conv_w, conv_w, conv_b, conv_b, conv_state, conv_state)


def _ffn_down_kernel(a_ref, x1_ref, gt_ref, w_ref, gf_ref, y_ref, *, final_norm):
    x2 = x1_ref[...] + gt_ref[...] * _dot(a_ref[...].astype(BF16), w_ref[...])
    y_ref[...] = _rms(x2, gf_ref[...]) if final_norm else x2


def _ffn_down_call(act, x1, gt3, mod_idx, w_down, g_final, tm, final_norm):
    m = x1.shape[0]
    mrows = gt3.shape[1]
    return pl.pallas_call(
        functools.partial(_ffn_down_kernel, final_norm=final_norm),
        grid=(m // tm,),
        in_specs=[pl.BlockSpec((tm, D_FF), lambda i: (i, 0)),
                  pl.BlockSpec((tm, D_MODEL), lambda i: (i, 0)),
                  pl.BlockSpec((None, mrows, D_MODEL), lambda i: (mod_idx(i), 0, 0)),
                  pl.BlockSpec((D_FF, D_MODEL), lambda i: (0, 0)),
                  pl.BlockSpec((1, D_MODEL), lambda i: (0, 0))],
        out_specs=pl.BlockSpec((tm, D_MODEL), lambda i: (i, 0)),
        out_shape=jax.ShapeDtypeStruct((m, D_MODEL), F32),
        compiler_params=_cparams("parallel"),
        name="ffn_down",
    )(act, x1, gt3, w_down, g_final)


def _relayout_w_in(w_in):
    q_n, kv_n, gate_n, q_r, k_r, v_r, g_r = jnp.split(w_in, [512, 1280, 1304, 1560, 1816, 2328], axis=1)
    gate = gate_n.reshape(D_MODEL, 3, N_KV, REP)
    gate = jnp.transpose(gate, (0, 2, 1, 3)).reshape(D_MODEL, N_KV, 3 * REP)
    gate = jnp.pad(gate, ((0, 0), (0, 0), (0, LANES - 3 * REP))).reshape(D_MODEL, N_KV * LANES)
    return jnp.concatenate([q_n, kv_n, q_r, k_r, v_r, g_r, gate], axis=1).astype(BF16)


def _relayout_cmp(w1, pos):
    w1r = w1.reshape(2, CMP_STRIDE, HD, HD)
    w = jnp.einsum('atdn,gh->tgdahn', w1r, jnp.eye(N_KV, dtype=w1.dtype)).reshape(CMP_STRIDE * N_KV * HD, 2 * N_KV * HD)
    posr = pos.reshape(2, CMP_STRIDE, 1, HD)
    prow = jnp.broadcast_to(posr, (2, CMP_STRIDE, N_KV, HD)).reshape(2, CMP_STRIDE * N_KV * HD)
    prow = jnp.pad(prow, ((0, 14), (0, 0)))
    return w.astype(BF16), prow.astype(BF16)


def _block_diag4(w2k, w2v):
    z = jnp.zeros((HD, HD), w2k.dtype)
    rows = [[w2k, z, z, z], [z, w2k, z, z], [z, z, w2v, z], [z, z, z, w2v]]
    return jnp.block(rows).astype(BF16)


def _rope_tables(pos):
    half = HD // 2
    inv = ROPE_THETA ** (-jnp.arange(half, dtype=F32) / half)
    ang = pos.astype(F32)[:, None] * inv[None, :]
    cos, sin = jnp.cos(ang), jnp.sin(ang)
    return jnp.tile(cos, (1, 4)), jnp.tile(jnp.concatenate([-sin, sin], axis=1), (1, 2))


def _importance_matrix(nc, nsp):
    n = np.arange(nc)[:, None]
    d = n - 4 * np.arange(nsp)[None, :]
    m = ((d >= 0) & (d <= 3)).astype(np.float32) + ((d >= -1) & (d <= 2)).astype(np.float32)
    return jnp.asarray(m, dtype=BF16)


def _block_onehot_t(past):
    blk = np.arange(LANES)[:, None]
    return jnp.asarray((blk == np.arange(past)[None, :] // SEL_BLOCK).astype(np.float32), dtype=BF16)


def _round_up(x, m):
    return (x + m - 1) // m * m


def kernel(x_prompt, x_sample, c_prompt, c_sample, cache_nsa_kv, cache_win_kv, state_ret, state_conv, page_table,
           w_ada, b_ada, g_norm1, w_in, cmp_pos_k, cmp_w1_k, cmp_w2_k, cmp_pos_v, cmp_w1_v, cmp_w2_v,
           g_nsa_out, g_ret_out, w_out, g_norm2, w_up, conv_w, conv_b, w_down, g_final):
    nb_p, seq, _ = x_prompt.shape
    nb_s, ts, _ = x_sample.shape
    depth = w_ada.shape[0]
    n_pages = page_table.shape[1]
    past = n_pages * PAGE
    wb = cache_win_kv.shape[2]
    m_p, m_s = nb_p * seq, nb_s * ts
    tm_p = 512

    xp = x_prompt.reshape(m_p, D_MODEL)
    xs = x_sample.reshape(m_s, D_MODEL)
    n_c = nb_p + nb_s
    c_all = jnp.pad(jnp.concatenate([c_prompt, c_sample], axis=0), ((0, _round_up(n_c, 8) - n_c), (0, 0)))

    cos_p, sin_p = _rope_tables(jnp.arange(seq, dtype=jnp.int32))
    cos_s, sin_s = _rope_tables(jnp.tile(past + jnp.arange(ts, dtype=jnp.int32), nb_s))
    tabs_p = _ret_tables(min(RET_CHUNK, seq))
    tabs_s = _ret_tables(ts)
    nc_p, nc_s = seq // CMP_STRIDE, past // CMP_STRIDE
    nsp_p = _round_up(seq // SEL_BLOCK, LANES)
    nsp_s = _round_up(past // SEL_BLOCK + 1, LANES)
    mmat_p = _importance_matrix(nc_p, nsp_p)
    mmat_s, oht_s = _importance_matrix(nc_s, nsp_s), _block_onehot_t(past)

    tiles_per_batch = seq // tm_p
    idx_p = lambda i: i // tiles_per_batch
    tab_p = lambda i: i % tiles_per_batch
    idx_s = lambda i: 0

    outs = {k: [] for k in ('kv_p', 'kv_s', 'win_p', 'win_s', 'ret_p', 'ret_s', 'conv_p', 'conv_s')}
    for l in range(depth):
        mod = _mod_call(c_all, w_ada[l], b_ada[l])
        mods_p = [a.reshape(nb_p, 1, D_MODEL) for a in jnp.split(mod[:nb_p], 6, axis=1)]
        mods_s = [jnp.repeat(a, ts, axis=0).reshape(1, m_s, D_MODEL) for a in jnp.split(mod[nb_p:n_c], 6, axis=1)]
        w_in_re = _relayout_w_in(w_in[l])
        wk, pos_k = _relayout_cmp(cmp_w1_k[l], cmp_pos_k[l])
        wv, pos_v = _relayout_cmp(cmp_w1_v[l], cmp_pos_v[l])
        pos_kv = jnp.stack([pos_k, pos_v])
        w2bd = _block_diag4(cmp_w2_k[l], cmp_w2_v[l])
        g1 = g_norm1[l].reshape(1, -1)
        g2 = g_norm2[l].reshape(1, -1)
        g_nsa = g_nsa_out[l].reshape(1, -1)
        g_ret = g_ret_out[l].reshape(1, -1)
        w_out_bf = w_out[l].astype(BF16)
        w_up_a = w_up[l][:, :D_FF].astype(BF16)
        w_up_b = w_up[l][:, D_FF:].astype(BF16)
        w_down_bf = w_down[l].astype(BF16)

        sh1, sc1, gt1, sh2, sc2, gt2 = mods_p
        qh_p, kvn_t, kvw_t, kaug_p, vaug_p, kwin_p, vwaug_p, ret_p, gat_p, kvc_p = _inproj_call(
            xp, sc1, sh1, idx_p, g1, w_in_re, cos_p, sin_p, tab_p, tm_p, HD ** -0.5 * LOG2_E, tiles_per_batch)
        sh1s, sc1s, gt1s, sh2s, sc2s, gt2s = mods_s
        qh_s, kvn_s, kvw_s, kvh_s, ret_s, gat_s = _inproj_call(
            xs, sc1s, sh1s, idx_s, g1, w_in_re, cos_s, sin_s, idx_s, m_s, HD ** -0.5)

        ab_p = _cmp_prompt_call(kvc_p, wk, wv, pos_kv).reshape(nb_p, nc_p, 512)
        cache_t = jnp.transpose(cache_nsa_kv[:, l], (0, 2, 3, 4, 1)).reshape(cache_nsa_kv.shape[0], 512, PAGE)
        win_t = jnp.transpose(cache_win_kv[:, l], (0, 2, 3, 4, 1)).reshape(nb_s, 256, wb)
        ab_s, ks_s = _cmp_sample_call(cache_t, page_table, wk, wv, pos_kv)
        cmp_p = _cmp_fin_call(ab_p, w2bd)
        cmp_s = _cmp_fin_call(ab_s, w2bd)

        o_nsa_p = _nsa_prompt_call(qh_p, gat_p, cmp_p, kaug_p, vaug_p, kwin_p, vwaug_p, mmat_p, nb_p, seq, 0)
        o_nsa_s = _nsa_sample_call(qh_s, gat_s, cmp_s, ks_s, kvh_s, win_t, mmat_s, oht_s, nb_s, ts, past, 0)

        s0_p = jnp.zeros((nb_p, N_RET, DK_RET, DV_RET), F32)
        o_ret_p, s_new_p = _ret_call(ret_p, s0_p, tabs_p, g_ret, nb_p, seq, min(4 * RET_CHUNK, seq))
        o_ret_s, s_new_s = _ret_call(ret_s, state_ret[:, l], tabs_s, g_ret, nb_s, ts, ts)

        x1_p, h2_p = _outproj_call(o_nsa_p, o_ret_p, xp, gt1, sc2, sh2, idx_p, g_nsa, g2, w_out_bf, tm_p, BF16)
        x1_s, h2_s = _outproj_call(o_nsa_s, o_ret_s, xs, gt1s, sc2s, sh2s, idx_s, g_nsa, g2, w_out_bf, m_s, F32)

        conv0_p = jnp.zeros((nb_p, 2, 2 * D_FF), F32)
        act_p, csa_p, csb_p = _ffn_up_call(h2_p, w_up_a, w_up_b, conv_w[l], conv_b[l].reshape(1, -1), conv0_p,
                                           nb_p, seq, FF_ROWS, BF16)
        act_s, csa_s, csb_s = _ffn_up_decode_call(h2_s, w_up_a, w_up_b, conv_w[l], conv_b[l].reshape(1, -1),
                                                  state_conv[:, l], nb_s, ts)
        last = l == depth - 1
        gf = g_final.reshape(1, -1)
        xp = _ffn_down_call(act_p, x1_p, gt2, idx_p, w_down_bf, gf, tm_p, last)
        xs = _ffn_down_call(act_s, x1_s, gt2s, idx_s, w_down_bf, gf, m_s, last)

        outs['kv_p'].append(jnp.transpose(kvn_t.reshape(nb_p, 4, N_KV, HD, seq), (0, 4, 1, 2, 3)))
        outs['kv_s'].append(kvn_s.reshape(nb_s, ts, 4, N_KV, HD))
        keep_p = min(WINDOW, seq)
        win_t_p = kvw_t[:, :, seq - keep_p:].reshape(nb_p, 2, N_KV, HD, keep_p)
        outs['win_p'].append(jnp.transpose(win_t_p, (0, 4, 1, 2, 3)))
        win_all = jnp.concatenate([cache_win_kv[:, l], kvw_s.reshape(nb_s, ts, 2, N_KV, HD)], axis=1)
        outs['win_s'].append(win_all[:, win_all.shape[1] - min(WINDOW, past + ts):])
        outs['ret_p'].append(s_new_p)
        outs['ret_s'].append(s_new_s)
        outs['conv_p'].append(jnp.concatenate([csa_p, csb_p], axis=-1))
        outs['conv_s'].append(jnp.concatenate([csa_s, csb_s], axis=-1))

    st = lambda k: jnp.stack(outs[k], axis=1)
    return (xp.reshape(nb_p, seq, D_MODEL), xs.reshape(nb_s, ts, D_MODEL),
            st('kv_p'), st('kv_s'), st('win_p'), st('win_s'), st('ret_p'), st('ret_s'), st('conv_p'), st('conv_s'))
```
